```python
import math
import jax, jax.numpy as jnp
from jax import lax
import numpy as np

D_MODEL = 1024
BATCH = 8
SEQ = 2048
DEPTH = 2

D_MIX = D_MODEL
D_ATTN = D_MIX // 2
D_LRU = D_MIX - D_ATTN
HEAD_DIM = 64
N_ATTN_HEADS = D_ATTN // HEAD_DIM
N_LRU_BLOCKS = 8
LRU_BLOCK = D_LRU // N_LRU_BLOCKS
CONV_WIDTH = 4
RG_C = 8.0
MOBA_BLOCK = 256
MOBA_TOPK = 3
Q_CHUNK = 32
D_IN = 4 * D_ATTN + 2 * D_LRU
EPS = 1e-6

kernel_name = "hybrid_moba_rglru_adaln"


def rmsnorm(x, w):
    xf = x.astype(jnp.float32)
    y = xf * lax.rsqrt(jnp.mean(xf * xf, axis=-1, keepdims=True) + EPS)
    return (y * w.astype(jnp.float32)).astype(x.dtype)


def alibi_slopes(n_heads):
    return jnp.exp2(-8.0 * (jnp.arange(n_heads, dtype=jnp.float32) + 1.0) / n_heads)


def moba_attention(q, k, v):
    B, S, H, Dh = q.shape
    n_blk = -(-S // MOBA_BLOCK)
    s_pad = n_blk * MOBA_BLOCK
    k_eff = min(MOBA_TOPK, n_blk)
    pad = ((0, 0), (0, s_pad - S), (0, 0), (0, 0))
    q = jnp.transpose(q, (0, 2, 1, 3))
    k = jnp.transpose(jnp.pad(k, pad), (0, 2, 1, 3))
    v = jnp.transpose(jnp.pad(v, pad), (0, 2, 1, 3))
    k_blocks = k.reshape(B, H, n_blk, MOBA_BLOCK, Dh)
    v_blocks = v.reshape(B, H, n_blk, MOBA_BLOCK, Dh)
    k_mean = jnp.mean(k_blocks.astype(jnp.float32), axis=3)
    scale = Dh ** -0.5
    slopes = alibi_slopes(H)[None, :, None, None]
    blk_ids = jnp.arange(n_blk)
    offs = jnp.arange(MOBA_BLOCK)
    b_ix = jnp.arange(B)[:, None, None, None]
    h_ix = jnp.arange(H)[None, :, None, None]
    n_sel = k_eff * MOBA_BLOCK

    def chunk(ci):
        q0 = ci * Q_CHUNK
        qb = q0 // MOBA_BLOCK
        qc = lax.dynamic_slice_in_dim(q, q0, Q_CHUNK, axis=2).astype(jnp.float32)
        q_pos = q0 + jnp.arange(Q_CHUNK)
        gate = jnp.einsum('bhqd,bhnd->bhqn', qc, k_mean)
        gate = jnp.where(blk_ids < qb, gate, -jnp.inf)
        _, sel = lax.top_k(gate, k_eff)
        valid = sel < qb
        ks = k_blocks[b_ix, h_ix, sel].astype(jnp.float32)
        vs = v_blocks[b_ix, h_ix, sel].astype(jnp.float32)
        k_pos_sel = sel[..., None] * MOBA_BLOCK + offs
        dist_sel = (q_pos[None, None, :, None, None] - k_pos_sel).astype(jnp.float32)
        s_sel = jnp.einsum('bhqd,bhqkpd->bhqkp', qc, ks) * scale - slopes[..., None] * dist_sel
        s_sel = jnp.where(valid[..., None], s_sel, -jnp.inf).reshape(B, H, Q_CHUNK, n_sel)
        k_own = lax.dynamic_slice_in_dim(k, qb * MOBA_BLOCK, MOBA_BLOCK, axis=2).astype(jnp.float32)
        v_own = lax.dynamic_slice_in_dim(v, qb * MOBA_BLOCK, MOBA_BLOCK, axis=2).astype(jnp.float32)
        k_pos_own = qb * MOBA_BLOCK + offs
        dist_own = q_pos[:, None] - k_pos_own[None, :]
        s_own = jnp.einsum('bhqd,bhpd->bhqp', qc, k_own) * scale - slopes * dist_own.astype(jnp.float32)
        s_own = jnp.where(dist_own >= 0, s_own, -jnp.inf)
        p = jax.nn.softmax(jnp.concatenate([s_sel, s_own], axis=-1), axis=-1)
        p_sel = p[..., :n_sel].reshape(B, H, Q_CHUNK, k_eff, MOBA_BLOCK)
        p_own = p[..., n_sel:]
        out = (jnp.einsum('bhqkp,bhqkpd->bhqd', p_sel, vs)
               + jnp.einsum('bhqp,bhpd->bhqd', p_own, v_own))
        return out.astype(q.dtype)

    outs = lax.map(chunk, jnp.arange(S // Q_CHUNK))
    return jnp.transpose(outs, (1, 0, 3, 2, 4)).reshape(B, S, H * Dh)


def rg_lru(xl, conv_w, conv_b, a_w, a_b, x_w, x_b, lam):
    B, S, C = xl.shape
    xc = lax.conv_general_dilated(
        xl, conv_w[:, None, :].astype(xl.dtype), window_strides=(1,),
        padding=[(CONV_WIDTH - 1, 0)], dimension_numbers=('NWC', 'WIO', 'NWC'),
        feature_group_count=C) + conv_b
    xg = xc.reshape(B, S, N_LRU_BLOCKS, LRU_BLOCK)
    r = jax.nn.sigmoid(jnp.einsum('bsgi,gij->bsgj', xg, a_w).reshape(B, S, C) + a_b)
    i = jax.nn.sigmoid(jnp.einsum('bsgi,gij->bsgj', xg, x_w).reshape(B, S, C) + x_b)
    log_a = -RG_C * r.astype(jnp.float32) * jax.nn.softplus(-lam.astype(jnp.float32))
    a = jnp.exp(log_a)
    b = jnp.sqrt(-jnp.expm1(2.0 * log_a)) * (i * xc).astype(jnp.float32)

    def combine(left, right):
        a_l, b_l = left
        a_r, b_r = right
        return a_l * a_r, a_r * b_l + b_r

    _, h = lax.associative_scan(combine, (a, b), axis=1)
    return h.astype(xl.dtype)


def hybrid_layer(x, c, norm_w, ada_w, ada_b, w_in, conv_w, conv_b, rg_a_w, rg_a_b,
                 rg_x_w, rg_x_b, rg_lambda, attn_out_norm_w, lru_out_norm_w, w_out):
    B, S, _ = x.shape
    mod = jax.nn.silu(c) @ ada_w + ada_b
    shift, scale, gate = jnp.split(mod[:, None, :], 3, axis=-1)
    h = rmsnorm(x, norm_w) * (1.0 + scale) + shift
    proj = h @ w_in
    q, k, v, z_attn, x_lru, z_lru = jnp.split(
        proj, [D_ATTN, 2 * D_ATTN, 3 * D_ATTN, 4 * D_ATTN, 4 * D_ATTN + D_LRU], axis=-1)
    hs = (B, S, N_ATTN_HEADS, HEAD_DIM)
    y_attn = moba_attention(q.reshape(hs), k.reshape(hs), v.reshape(hs))
    y_attn = rmsnorm(y_attn, attn_out_norm_w) * jax.nn.silu(z_attn)
    y_lru = rg_lru(x_lru, conv_w, conv_b, rg_a_w, rg_a_b, rg_x_w, rg_x_b, rg_lambda)
    y_lru = rmsnorm(y_lru, lru_out_norm_w) * jax.nn.silu(z_lru)
    y = jnp.concatenate([y_attn, y_lru], axis=-1) @ w_out
    return x + gate * y


def setup_inputs(seed: int = 0) -> dict:
    key = jax.random.key(seed)
    ks = jax.random.split(key, 20)
    f32 = jnp.float32
    L = DEPTH
    nrm = lambda k, shape, s: jax.random.normal(k, shape, f32) * s
    u = jax.random.uniform(ks[13], (L, D_LRU), f32, 0.9, 0.999)
    a0 = u ** (1.0 / RG_C)
    return {
        "x": jax.random.normal(ks[0], (BATCH, SEQ, D_MODEL), f32),
        "c": jax.random.normal(ks[1], (BATCH, D_MODEL), f32),
        "norm_w": 1.0 + nrm(ks[2], (L, D_MODEL), 0.05),
        "ada_w": nrm(ks[3], (L, D_MODEL, 3 * D_MODEL), D_MODEL ** -0.5),
        "ada_b": nrm(ks[4], (L, 3 * D_MODEL), 0.01),
        "w_in": nrm(ks[5], (L, D_MODEL, D_IN), D_MODEL ** -0.5),
        "conv_w": nrm(ks[6], (L, CONV_WIDTH, D_LRU), CONV_WIDTH ** -0.5),
        "conv_b": nrm(ks[7], (L, D_LRU), 0.01),
        "rg_a_w": nrm(ks[8], (L, N_LRU_BLOCKS, LRU_BLOCK, LRU_BLOCK), LRU_BLOCK ** -0.5),
        "rg_a_b": nrm(ks[9], (L, D_LRU), 0.01),
        "rg_x_w": nrm(ks[10], (L, N_LRU_BLOCKS, LRU_BLOCK, LRU_BLOCK), LRU_BLOCK ** -0.5),
        "rg_x_b": nrm(ks[11], (L, D_LRU), 0.01),
        "rg_lambda": jnp.log(a0) - jnp.log1p(-a0),
        "attn_out_norm_w": 1.0 + nrm(ks[14], (L, D_ATTN), 0.05),
        "lru_out_norm_w": 1.0 + nrm(ks[15], (L, D_LRU), 0.05),
        "w_out": nrm(ks[16], (L, D_MIX, D_MODEL), D_MIX ** -0.5),
        "final_norm_w": 1.0 + nrm(ks[17], (D_MODEL,), 0.05),
    }


def reference(x, c, norm_w, ada_w, ada_b, w_in, conv_w, conv_b, rg_a_w, rg_a_b, rg_x_w,
              rg_x_b, rg_lambda, attn_out_norm_w, lru_out_norm_w, w_out, final_norm_w):
    for l in range(DEPTH):
        x = hybrid_layer(x, c, norm_w[l], ada_w[l], ada_b[l], w_in[l], conv_w[l], conv_b[l],
                         rg_a_w[l], rg_a_b[l], rg_x_w[l], rg_x_b[l], rg_lambda[l],
                         attn_out_norm_w[l], lru_out_norm_w[l], w_out[l])
    return rmsnorm(x, final_norm_w)
```

```python
import functools
import math

import jax
import jax.numpy as jnp
from jax import lax
from jax.experimental import pallas as pl
from jax.experimental.pallas import tpu as pltpu

D_MODEL = 1024
D_ATTN = 512
D_LRU = 512
HEAD_DIM = 64
N_HEADS = D_ATTN // HEAD_DIM
N_LRU_BLOCKS = 8
LRU_BLOCK = D_LRU // N_LRU_BLOCKS
CONV_WIDTH = 4
RG_C = 8.0
MOBA_BLOCK = 256
MOBA_TOPK = 3
D_IN = 4 * D_ATTN + 2 * D_LRU
EPS = 1e-6

F32 = jnp.float32
BF16 = jnp.bfloat16
NEG_BIG = -1e30

LANES = 128
SUBLANES = 8
VMEM_LIMIT_BYTES = 48 * 1024 * 1024

ROW_TILE = 512
LRU_CHUNK = 256
HEADS_PER_STEP = LANES // HEAD_DIM
AUG = 2 * LANES


def _silu(v):
    return v * jax.nn.sigmoid(v)


def _mod_kernel(c_ref, w_ref, b_ref, o_ref):
    s = _silu(c_ref[...]).astype(BF16)
    w = w_ref[0].astype(BF16)
    o_ref[0] = jnp.dot(s, w, preferred_element_type=F32) + b_ref[0]


def _mod_call(c, ada_w, ada_b):
    n_layers, d, d3 = ada_w.shape
    b = c.shape[0]
    tn = 1024
    return pl.pallas_call(
        _mod_kernel,
        grid=(n_layers, d3 // tn),
        in_specs=[
            pl.BlockSpec((b, d), lambda l, j: (0, 0)),
            pl.BlockSpec((1, d, tn), lambda l, j: (l, 0, j)),
            pl.BlockSpec((1, 1, tn), lambda l, j: (l, 0, j)),
        ],
        out_specs=pl.BlockSpec((1, b, tn), lambda l, j: (l, 0, j)),
        out_shape=jax.ShapeDtypeStruct((n_layers, b, d3), F32),
        compiler_params=pltpu.CompilerParams(
            dimension_semantics=("arbitrary", "arbitrary"), vmem_limit_bytes=VMEM_LIMIT_BYTES),
        name="adaln_mod",
    )(c, ada_w, ada_b.reshape(n_layers, 1, d3))


def _inproj_kernel(x_ref, mod_ref, nw_ref, w_ref, q_ref, k_ref, v_ref, za_ref, xl_ref, zl_ref):
    x = x_ref[0]
    ms = jnp.mean(x * x, axis=-1, keepdims=True)
    y = x * lax.rsqrt(ms + EPS) * nw_ref[...]
    shift = mod_ref[0, 0:1, :]
    scale = mod_ref[0, 1:2, :]
    h = (y * (1.0 + scale) + shift).astype(BF16)
    outs = (q_ref, k_ref, v_ref, za_ref, xl_ref, zl_ref)
    for j, ref in enumerate(outs):
        r = jnp.dot(h, w_ref[:, j * D_ATTN:(j + 1) * D_ATTN], preferred_element_type=F32)
        if j == 0:
            r = r * (HEAD_DIM ** -0.5)
        ref[0] = r.astype(ref.dtype)


def _inproj_call(x, mod3, norm_w, w_in_bf):
    b, s, d = x.shape
    tm = ROW_TILE
    blk = lambda dt: jax.ShapeDtypeStruct((b, s, D_ATTN), dt)
    ospec = pl.BlockSpec((1, tm, D_ATTN), lambda i, j: (i, j, 0))
    return pl.pallas_call(
        _inproj_kernel,
        grid=(b, s // tm),
        in_specs=[
            pl.BlockSpec((1, tm, d), lambda i, j: (i, j, 0)),
            pl.BlockSpec((1, 3, d), lambda i, j: (i, 0, 0)),
            pl.BlockSpec((1, d), lambda i, j: (0, 0)),
            pl.BlockSpec((d, D_IN), lambda i, j: (0, 0)),
        ],
        out_specs=[ospec] * 6,
        out_shape=[blk(BF16), blk(BF16), blk(BF16), blk(F32), blk(F32), blk(F32)],
        compiler_params=pltpu.CompilerParams(
            dimension_semantics=("arbitrary", "arbitrary"), vmem_limit_bytes=VMEM_LIMIT_BYTES),
        name="inproj",
    )(x, mod3, norm_w.reshape(1, d), w_in_bf)


def _attn_kernel(slopes_ref, q_ref, k_ref, v_ref, o_ref):
    s_len = q_ref.shape[1]
    n_blk = s_len // MOBA_BLOCK
    pair = pl.program_id(1)

    kp = k_ref[0]
    qT = q_ref[0].astype(F32).T
    vT = v_ref[0].astype(F32).T

    prow = lax.broadcasted_iota(jnp.int32, (n_blk, s_len), 0)
    pcol = lax.broadcasted_iota(jnp.int32, (n_blk, s_len), 1)
    pm = jnp.where(pcol // MOBA_BLOCK == prow, 1.0 / MOBA_BLOCK, 0.0).astype(BF16)
    kmean = jnp.dot(pm, kp, preferred_element_type=F32)

    krow = lax.broadcasted_iota(jnp.int32, (s_len, LANES), 0)
    klane = lax.broadcasted_iota(jnp.int32, (s_len, LANES), 1)
    koff = (krow % MOBA_BLOCK).astype(F32)
    kext = jnp.where(klane == krow // MOBA_BLOCK, 1.0, 0.0)
    for hd in range(HEADS_PER_STEP):
        slope = slopes_ref[pair * HEADS_PER_STEP + hd]
        kext = jnp.where(klane == n_blk + hd, slope * koff, kext)
    k_aug = jnp.concatenate([kp, kext.astype(BF16)], axis=1)

    n_iota = lax.broadcasted_iota(jnp.int32, (n_blk, s_len), 0)
    q_blk = lax.broadcasted_iota(jnp.int32, (n_blk, s_len), 1) // MOBA_BLOCK
    drow = lax.broadcasted_iota(jnp.int32, (LANES, s_len), 0)
    mlane = lax.broadcasted_iota(jnp.int32, (n_blk, LANES), 1)
    key_off = lax.broadcasted_iota(jnp.int32, (MOBA_BLOCK, MOBA_BLOCK), 0)
    qry_off = lax.broadcasted_iota(jnp.int32, (MOBA_BLOCK, MOBA_BLOCK), 1)
    causal = key_off <= qry_off

    head_out = []
    for hd in range(HEADS_PER_STEP):
        slope = slopes_ref[pair * HEADS_PER_STEP + hd]
        kmean_h = jnp.where(mlane // HEAD_DIM == hd, kmean, 0.0)
        g_t = jnp.dot(kmean_h, qT, preferred_element_type=F32, precision=lax.Precision.HIGHEST)
        rank = jnp.zeros((n_blk, s_len), jnp.int32)
        for m in range(n_blk):
            gm = g_t[m:m + 1, :]
            beats = (gm > g_t) | ((gm == g_t) & (m < n_iota))
            rank = rank + jnp.where(beats & (m < q_blk), 1, 0)
        sel = ((n_iota < q_blk) & (rank < MOBA_TOPK)) | (n_iota == q_blk)
        blk_bias = slope * ((n_iota - q_blk) * MOBA_BLOCK).astype(F32)
        selb = jnp.where(sel, blk_bias, NEG_BIG)
        flag = jnp.where(n_iota == hd, 1.0, 0.0)
        q_aug_t = jnp.concatenate(
            [jnp.where(drow // HEAD_DIM == hd, qT, 0.0), selb, flag,
             jnp.zeros((AUG - LANES - 2 * n_blk, s_len), F32)], axis=0).astype(BF16)
        v_aug_t = jnp.concatenate(
            [vT[hd * HEAD_DIM:(hd + 1) * HEAD_DIM], jnp.ones((2 * SUBLANES, s_len), F32)],
            axis=0).astype(BF16)

        tiles = []
        for qb in range(n_blk):
            q0 = qb * MOBA_BLOCK
            qa = q_aug_t[:, q0:q0 + MOBA_BLOCK]
            s_diag = jnp.dot(k_aug[q0:q0 + MOBA_BLOCK], qa, preferred_element_type=F32)
            s_diag = jnp.where(causal, s_diag, NEG_BIG)
            m = jnp.max(s_diag, axis=0, keepdims=True)
            if qb > 0:
                s_past = jnp.dot(k_aug[0:q0], qa, preferred_element_type=F32)
                m = jnp.maximum(m, jnp.max(s_past, axis=0, keepdims=True))
            p_diag = jnp.exp(s_diag - m).astype(BF16)
            o = jnp.dot(v_aug_t[:, q0:q0 + MOBA_BLOCK], p_diag, preferred_element_type=F32)
            if qb > 0:
                p_past = jnp.exp(s_past - m).astype(BF16)
                o = o + jnp.dot(v_aug_t[:, 0:q0], p_past, preferred_element_type=F32)
            tiles.append(o[0:HEAD_DIM] / o[HEAD_DIM:HEAD_DIM + 1])
        head_out.append(jnp.concatenate(tiles, axis=1))
    o_ref[0] = jnp.concatenate(head_out, axis=0).T


def _attn_call(slopes, q, k, v):
    b, s, _ = q.shape
    spec = pl.BlockSpec((1, s, LANES), lambda i, j, sl: (i, 0, j))
    return pl.pallas_call(
        _attn_kernel,
        grid_spec=pltpu.PrefetchScalarGridSpec(
            num_scalar_prefetch=1,
            grid=(b, D_ATTN // LANES),
            in_specs=[spec, spec, spec],
            out_specs=spec,
        ),
        out_shape=jax.ShapeDtypeStruct((b, s, D_ATTN), F32),
        compiler_params=pltpu.CompilerParams(
            dimension_semantics=("arbitrary", "arbitrary"), vmem_limit_bytes=VMEM_LIMIT_BYTES),
        name="moba_attn",
    )(slopes, q, k, v)


def _lru_kernel(xl_ref, zl_ref, cw_ref, cb_ref, w0_ref, w1_ref, ab_ref, xb_ref, lam_ref, nw_ref,
                o_ref, xpad_scr, h_scr):
    s_len = xl_ref.shape[1]
    tc = LRU_CHUNK
    groups = tc // SUBLANES
    half = D_LRU // 2

    xpad_scr[0:SUBLANES, :] = jnp.zeros((SUBLANES, D_LRU), F32)
    xpad_scr[SUBLANES:, :] = xl_ref[0]

    neg_lam = -lam_ref[...]
    softplus = jnp.maximum(neg_lam, 0.0) + jnp.log1p(jnp.exp(-jnp.abs(neg_lam)))
    sub = lax.broadcasted_iota(jnp.int32, (groups, SUBLANES, D_LRU), 1)

    def chunk(ci, h_prev):
        t0 = pl.multiple_of(ci * tc, tc)
        xw = xpad_scr[pl.ds(t0, tc + SUBLANES), :]
        xc = cb_ref[...]
        for j in range(CONV_WIDTH):
            lo = SUBLANES - (CONV_WIDTH - 1) + j
            xc = xc + cw_ref[j:j + 1, :] * xw[lo:lo + tc]
        xcb = xc.astype(BF16)
        g0 = jnp.dot(xcb[:, :half], w0_ref[...], preferred_element_type=F32)
        g1 = jnp.dot(xcb[:, half:], w1_ref[...], preferred_element_type=F32)
        r = jax.nn.sigmoid(jnp.concatenate([g0[:, :half], g1[:, :half]], axis=1) + ab_ref[...])
        gi = jax.nn.sigmoid(jnp.concatenate([g0[:, half:], g1[:, half:]], axis=1) + xb_ref[...])
        log_a = -RG_C * r * softplus
        a = jnp.exp(log_a)
        bterm = jnp.sqrt(-jnp.tanh(log_a) * (a * a + 1.0)) * (gi * xc)

        av = a.reshape(groups, SUBLANES, D_LRU)
        bv = bterm.reshape(groups, SUBLANES, D_LRU)
        for step in (1, 2, 4):
            a_sh = jnp.where(sub >= step, pltpu.roll(av, step, axis=1), 1.0)
            b_sh = jnp.where(sub >= step, pltpu.roll(bv, step, axis=1), 0.0)
            bv = av * b_sh + bv
            av = av * a_sh
        h = h_prev
        for g in range(groups):
            h_last = jnp.broadcast_to(h[SUBLANES - 1:SUBLANES, :], (SUBLANES, D_LRU))
            h = bv[g] + av[g] * h_last
            h_scr[g * SUBLANES:(g + 1) * SUBLANES, :] = h
        hh = h_scr[...]
        ms = jnp.mean(hh * hh, axis=-1, keepdims=True)
        yn = hh * lax.rsqrt(ms + EPS) * nw_ref[...]
        o_ref[0, pl.ds(t0, tc), :] = (yn * _silu(zl_ref[0, pl.ds(t0, tc), :])).astype(o_ref.dtype)
        return h

    lax.fori_loop(0, s_len // tc, chunk, jnp.zeros((SUBLANES, D_LRU), F32))


def _lru_call(xl, zl, conv_w, conv_b, w0, w1, a_b, x_b, lam, nw):
    b, s, c = xl.shape
    row = lambda a: a.reshape(1, c)
    seq = pl.BlockSpec((1, s, c), lambda i: (i, 0, 0))
    vec = pl.BlockSpec((1, c), lambda i: (0, 0))
    wspec = pl.BlockSpec(w0.shape, lambda i: (0, 0))
    return pl.pallas_call(
        _lru_kernel,
        grid=(b,),
        in_specs=[seq, seq, pl.BlockSpec((CONV_WIDTH, c), lambda i: (0, 0)), vec, wspec, wspec,
                  vec, vec, vec, vec],
        out_specs=seq,
        out_shape=jax.ShapeDtypeStruct((b, s, c), BF16),
        scratch_shapes=[pltpu.VMEM((s + SUBLANES, c), F32), pltpu.VMEM((LRU_CHUNK, c), F32)],
        compiler_params=pltpu.CompilerParams(
            dimension_semantics=("arbitrary",), vmem_limit_bytes=VMEM_LIMIT_BYTES),
        name="rg_lru",
    )(xl, zl, conv_w, row(conv_b), w0, w1, row(a_b), row(x_b), row(lam), row(nw))


def _outproj_kernel(ya_ref, za_ref, yl_ref, x_ref, mod_ref, anw_ref, w_ref, fnw_ref, o_ref, *, last):
    ya = ya_ref[0]
    ms = jnp.mean(ya * ya, axis=-1, keepdims=True)
    ya = (ya * lax.rsqrt(ms + EPS) * anw_ref[...]) * _silu(za_ref[0])
    y = jnp.dot(ya.astype(BF16), w_ref[0:D_ATTN, :], preferred_element_type=F32)
    y = y + jnp.dot(yl_ref[0], w_ref[D_ATTN:, :], preferred_element_type=F32)
    out = x_ref[0] + mod_ref[0, 2:3, :] * y
    if last:
        ms2 = jnp.mean(out * out, axis=-1, keepdims=True)
        out = out * lax.rsqrt(ms2 + EPS) * fnw_ref[...]
    o_ref[0] = out


def _outproj_call(ya, za, yl, x, mod3, anw, w_out_bf, fnw, last):
    b, s, d = x.shape
    tm = ROW_TILE
    half = pl.BlockSpec((1, tm, D_ATTN), lambda i, j: (i, j, 0))
    full = pl.BlockSpec((1, tm, d), lambda i, j: (i, j, 0))
    return pl.pallas_call(
        functools.partial(_outproj_kernel, last=last),
        grid=(b, s // tm),
        in_specs=[
            half, half, half, full,
            pl.BlockSpec((1, 3, d), lambda i, j: (i, 0, 0)),
            pl.BlockSpec((1, D_ATTN), lambda i, j: (0, 0)),
            pl.BlockSpec((d, d), lambda i, j: (0, 0)),
            pl.BlockSpec((1, d), lambda i, j: (0, 0)),
        ],
        out_specs=full,
        out_shape=jax.ShapeDtypeStruct((b, s, d), F32),
        compiler_params=pltpu.CompilerParams(
            dimension_semantics=("arbitrary", "arbitrary"), vmem_limit_bytes=VMEM_LIMIT_BYTES),
        name="outproj",
    )(ya, za, yl, x, mod3, anw.reshape(1, D_ATTN), w_out_bf, fnw.reshape(1, d))


def _gate_weights(a_w, x_w):
    def half(w):
        return jax.scipy.linalg.block_diag(*[w[g] for g in range(w.shape[0])])
    nb = N_LRU_BLOCKS // 2
    w0 = jnp.concatenate([half(a_w[:nb]), half(x_w[:nb])], axis=1)
    w1 = jnp.concatenate([half(a_w[nb:]), half(x_w[nb:])], axis=1)
    return w0.astype(BF16), w1.astype(BF16)


def kernel(x, c, norm_w, ada_w, ada_b, w_in, conv_w, conv_b, rg_a_w, rg_a_b, rg_x_w, rg_x_b,
           rg_lambda, attn_out_norm_w, lru_out_norm_w, w_out, final_norm_w):
    n_layers = ada_w.shape[0]
    b = x.shape[0]
    mod = _mod_call(c, ada_w, ada_b)
    slopes = jnp.exp2(-8.0 * (jnp.arange(N_HEADS, dtype=F32) + 1.0) / N_HEADS)
    w_in_bf = w_in.astype(BF16)
    w_out_bf = w_out.astype(BF16)
    for l in range(n_layers):
        mod3 = mod[l].reshape(b, 3, D_MODEL)
        q, k, v, za, xl, zl = _inproj_call(x, mod3, norm_w[l], w_in_bf[l])
        ya = _attn_call(slopes, q, k, v)
        w0, w1 = _gate_weights(rg_a_w[l], rg_x_w[l])
        yl = _lru_call(xl, zl, conv_w[l], conv_b[l], w0, w1, rg_a_b[l], rg_x_b[l],
                       rg_lambda[l], lru_out_norm_w[l])
        x = _outproj_call(ya, za, yl, x, mod3, attn_out_norm_w[l], w_out_bf[l], final_norm_w,
                          last=(l == n_layers - 1))
    return x
```

```python
import functools
import math

import jax
import jax.numpy as jnp
from jax import lax
from jax.experimental import pallas as pl
from jax.experimental.pallas import tpu as pltpu

D_MODEL = 1024
D_ATTN = 512
D_LRU = 512
HEAD_DIM = 64
N_HEADS = D_ATTN // HEAD_DIM
N_LRU_BLOCKS = 8
LRU_BLOCK = D_LRU // N_LRU_BLOCKS
CONV_WIDTH = 4
RG_C = 8.0
MOBA_BLOCK = 256
MOBA_TOPK = 3
D_IN = 4 * D_ATTN + 2 * D_LRU
EPS = 1e-6

F32 = jnp.float32
BF16 = jnp.bfloat16
NEG_BIG = -1e30
LOG2E = math.log2(math.e)

LANES = 128
SUBLANES = 8
VMEM_LIMIT_BYTES = 48 * 1024 * 1024

ROW_TILE = 512
LRU_CHUNK = 256
HEADS_PER_STEP = LANES // HEAD_DIM
AUG = 2 * LANES


def _sigmoid(v):
    return 0.5 * jnp.tanh(0.5 * v) + 0.5


def _silu(v):
    return v * _sigmoid(v)


def _split_bf16(v):
    hi = v.astype(BF16).astype(F32)
    lo = (v - hi).astype(BF16).astype(F32)
    return hi, lo


def _mod_kernel(c_ref, w_ref, b_ref, o_ref):
    s = _silu(c_ref[...]).astype(BF16)
    w = w_ref[0].astype(BF16)
    o_ref[0] = jnp.dot(s, w, preferred_element_type=F32) + b_ref[0]


def _mod_call(c, ada_w, ada_b):
    n_layers, d, d3 = ada_w.shape
    b = c.shape[0]
    tn = 1024
    return pl.pallas_call(
        _mod_kernel,
        grid=(n_layers, d3 // tn),
        in_specs=[
            pl.BlockSpec((b, d), lambda l, j: (0, 0)),
            pl.BlockSpec((1, d, tn), lambda l, j: (l, 0, j)),
            pl.BlockSpec((1, 1, tn), lambda l, j: (l, 0, j)),
        ],
        out_specs=pl.BlockSpec((1, b, tn), lambda l, j: (l, 0, j)),
        out_shape=jax.ShapeDtypeStruct((n_layers, b, d3), F32),
        compiler_params=pltpu.CompilerParams(
            dimension_semantics=("arbitrary", "arbitrary"), vmem_limit_bytes=VMEM_LIMIT_BYTES),
        name="adaln_mod",
    )(c, ada_w, ada_b.reshape(n_layers, 1, d3))


def _inproj_kernel(x_ref, mod_ref, nw_ref, w_ref, q_ref, k_ref, v_ref, za_ref, xl_ref, zl_ref):
    x = x_ref[0]
    ms = jnp.mean(x * x, axis=-1, keepdims=True)
    y = x * lax.rsqrt(ms + EPS) * nw_ref[...]
    shift = mod_ref[0, 0:1, :]
    scale = mod_ref[0, 1:2, :]
    h = (y * (1.0 + scale) + shift).astype(BF16)
    outs = (q_ref, k_ref, v_ref, za_ref, xl_ref, zl_ref)
    for j, ref in enumerate(outs):
        r = jnp.dot(h, w_ref[:, j * D_ATTN:(j + 1) * D_ATTN], preferred_element_type=F32)
        if j == 0:
            r = r * (HEAD_DIM ** -0.5 * LOG2E)
        ref[0] = r.astype(ref.dtype)


def _inproj_call(x, mod3, norm_w, w_in_bf):
    b, s, d = x.shape
    tm = ROW_TILE
    blk = lambda dt: jax.ShapeDtypeStruct((b, s, D_ATTN), dt)
    ospec = pl.BlockSpec((1, tm, D_ATTN), lambda i, j: (i, j, 0))
    return pl.pallas_call(
        _inproj_kernel,
        grid=(b, s // tm),
        in_specs=[
            pl.BlockSpec((1, tm, d), lambda i, j: (i, j, 0)),
            pl.BlockSpec((1, 3, d), lambda i, j: (i, 0, 0)),
            pl.BlockSpec((1, d), lambda i, j: (0, 0)),
            pl.BlockSpec((d, D_IN), lambda i, j: (0, 0)),
        ],
        out_specs=[ospec] * 6,
        out_shape=[blk(BF16), blk(BF16), blk(BF16), blk(F32), blk(F32), blk(F32)],
        compiler_params=pltpu.CompilerParams(
            dimension_semantics=("arbitrary", "arbitrary"), vmem_limit_bytes=VMEM_LIMIT_BYTES),
        name="inproj",
    )(x, mod3, norm_w.reshape(1, d), w_in_bf)


def _attn_kernel(slopes_ref, q_ref, k_ref, v_ref, o_ref,
                 kaug_scr, qaug_scr, vaug_scr, s_scr, p_scr, out_scr):
    s_len = q_ref.shape[1]
    n_blk = s_len // MOBA_BLOCK
    pair = pl.program_id(1)

    kp = k_ref[0]
    qT = q_ref[0].astype(F32).T
    vT = v_ref[0].astype(F32).T

    prow = lax.broadcasted_iota(jnp.int32, (n_blk, s_len), 0)
    pcol = lax.broadcasted_iota(jnp.int32, (n_blk, s_len), 1)
    pm = jnp.where(pcol // MOBA_BLOCK == prow, 1.0 / MOBA_BLOCK, 0.0).astype(BF16)
    kmean = jnp.dot(pm, kp, preferred_element_type=F32)

    krow = lax.broadcasted_iota(jnp.int32, (s_len, LANES), 0)
    klane = lax.broadcasted_iota(jnp.int32, (s_len, LANES), 1)
    koff = (krow % MOBA_BLOCK).astype(F32)
    kext = jnp.where(klane % n_blk == krow // MOBA_BLOCK, 1.0, 0.0)
    kext = jnp.where(klane < 2 * n_blk, kext, 0.0)
    for hd in range(HEADS_PER_STEP):
        slope2 = slopes_ref[pair * HEADS_PER_STEP + hd] * LOG2E
        off_hi, off_lo = _split_bf16(slope2 * koff)
        kext = jnp.where(klane == 2 * n_blk + 2 * hd, off_hi, kext)
        kext = jnp.where(klane == 2 * n_blk + 2 * hd + 1, off_lo, kext)
    kaug_scr[:, 0:LANES] = kp
    kaug_scr[:, LANES:AUG] = kext.astype(BF16)

    n_iota = lax.broadcasted_iota(jnp.int32, (n_blk, s_len), 0)
    q_blk = lax.broadcasted_iota(jnp.int32, (n_blk, s_len), 1) // MOBA_BLOCK
    drow = lax.broadcasted_iota(jnp.int32, (LANES, s_len), 0)
    mlane = lax.broadcasted_iota(jnp.int32, (n_blk, LANES), 1)
    key_off = lax.broadcasted_iota(jnp.int32, (MOBA_BLOCK, MOBA_BLOCK), 0)
    qry_off = lax.broadcasted_iota(jnp.int32, (MOBA_BLOCK, MOBA_BLOCK), 1)
    causal = key_off <= qry_off

    for hd in range(HEADS_PER_STEP):
        slope2 = slopes_ref[pair * HEADS_PER_STEP + hd] * LOG2E
        kmean_h = jnp.where(mlane // HEAD_DIM == hd, kmean, 0.0)
        g_t = jnp.dot(kmean_h, qT, preferred_element_type=F32, precision=lax.Precision.HIGHEST)
        rank = jnp.zeros((n_blk, s_len), jnp.int32)
        for m in range(n_blk):
            gm = g_t[m:m + 1, :]
            beats = (gm > g_t) | ((gm == g_t) & (m < n_iota))
            rank = rank + jnp.where(beats & (m < q_blk), 1, 0)
        sel = ((n_iota < q_blk) & (rank < MOBA_TOPK)) | (n_iota == q_blk)
        bias_hi, bias_lo = _split_bf16(slope2 * ((n_iota - q_blk) * MOBA_BLOCK).astype(F32))
        selb_hi = jnp.where(sel, bias_hi, NEG_BIG)
        selb_lo = jnp.where(sel, bias_lo, 0.0)
        flag = jnp.where(n_iota // 2 == hd, 1.0, 0.0)
        qaug_scr[hd] = jnp.concatenate(
            [jnp.where(drow // HEAD_DIM == hd, qT, 0.0), selb_hi, selb_lo, flag,
             jnp.zeros((AUG - LANES - 3 * n_blk, s_len), F32)], axis=0).astype(BF16)
        vaug_scr[hd] = jnp.concatenate(
            [vT[hd * HEAD_DIM:(hd + 1) * HEAD_DIM], jnp.ones((2 * SUBLANES, s_len), F32)],
            axis=0).astype(BF16)

    items = [(hd, qb) for hd in range(HEADS_PER_STEP) for qb in range(n_blk)]

    def scores(i):
        hd, qb = items[i]
        slot = i % 2
        q0, kk = qb * MOBA_BLOCK, (qb + 1) * MOBA_BLOCK
        qa = qaug_scr[hd, :, q0:kk]
        m = None
        for r0, r1 in ((0, kk // 2), (kk // 2, kk)):
            s = jnp.dot(kaug_scr[r0:r1, :], qa, preferred_element_type=F32)
            if r1 > q0:
                d0 = max(r0, q0)
                masked = jnp.where(causal[d0 - q0:r1 - q0], s[d0 - r0:], NEG_BIG)
                s = masked if d0 == r0 else jnp.concatenate([s[0:d0 - r0], masked], axis=0)
            s_scr[slot, r0:r1, :] = s
            mh = jnp.max(s, axis=0, keepdims=True)
            m = mh if m is None else jnp.maximum(m, mh)
        return m

    def probs(i, m):
        hd, qb = items[i]
        slot = i % 2
        q0, kk = qb * MOBA_BLOCK, (qb + 1) * MOBA_BLOCK
        p_scr[slot, 0:kk, :] = jnp.exp2(s_scr[slot, 0:kk, :] - m).astype(BF16)

    def weighted(i):
        hd, qb = items[i]
        slot = i % 2
        q0, kk = qb * MOBA_BLOCK, (qb + 1) * MOBA_BLOCK
        h = kk // 2
        o = (jnp.dot(vaug_scr[hd, :, 0:h], p_scr[slot, 0:h, :], preferred_element_type=F32)
             + jnp.dot(vaug_scr[hd, :, h:kk], p_scr[slot, h:kk, :], preferred_element_type=F32))
        out_scr[hd * HEAD_DIM:(hd + 1) * HEAD_DIM, q0:kk] = o[0:HEAD_DIM] / o[HEAD_DIM:HEAD_DIM + 1]

    n_items = len(items)
    col_max = {0: scores(0)}
    if n_items > 1:
        col_max[1] = scores(1)
    probs(0, col_max.pop(0))
    for i in range(n_items):
        if i + 2 < n_items:
            col_max[i + 2] = scores(i + 2)
        if i + 1 < n_items:
            probs(i + 1, col_max.pop(i + 1))
        weighted(i)
    o_ref[0] = out_scr[...].T


def _attn_call(slopes, q, k, v):
    b, s, _ = q.shape
    spec = pl.BlockSpec((1, s, LANES), lambda i, j, sl: (i, 0, j))
    return pl.pallas_call(
        _attn_kernel,
        grid_spec=pltpu.PrefetchScalarGridSpec(
            num_scalar_prefetch=1,
            grid=(b, D_ATTN // LANES),
            in_specs=[spec, spec, spec],
            out_specs=spec,
            scratch_shapes=[
                pltpu.VMEM((s, AUG), BF16),
                pltpu.VMEM((HEADS_PER_STEP, AUG, s), BF16),
                pltpu.VMEM((HEADS_PER_STEP, HEAD_DIM + 2 * SUBLANES, s), BF16),
                pltpu.VMEM((2, s, MOBA_BLOCK), F32),
                pltpu.VMEM((2, s, MOBA_BLOCK), BF16),
                pltpu.VMEM((LANES, s), F32),
            ],
        ),
        out_shape=jax.ShapeDtypeStruct((b, s, D_ATTN), F32),
        compiler_params=pltpu.CompilerParams(
            dimension_semantics=("arbitrary", "arbitrary"), vmem_limit_bytes=VMEM_LIMIT_BYTES),
        name="moba_attn",
    )(slopes, q, k, v)


def _lru_kernel(xl_ref, zl_ref, cw_ref, cb_ref, w0_ref, w1_ref, ab_ref, xb_ref, lam_ref, nw_ref,
                o_ref, xpad_scr, h_scr):
    s_len = xl_ref.shape[1]
    tc = LRU_CHUNK
    groups = tc // SUBLANES
    half = D_LRU // 2

    xpad_scr[0:SUBLANES, :] = jnp.zeros((SUBLANES, D_LRU), F32)
    xpad_scr[SUBLANES:, :] = xl_ref[0]

    neg_lam = -lam_ref[...]
    softplus = jnp.maximum(neg_lam, 0.0) + jnp.log1p(jnp.exp(-jnp.abs(neg_lam)))
    half_rate = (0.5 * RG_C) * softplus
    ab_half = 0.5 * ab_ref[...]
    xb_half = 0.5 * xb_ref[...]
    sub =lax.broadcasted_iota(jnp.int32, (SUBLANES, D_LRU), 0)
    steps = (1, 2, 4)
    in_group = [sub >= step for step in steps]

    def chunk(ci, h_prev):
        t0 = pl.multiple_of(ci * tc, tc)
        xw = xpad_scr[pl.ds(t0, tc + SUBLANES), :]
        xc = cb_ref[...]
        for j in range(CONV_WIDTH):
            lo = SUBLANES - (CONV_WIDTH - 1) + j
            xc = xc + cw_ref[j:j + 1, :] * xw[lo:lo + tc]
        xcb = xc.astype(BF16)
        g0 = jnp.dot(xcb[:, :half], w0_ref[...], preferred_element_type=F32)
        g1 = jnp.dot(xcb[:, half:], w1_ref[...], preferred_element_type=F32)
        t_r = jnp.tanh(jnp.concatenate([g0[:, :half], g1[:, :half]], axis=1) + ab_half)
        t_i = jnp.tanh(jnp.concatenate([g0[:, half:], g1[:, half:]], axis=1) + xb_half)
        neg_log_a = half_rate * t_r + half_rate
        a = jnp.exp(-neg_log_a)
        bterm = jnp.sqrt(jnp.tanh(neg_log_a) * (a * a + 1.0)) * ((t_i + 1.0) * (0.5 * xc))

        h = h_prev
        for g in range(groups):
            rows = slice(g * SUBLANES, (g + 1) * SUBLANES)
            av, bv = a[rows], bterm[rows]
            for step, valid in zip(steps, in_group):
                a_sh = jnp.where(valid, pltpu.roll(av, step, axis=0), 1.0)
                b_sh = jnp.where(valid, pltpu.roll(bv, step, axis=0), 0.0)
                bv = av * b_sh + bv
                av = av * a_sh
            h_last = jnp.broadcast_to(h[SUBLANES - 1:SUBLANES, :], (SUBLANES, D_LRU))
            h = bv + av * h_last
            h_scr[rows, :] = h
        hh = h_scr[...]
        ms = jnp.mean(hh * hh, axis=-1, keepdims=True)
        yn = hh * lax.rsqrt(ms + EPS) * nw_ref[...]
        o_ref[0, pl.ds(t0, tc), :] = (yn * _silu(zl_ref[0, pl.ds(t0, tc), :])).astype(o_ref.dtype)
        return h

    lax.fori_loop(0, s_len // tc, chunk, jnp.zeros((SUBLANES, D_LRU), F32))


def _lru_call(xl, zl, conv_w, conv_b, w0, w1, a_b, x_b, lam, nw):
    b, s, c = xl.shape
    row = lambda a: a.reshape(1, c)
    seq = pl.BlockSpec((1, s, c), lambda i: (i, 0, 0))
    vec = pl.BlockSpec((1, c), lambda i: (0, 0))
    wspec = pl.BlockSpec(w0.shape, lambda i: (0, 0))
    return pl.pallas_call(
        _lru_kernel,
        grid=(b,),
        in_specs=[seq, seq, pl.BlockSpec((CONV_WIDTH, c), lambda i: (0, 0)), vec, wspec, wspec,
                  vec, vec, vec, vec],
        out_specs=seq,
        out_shape=jax.ShapeDtypeStruct((b, s, c), BF16),
        scratch_shapes=[pltpu.VMEM((s + SUBLANES, c), F32), pltpu.VMEM((LRU_CHUNK, c), F32)],
        compiler_params=pltpu.CompilerParams(
            dimension_semantics=("arbitrary",), vmem_limit_bytes=VMEM_LIMIT_BYTES),
        name="rg_lru",
    )(xl, zl, conv_w, row(conv_b), w0, w1, row(a_b), row(x_b), row(lam), row(nw))


def _outproj_kernel(ya_ref, za_ref, yl_ref, x_ref, mod_ref, anw_ref, w_ref, fnw_ref, o_ref, *, last):
    ya = ya_ref[0]
    ms = jnp.mean(ya * ya, axis=-1, keepdims=True)
    ya = (ya * lax.rsqrt(ms + EPS) * anw_ref[...]) * _silu(za_ref[0])
    y = jnp.dot(ya.astype(BF16), w_ref[0:D_ATTN, :], preferred_element_type=F32)
    y = y + jnp.dot(yl_ref[0], w_ref[D_ATTN:, :], preferred_element_type=F32)
    out = x_ref[0] + mod_ref[0, 2:3, :] * y
    if last:
        ms2 = jnp.mean(out * out, axis=-1, keepdims=True)
        out = out * lax.rsqrt(ms2 + EPS) * fnw_ref[...]
    o_ref[0] = out


def _outproj_call(ya, za, yl, x, mod3, anw, w_out_bf, fnw, last):
    b, s, d = x.shape
    tm = ROW_TILE
    half = pl.BlockSpec((1, tm, D_ATTN), lambda i, j: (i, j, 0))
    full = pl.BlockSpec((1, tm, d), lambda i, j: (i, j, 0))
    return pl.pallas_call(
        functools.partial(_outproj_kernel, last=last),
        grid=(b, s // tm),
        in_specs=[
            half, half, half, full,
            pl.BlockSpec((1, 3, d), lambda i, j: (i, 0, 0)),
            pl.BlockSpec((1, D_ATTN), lambda i, j: (0, 0)),
            pl.BlockSpec((d, d), lambda i, j: (0, 0)),
            pl.BlockSpec((1, d), lambda i, j: (0, 0)),
        ],
        out_specs=full,
        out_shape=jax.ShapeDtypeStruct((b, s, d), F32),
        compiler_params=pltpu.CompilerParams(
            dimension_semantics=("arbitrary", "arbitrary"), vmem_limit_bytes=VMEM_LIMIT_BYTES),
        name="outproj",
    )(ya, za, yl, x, mod3, anw.reshape(1, D_ATTN), w_out_bf, fnw.reshape(1, d))


def _gate_weights(a_w, x_w):
    def half(w):
        return 0.5 * jax.scipy.linalg.block_diag(*[w[g] for g in range(w.shape[0])])
    nb = N_LRU_BLOCKS // 2
    w0 = jnp.concatenate([half(a_w[:nb]), half(x_w[:nb])], axis=1)
    w1 = jnp.concatenate([half(a_w[nb:]), half(x_w[nb:])], axis=1)
    return w0.astype(BF16), w1.astype(BF16)


def kernel(x, c, norm_w, ada_w, ada_b, w_in, conv_w, conv_b, rg_a_w, rg_a_b, rg_x_w, rg_x_b,
           rg_lambda, attn_out_norm_w, lru_out_norm_w, w_out, final_norm_w):
    n_layers = ada_w.shape[0]
    b = x.shape[0]
    mod = _mod_call(c, ada_w, ada_b)
    slopes = jnp.exp2(-8.0 * (jnp.arange(N_HEADS, dtype=F32) + 1.0) / N_HEADS)
    w_in_bf = w_in.astype(BF16)
    w_out_bf = w_out.astype(BF16)
    for l in range(n_layers):
        mod3 = mod[l].reshape(b, 3, D_MODEL)
        q, k, v, za, xl, zl = _inproj_call(x, mod3, norm_w[l], w_in_bf[l])
        ya = _attn_call(slopes, q, k, v)
        w0, w1 = _gate_weights(rg_a_w[l], rg_x_w[l])
        yl = _lru_call(xl, zl, conv_w[l], conv_b[l], w0, w1, rg_a_b[l], rg_x_b[l],
                       rg_lambda[l], lru_out_norm_w[l])
        x = _outproj_call(ya, za, yl, x, mod3, attn_out_norm_w[l], w_out_bf[l], final_norm_w,
                          last=(l == n_layers - 1))
    return x
```

```python
import functools
import math

import jax
import jax.numpy as jnp
from jax import lax
from jax.experimental import pallas as pl
from jax.experimental.pallas import tpu as pltpu

D_MODEL = 1024
D_ATTN = 512
D_LRU = 512
HEAD_DIM = 64
N_HEADS = D_ATTN // HEAD_DIM
N_LRU_BLOCKS = 8
LRU_BLOCK = D_LRU // N_LRU_BLOCKS
CONV_WIDTH = 4
RG_C = 8.0
MOBA_BLOCK = 256
MOBA_TOPK = 3
D_IN = 4 * D_ATTN + 2 * D_LRU
EPS = 1e-6

F32 = jnp.float32
BF16 = jnp.bfloat16
NEG_BIG = -1e30
LOG2E = math.log2(math.e)

LANES = 128
SUBLANES = 8
VMEM_LIMIT_BYTES = 48 * 1024 * 1024

ROW_TILE = 512
LRU_CHUNK = 256
LRU_SEGMENT = 4
HEADS_PER_STEP = LANES // HEAD_DIM


def _sigmoid(v):
    return 0.5 * jnp.tanh(0.5 * v) + 0.5


def _silu(v):
    return v * _sigmoid(v)


def _split_bf16(v):
    hi = v.astype(BF16).astype(F32)
    lo = (v - hi).astype(BF16).astype(F32)
    return hi, lo


def _mod_kernel(c_ref, w_ref, b_ref, o_ref):
    s = _silu(c_ref[...]).astype(BF16)
    w = w_ref[0].astype(BF16)
    o_ref[0] = jnp.dot(s, w, preferred_element_type=F32) + b_ref[0]


def _mod_call(c, ada_w, ada_b):
    n_layers, d, d3 = ada_w.shape
    b = c.shape[0]
    tn = 1024
    return pl.pallas_call(
        _mod_kernel,
        grid=(n_layers, d3 // tn),
        in_specs=[
            pl.BlockSpec((b, d), lambda l, j: (0, 0)),
            pl.BlockSpec((1, d, tn), lambda l, j: (l, 0, j)),
            pl.BlockSpec((1, 1, tn), lambda l, j: (l, 0, j)),
        ],
        out_specs=pl.BlockSpec((1, b, tn), lambda l, j: (l, 0, j)),
        out_shape=jax.ShapeDtypeStruct((n_layers, b, d3), F32),
        compiler_params=pltpu.CompilerParams(
            dimension_semantics=("arbitrary", "arbitrary"), vmem_limit_bytes=VMEM_LIMIT_BYTES),
        name="adaln_mod",
    )(c, ada_w, ada_b.reshape(n_layers, 1, d3))


def _inproj_kernel(x_ref, mod_ref, nw_ref, w_ref, q_ref, k_ref, v_ref, za_ref, xl_ref, zl_ref):
    x = x_ref[0]
    ms = jnp.mean(x * x, axis=-1, keepdims=True)
    y = x * lax.rsqrt(ms + EPS) * nw_ref[...]
    shift = mod_ref[0, 0:1, :]
    scale = mod_ref[0, 1:2, :]
    h = (y * (1.0 + scale) + shift).astype(BF16)
    outs = (q_ref, k_ref, v_ref, za_ref, xl_ref, zl_ref)
    for j, ref in enumerate(outs):
        r = jnp.dot(h, w_ref[:, j * D_ATTN:(j + 1) * D_ATTN], preferred_element_type=F32)
        if j == 0:
            r = r * (HEAD_DIM ** -0.5 * LOG2E)
        ref[0] = r.astype(ref.dtype)


def _inproj_call(x, mod3, norm_w, w_in_bf):
    b, s, d = x.shape
    tm = ROW_TILE
    blk = lambda dt: jax.ShapeDtypeStruct((b, s, D_ATTN), dt)
    ospec = pl.BlockSpec((1, tm, D_ATTN), lambda i, j: (i, j, 0))
    return pl.pallas_call(
        _inproj_kernel,
        grid=(b, s // tm),
        in_specs=[
            pl.BlockSpec((1, tm, d), lambda i, j: (i, j, 0)),
            pl.BlockSpec((1, 3, d), lambda i, j: (i, 0, 0)),
            pl.BlockSpec((1, d), lambda i, j: (0, 0)),
            pl.BlockSpec((d, D_IN), lambda i, j: (0, 0)),
        ],
        out_specs=[ospec] * 6,
        out_shape=[blk(BF16), blk(BF16), blk(BF16), blk(BF16), blk(F32), blk(BF16)],
        compiler_params=pltpu.CompilerParams(
            dimension_semantics=("arbitrary", "arbitrary"), vmem_limit_bytes=VMEM_LIMIT_BYTES),
        name="inproj",
    )(x, mod3, norm_w.reshape(1, d), w_in_bf)


def _attn_kernel(slopes_ref, q_ref, k_ref, v_ref, kext_ref, o_ref,
                 kaug_scr, qaug_scr, vaug_scr, s_scr, p_scr, out_scr):
    s_len = q_ref.shape[1]
    n_blk = s_len // MOBA_BLOCK
    pair = pl.program_id(0)

    kp = k_ref[0]
    q_t = q_ref[0].astype(F32).T
    v_t = v_ref[0].astype(F32).T

    prow = lax.broadcasted_iota(jnp.int32, (n_blk, s_len), 0)
    pcol = lax.broadcasted_iota(jnp.int32, (n_blk, s_len), 1)
    pm = jnp.where(pcol // MOBA_BLOCK == prow, 1.0 / MOBA_BLOCK, 0.0).astype(BF16)
    kmean = jnp.dot(pm, kp, preferred_element_type=F32)

    klane = lax.broadcasted_iota(jnp.int32, (s_len, LANES), 1)
    mlane = lax.broadcasted_iota(jnp.int32, (n_blk, LANES), 1)
    for hd in range(HEADS_PER_STEP):
        kaug_scr[hd] = jnp.where(klane // HEAD_DIM == hd, kp, kext_ref[hd])

    terms = []
    for hd in range(HEADS_PER_STEP):
        rest = jnp.where(mlane // HEAD_DIM == hd, kmean, 0.0)
        for _ in range(3):
            part = rest.astype(BF16).astype(F32)
            terms.append(part)
            rest = rest - part
    gates = jnp.dot(jnp.concatenate(terms, axis=0).astype(BF16), q_t.astype(BF16),
                    preferred_element_type=F32)

    n_iota = lax.broadcasted_iota(jnp.int32, (n_blk, s_len), 0)
    q_blk = lax.broadcasted_iota(jnp.int32, (n_blk, s_len), 1) // MOBA_BLOCK
    key_off = lax.broadcasted_iota(jnp.int32, (MOBA_BLOCK, MOBA_BLOCK), 0)
    qry_off = lax.broadcasted_iota(jnp.int32, (MOBA_BLOCK, MOBA_BLOCK), 1)
    causal = key_off <= qry_off

    for hd in range(HEADS_PER_STEP):
        slope2 = slopes_ref[pair * HEADS_PER_STEP + hd] * LOG2E
        g_hi, g_mid, g_lo = (gates[(3 * hd + t) * n_blk:(3 * hd + t + 1) * n_blk] for t in range(3))
        g_t = g_hi + g_mid + g_lo
        rank = jnp.zeros((n_blk, s_len), jnp.int32)
        for m in range(n_blk):
            gm = g_t[m:m + 1, :]
            beats = (gm > g_t) | ((gm == g_t) & (m < n_iota))
            rank = rank + jnp.where(beats & (m < q_blk), 1, 0)
        sel = ((n_iota < q_blk) & (rank < MOBA_TOPK)) | (n_iota == q_blk)
        bias_hi, bias_lo = _split_bf16(slope2 * ((n_iota - q_blk) * MOBA_BLOCK).astype(F32))
        selb_hi = jnp.where(sel, bias_hi, NEG_BIG)
        selb_lo = jnp.where(sel, bias_lo, 0.0)
        flag = jnp.where(n_iota < 2, 1.0, 0.0)
        ext = jnp.concatenate(
            [selb_hi, selb_lo, flag, jnp.zeros((HEAD_DIM - 3 * n_blk, s_len), F32)], axis=0)
        q_own = q_t[hd * HEAD_DIM:(hd + 1) * HEAD_DIM]
        halves = [q_own, ext] if hd == 0 else [ext, q_own]
        qaug_scr[hd] = jnp.concatenate(halves, axis=0).astype(BF16)
        vaug_scr[hd] = jnp.concatenate(
            [v_t[hd * HEAD_DIM:(hd + 1) * HEAD_DIM], jnp.ones((2 * SUBLANES, s_len), F32)],
            axis=0).astype(BF16)

    items = [(hd, qb) for hd in range(HEADS_PER_STEP) for qb in range(n_blk)]

    def scores(i):
        hd, qb = items[i]
        slot = i % 2
        q0, kk = qb * MOBA_BLOCK, (qb + 1) * MOBA_BLOCK
        qa = qaug_scr[hd, :, q0:kk]
        m = None
        for r0, r1 in ((0, kk // 2), (kk // 2, kk)):
            s = jnp.dot(kaug_scr[hd, r0:r1, :], qa, preferred_element_type=F32)
            if r1 > q0:
                d0 = max(r0, q0)
                masked = jnp.where(causal[d0 - q0:r1 - q0], s[d0 - r0:], NEG_BIG)
                s = masked if d0 == r0 else jnp.concatenate([s[0:d0 - r0], masked], axis=0)
            s_scr[slot, r0:r1, :] = s
            mh = jnp.max(s, axis=0, keepdims=True)
            m = mh if m is None else jnp.maximum(m, mh)
        return m

    def probs(i, m):
        hd, qb = items[i]
        slot = i % 2
        q0, kk = qb * MOBA_BLOCK, (qb + 1) * MOBA_BLOCK
        p_scr[slot, 0:kk, :] = jnp.exp2(s_scr[slot, 0:kk, :] - m).astype(BF16)

    def weighted(i):
        hd, qb = items[i]
        slot = i % 2
        q0, kk = qb * MOBA_BLOCK, (qb + 1) * MOBA_BLOCK
        h = kk // 2
        o = (jnp.dot(vaug_scr[hd, :, 0:h], p_scr[slot, 0:h, :], preferred_element_type=F32)
             + jnp.dot(vaug_scr[hd, :, h:kk], p_scr[slot, h:kk, :], preferred_element_type=F32))
        out_scr[hd * HEAD_DIM:(hd + 1) * HEAD_DIM, q0:kk] = o[0:HEAD_DIM] / o[HEAD_DIM:HEAD_DIM + 1]

    n_items = len(items)
    col_max = {0: scores(0)}
    if n_items > 1:
        col_max[1] = scores(1)
    probs(0, col_max.pop(0))
    for i in range(n_items):
        if i + 2 < n_items:
            col_max[i + 2] = scores(i + 2)
        if i + 1 < n_items:
            probs(i + 1, col_max.pop(i + 1))
        weighted(i)
    o_ref[0] = out_scr[...].T.astype(o_ref.dtype)


def _key_side_table(slopes, s_len):
    n_blk = s_len // MOBA_BLOCK
    pos = jnp.arange(s_len)
    onehot = (pos[:, None] // MOBA_BLOCK == jnp.arange(n_blk)[None, :]).astype(F32)
    off = (slopes * LOG2E)[:, None] * (pos % MOBA_BLOCK).astype(F32)[None, :]
    off_hi = lax.bitcast_convert_type(
        lax.bitcast_convert_type(off, jnp.uint32) & jnp.uint32(0xFFFF0000), F32)
    off_lo = off - off_hi
    ext = jnp.concatenate(
        [jnp.broadcast_to(onehot, (N_HEADS, s_len, n_blk))] * 2
        + [off_hi[:, :, None], off_lo[:, :, None],
           jnp.zeros((N_HEADS, s_len, HEAD_DIM - 2 * n_blk - 2), F32)], axis=2)
    zeros = jnp.zeros_like(ext)
    first = (jnp.arange(N_HEADS) % HEADS_PER_STEP == 0)[:, None, None]
    return jnp.where(first, jnp.concatenate([zeros, ext], axis=2),
                     jnp.concatenate([ext, zeros], axis=2)).astype(BF16)


def _attn_call(slopes, kext, q, k, v):
    b, s, _ = q.shape
    spec = pl.BlockSpec((1, s, LANES), lambda j, i, sl: (i, 0, j))
    return pl.pallas_call(
        _attn_kernel,
        grid_spec=pltpu.PrefetchScalarGridSpec(
            num_scalar_prefetch=1,
            grid=(D_ATTN // LANES, b),
            in_specs=[spec, spec, spec,
                      pl.BlockSpec((HEADS_PER_STEP, s, LANES), lambda j, i, sl: (j, 0, 0))],
            out_specs=spec,
            scratch_shapes=[
                pltpu.VMEM((HEADS_PER_STEP, s, LANES), BF16),
                pltpu.VMEM((HEADS_PER_STEP, LANES, s), BF16),
                pltpu.VMEM((HEADS_PER_STEP, HEAD_DIM + 2 * SUBLANES, s), BF16),
                pltpu.VMEM((2, s, MOBA_BLOCK), F32),
                pltpu.VMEM((2, s, MOBA_BLOCK), BF16),
                pltpu.VMEM((LANES, s), F32),
            ],
        ),
        out_shape=jax.ShapeDtypeStruct((b, s, D_ATTN), BF16),
        compiler_params=pltpu.CompilerParams(
            dimension_semantics=("arbitrary", "arbitrary"), vmem_limit_bytes=VMEM_LIMIT_BYTES),
        name="moba_attn",
    )(slopes, q, k, v, kext)


def _lru_kernel(xl_ref, cw_ref, cb_ref, w_ref, ab_ref, xb_ref, lam_ref, o_ref,
                xpad_scr, h_scr, xc_scr, g_scr):
    s_len = xl_ref.shape[1]
    seg = LRU_SEGMENT
    sub_len = seg * SUBLANES
    n_sub = LRU_CHUNK // sub_len

    xpad_scr[0:SUBLANES, :] = jnp.zeros((SUBLANES, LANES), F32)
    xpad_scr[SUBLANES:SUBLANES + s_len, :] = xl_ref[0]
    xpad_scr[SUBLANES + s_len:, :] = jnp.zeros((LRU_CHUNK, LANES), F32)

    neg_lam = -lam_ref[...]
    softplus = jnp.maximum(neg_lam, 0.0) + jnp.log1p(jnp.exp(-jnp.abs(neg_lam)))
    half_rate = (0.5 * RG_C) * softplus
    ab_half = 0.5 * ab_ref[...]
    xb_half = 0.5 * xb_ref[...]
    sub = lax.broadcasted_iota(jnp.int32, (SUBLANES, LANES), 0)
    steps = (1, 2, 4)
    in_vreg = [sub >= step for step in steps]

    def strided(ref, start):
        return ref[pl.ds(start, SUBLANES, stride=seg), :]

    def chunk_start(ci):
        return ci * LRU_CHUNK if isinstance(ci, int) else pl.multiple_of(ci * LRU_CHUNK, LRU_CHUNK)

    def conv_and_gates(ci, c):
        t0 = chunk_start(ci)
        rows = slice(c * sub_len, (c + 1) * sub_len)
        cache = {}
        xc_parts = []
        for g in range(seg):
            acc = cb_ref[...]
            for j in range(CONV_WIDTH):
                off = SUBLANES + c * sub_len + g - (CONV_WIDTH - 1) + j
                if off not in cache:
                    cache[off] = strided(xpad_scr, t0 + off)
                acc = acc + cw_ref[j:j + 1, :] * cache[off]
            xc_parts.append(acc)
        xc = jnp.concatenate(xc_parts, axis=0)
        xc_scr[ci % 2, rows, :] = xc
        g_scr[ci % 2, rows, :] = jnp.dot(xc.astype(BF16), w_ref[0], preferred_element_type=F32)

    def recurrence(ci, c, h_in):
        t0 = chunk_start(ci)
        rows = slice(c * sub_len, (c + 1) * sub_len)
        xc = xc_scr[ci % 2, rows, :]
        t_r = jnp.tanh(g_scr[ci % 2, rows, :LANES] + ab_half)
        t_i = jnp.tanh(g_scr[ci % 2, rows, LANES:] + xb_half)
        neg_log_a = half_rate * t_r + half_rate
        a = jnp.exp(-neg_log_a)
        y = jnp.tanh(neg_log_a) * (a * a + 1.0)
        root = jnp.where(y == 0.0, 0.0, y * lax.rsqrt(y))
        bterm = root * ((t_i + 1.0) * (0.5 * xc))

        comp = []
        for g in range(seg):
            av, bv = a[g * SUBLANES:(g + 1) * SUBLANES], bterm[g * SUBLANES:(g + 1) * SUBLANES]
            if g > 0:
                bv = av * comp[-1][1] + bv
                av = av * comp[-1][0]
            comp.append((av, bv))
        pa, pb = comp[-1]
        for step, valid in zip(steps, in_vreg):
            a_sh = jnp.where(valid, pltpu.roll(pa, step, axis=0), 1.0)
            b_sh = jnp.where(valid, pltpu.roll(pb, step, axis=0), 0.0)
            pb = pa * b_sh + pb
            pa = pa * a_sh
        ea = jnp.where(in_vreg[0], pltpu.roll(pa, 1, axis=0), 1.0)
        eb = jnp.where(in_vreg[0], pltpu.roll(pb, 1, axis=0), 0.0)
        seg_in = ea * h_in + eb
        for g in range(seg):
            h_g = comp[g][0] * seg_in + comp[g][1]
            h_scr[pl.ds(t0 + c * sub_len + g, SUBLANES, stride=seg), :] = h_g
        h_last = pa * h_in + pb
        return jnp.broadcast_to(h_last[SUBLANES - 1:SUBLANES, :], (SUBLANES, LANES))

    def body(ci, h_in):
        for c in range(n_sub):
            h_in = recurrence(ci, c, h_in)
            conv_and_gates(ci + 1, c)
        return h_in

    for c in range(n_sub):
        conv_and_gates(0, c)
    lax.fori_loop(0, s_len // LRU_CHUNK, body, jnp.zeros((SUBLANES, LANES), F32))
    o_ref[0] = h_scr[...].astype(o_ref.dtype)


def _lru_call(xl, conv_w, conv_b, w_gate, a_b, x_b, lam):
    b, s, c = xl.shape
    row = lambda a: a.reshape(1, c)
    seq = pl.BlockSpec((1, s, LANES), lambda i, j: (i, 0, j))
    vec = pl.BlockSpec((1, LANES), lambda i, j: (0, j))
    return pl.pallas_call(
        _lru_kernel,
        grid=(b, c // LANES),
        in_specs=[seq, pl.BlockSpec((CONV_WIDTH, LANES), lambda i, j: (0, j)), vec,
                  pl.BlockSpec((1, LANES, 2 * LANES), lambda i, j: (j, 0, 0)), vec, vec, vec],
        out_specs=seq,
        out_shape=jax.ShapeDtypeStruct((b, s, c), BF16),
        scratch_shapes=[
            pltpu.VMEM((SUBLANES + s + LRU_CHUNK, LANES), F32),
            pltpu.VMEM((s, LANES), F32),
            pltpu.VMEM((2, LRU_CHUNK, LANES), F32),
            pltpu.VMEM((2, LRU_CHUNK, 2 * LANES), F32),
        ],
        compiler_params=pltpu.CompilerParams(
            dimension_semantics=("arbitrary", "arbitrary"), vmem_limit_bytes=VMEM_LIMIT_BYTES),
        name="rg_lru",
    )(xl, conv_w, row(conv_b), w_gate, row(a_b), row(x_b), row(lam))


def _gated_norm(y_ref, z_ref, nw_ref):
    y = y_ref[0].astype(F32)
    ms = jnp.mean(y * y, axis=-1, keepdims=True)
    return ((y * lax.rsqrt(ms + EPS) * nw_ref[...]) * _silu(z_ref[0].astype(F32))).astype(BF16)


def _outproj_kernel(ya_ref, za_ref, yl_ref, zl_ref, x_ref, mod_ref, anw_ref, lnw_ref, w_ref, fnw_ref,
                    o_ref, *, last):
    y = jnp.dot(_gated_norm(ya_ref, za_ref, anw_ref), w_ref[0:D_ATTN, :], preferred_element_type=F32)
    y = y + jnp.dot(_gated_norm(yl_ref, zl_ref, lnw_ref), w_ref[D_ATTN:, :],
                    preferred_element_type=F32)
    out = x_ref[0] + mod_ref[0, 2:3, :] * y
    if last:
        ms2 = jnp.mean(out * out, axis=-1, keepdims=True)
        out = out * lax.rsqrt(ms2 + EPS) * fnw_ref[...]
    o_ref[0] = out


def _outproj_call(ya, za, yl, zl, x, mod3, anw, lnw, w_out_bf, fnw, last):
    b, s, d = x.shape
    tm = ROW_TILE
    half = pl.BlockSpec((1, tm, D_ATTN), lambda i, j: (i, j, 0))
    full = pl.BlockSpec((1, tm, d), lambda i, j: (i, j, 0))
    nvec = pl.BlockSpec((1, D_ATTN), lambda i, j: (0, 0))
    return pl.pallas_call(
        functools.partial(_outproj_kernel, last=last),
        grid=(b, s // tm),
        in_specs=[
            half, half, half, half, full,
            pl.BlockSpec((1, 3, d), lambda i, j: (i, 0, 0)),
            nvec, nvec,
            pl.BlockSpec((d, d), lambda i, j: (0, 0)),
            pl.BlockSpec((1, d), lambda i, j: (0, 0)),
        ],
        out_specs=full,
        out_shape=jax.ShapeDtypeStruct((b, s, d), F32),
        compiler_params=pltpu.CompilerParams(
            dimension_semantics=("arbitrary", "arbitrary"), vmem_limit_bytes=VMEM_LIMIT_BYTES),
        name="outproj",
    )(ya, za, yl, zl, x, mod3, anw.reshape(1, D_ATTN), lnw.reshape(1, D_LRU), w_out_bf,
      fnw.reshape(1, d))


def _gate_weights(a_w, x_w):
    per_slab = LANES // LRU_BLOCK
    slabs = []
    for j in range(D_LRU // LANES):
        blocks = range(j * per_slab, (j + 1) * per_slab)
        diag = lambda w: jax.scipy.linalg.block_diag(*[w[g] for g in blocks])
        slabs.append(jnp.concatenate([diag(a_w), diag(x_w)], axis=1))
    return (0.5 * jnp.stack(slabs)).astype(BF16)


def kernel(x, c, norm_w, ada_w, ada_b, w_in, conv_w, conv_b, rg_a_w, rg_a_b, rg_x_w, rg_x_b,
           rg_lambda, attn_out_norm_w, lru_out_norm_w, w_out, final_norm_w):
    n_layers = ada_w.shape[0]
    b = x.shape[0]
    mod = _mod_call(c, ada_w, ada_b)
    slopes = jnp.exp2(-8.0 * (jnp.arange(N_HEADS, dtype=F32) + 1.0) / N_HEADS)
    kext = _key_side_table(slopes, x.shape[1])
    w_in_bf = w_in.astype(BF16)
    w_out_bf = w_out.astype(BF16)
    for l in range(n_layers):
        mod3 = mod[l].reshape(b, 3, D_MODEL)
        q, k, v, za, xl, zl = _inproj_call(x, mod3, norm_w[l], w_in_bf[l])
        ya = _attn_call(slopes, kext, q, k, v)
        yl = _lru_call(xl, conv_w[l], conv_b[l], _gate_weights(rg_a_w[l], rg_x_w[l]), rg_a_b[l],
                       rg_x_b[l], rg_lambda[l])
        x = _outproj_call(ya, za, yl, zl, x, mod3, attn_out_norm_w[l], lru_out_norm_w[l],
                          w_out_bf[l], final_norm_w, last=(l == n_layers - 1))
    return x
```

```python
import functools
import math

import jax
import jax.numpy as jnp
from jax import lax
from jax.experimental import pallas as pl
from jax.experimental.pallas import tpu as pltpu

D_MODEL = 1024
D_ATTN = 512
D_LRU = 512
HEAD_DIM = 64
N_HEADS = D_ATTN // HEAD_DIM
N_LRU_BLOCKS = 8
LRU_BLOCK = D_LRU // N_LRU_BLOCKS
CONV_WIDTH = 4
RG_C = 8.0
MOBA_BLOCK = 256
MOBA_TOPK = 3
D_IN = 4 * D_ATTN + 2 * D_LRU
EPS = 1e-6

F32 = jnp.float32
BF16 = jnp.bfloat16
NEG_BIG = -1e30
LOG2E = math.log2(math.e)

LANES = 128
SUBLANES = 8
VMEM_LIMIT_BYTES = 48 * 1024 * 1024

ROW_TILE = 512
WEIGHT_CAST_ROWS = 64
LRU_CHUNK = 256
LRU_SEGMENT = 4
HEADS_PER_STEP = LANES // HEAD_DIM


def _sigmoid(v):
    return 0.5 * jnp.tanh(0.5 * v) + 0.5


def _silu(v):
    return v * _sigmoid(v)


def _split_bf16(v):
    hi = v.astype(BF16).astype(F32)
    lo = (v - hi).astype(BF16).astype(F32)
    return hi, lo


def _mod_kernel(c_ref, w_ref, b_ref, o_ref):
    s = _silu(c_ref[...]).astype(BF16)
    w = w_ref[0].astype(BF16)
    o_ref[0] = jnp.dot(s, w, preferred_element_type=F32) + b_ref[0]


def _mod_call(c, ada_w, ada_b):
    n_layers, d, d3 = ada_w.shape
    b = c.shape[0]
    tn = 1024
    return pl.pallas_call(
        _mod_kernel,
        grid=(n_layers, d3 // tn),
        in_specs=[
            pl.BlockSpec((b, d), lambda l, j: (0, 0)),
            pl.BlockSpec((1, d, tn), lambda l, j: (l, 0, j)),
            pl.BlockSpec((1, 1, tn), lambda l, j: (l, 0, j)),
        ],
        out_specs=pl.BlockSpec((1, b, tn), lambda l, j: (l, 0, j)),
        out_shape=jax.ShapeDtypeStruct((n_layers, b, d3), F32),
        compiler_params=pltpu.CompilerParams(
            dimension_semantics=("arbitrary", "arbitrary"), vmem_limit_bytes=VMEM_LIMIT_BYTES),
        name="adaln_mod",
    )(c, ada_w, ada_b.reshape(n_layers, 1, d3))


def _cast_weight_once(w_ref, wbf_scr):
    rows = w_ref.shape[1]

    @pl.when((pl.program_id(0) == 0) & (pl.program_id(1) == 0))
    def _():
        def body(i, carry):
            r0 = pl.multiple_of(i * WEIGHT_CAST_ROWS, WEIGHT_CAST_ROWS)
            wbf_scr[pl.ds(r0, WEIGHT_CAST_ROWS), :] = w_ref[0, pl.ds(r0, WEIGHT_CAST_ROWS), :].astype(BF16)
            return carry
        lax.fori_loop(0, rows // WEIGHT_CAST_ROWS, body, 0)


def _inproj_kernel(x_ref, mod_ref, nw_ref, wf_ref, q_ref, k_ref, v_ref, za_ref, xl_ref, zl_ref,
                   w_ref):
    _cast_weight_once(wf_ref, w_ref)
    x = x_ref[0]
    ms = jnp.mean(x * x, axis=-1, keepdims=True)
    y = x * lax.rsqrt(ms + EPS) * nw_ref[...]
    shift = mod_ref[0, 0:1, :]
    scale = mod_ref[0, 1:2, :]
    h = (y * (1.0 + scale) + shift).astype(BF16)
    outs = (q_ref, k_ref, v_ref, za_ref, xl_ref, zl_ref)
    for j, ref in enumerate(outs):
        r = jnp.dot(h, w_ref[:, j * D_ATTN:(j + 1) * D_ATTN], preferred_element_type=F32)
        if j == 0:
            r = r * (HEAD_DIM ** -0.5 * LOG2E)
        ref[0] = r.astype(ref.dtype)


def _inproj_call(x, mod3, norm_w, w_in, layer):
    b, s, d = x.shape
    tm = ROW_TILE
    blk = lambda dt: jax.ShapeDtypeStruct((b, s, D_ATTN), dt)
    ospec = pl.BlockSpec((1, tm, D_ATTN), lambda i, j: (i, j, 0))
    return pl.pallas_call(
        _inproj_kernel,
        grid=(b, s // tm),
        in_specs=[
            pl.BlockSpec((1, tm, d), lambda i, j: (i, j, 0)),
            pl.BlockSpec((1, 3, d), lambda i, j: (i, 0, 0)),
            pl.BlockSpec((1, d), lambda i, j: (0, 0)),
            pl.BlockSpec((1, d, D_IN), lambda i, j: (layer, 0, 0), pipeline_mode=pl.Buffered(1)),
        ],
        out_specs=[ospec] * 6,
        out_shape=[blk(BF16), blk(BF16), blk(BF16), blk(BF16), blk(F32), blk(BF16)],
        scratch_shapes=[pltpu.VMEM((d, D_IN), BF16)],
        compiler_params=pltpu.CompilerParams(
            dimension_semantics=("arbitrary", "arbitrary"), vmem_limit_bytes=VMEM_LIMIT_BYTES),
        name="inproj",
    )(x, mod3, norm_w.reshape(1, d), w_in)


def _attn_kernel(slopes_ref, q_ref, k_ref, v_ref, kext_ref, o_ref,
                 kaug_scr, qaug_scr, vaug_scr, s_scr, p_scr, out_scr):
    s_len = q_ref.shape[1]
    n_blk = s_len // MOBA_BLOCK
    pair = pl.program_id(0)

    kp = k_ref[0]
    q_t = q_ref[0].astype(F32).T
    v_t = v_ref[0].astype(F32).T

    prow = lax.broadcasted_iota(jnp.int32, (n_blk, s_len), 0)
    pcol = lax.broadcasted_iota(jnp.int32, (n_blk, s_len), 1)
    pm = jnp.where(pcol // MOBA_BLOCK == prow, 1.0 / MOBA_BLOCK, 0.0).astype(BF16)
    kmean = jnp.dot(pm, kp, preferred_element_type=F32)

    klane = lax.broadcasted_iota(jnp.int32, (s_len, LANES), 1)
    mlane = lax.broadcasted_iota(jnp.int32, (n_blk, LANES), 1)
    terms = []
    for hd in range(HEADS_PER_STEP):
        rest = jnp.where(mlane // HEAD_DIM == hd, kmean, 0.0)
        for _ in range(3):
            part = rest.astype(BF16).astype(F32)
            terms.append(part)
            rest = rest - part
    gates = jnp.dot(jnp.concatenate(terms, axis=0).astype(BF16), q_t.astype(BF16),
                    preferred_element_type=F32)

    n_iota = lax.broadcasted_iota(jnp.int32, (n_blk, s_len), 0)
    q_blk = lax.broadcasted_iota(jnp.int32, (n_blk, s_len), 1) // MOBA_BLOCK
    key_off = lax.broadcasted_iota(jnp.int32, (MOBA_BLOCK, MOBA_BLOCK), 0)
    qry_off = lax.broadcasted_iota(jnp.int32, (MOBA_BLOCK, MOBA_BLOCK), 1)
    causal = key_off <= qry_off

    def prepare(hd):
        kaug_scr[hd] = jnp.where(klane // HEAD_DIM == hd, kp, kext_ref[hd])
        slope2 = slopes_ref[pair * HEADS_PER_STEP + hd] * LOG2E
        g_hi, g_mid, g_lo = (gates[(3 * hd + t) * n_blk:(3 * hd + t + 1) * n_blk] for t in range(3))
        g_t = g_hi + g_mid + g_lo
        rank = jnp.zeros((n_blk, s_len), jnp.int32)
        for m in range(n_blk):
            gm = g_t[m:m + 1, :]
            beats = (gm > g_t) | ((gm == g_t) & (m < n_iota))
            rank = rank + jnp.where(beats & (m < q_blk), 1, 0)
        sel = ((n_iota < q_blk) & (rank < MOBA_TOPK)) | (n_iota == q_blk)
        bias_hi, bias_lo = _split_bf16(slope2 * ((n_iota - q_blk) * MOBA_BLOCK).astype(F32))
        selb_hi = jnp.where(sel, bias_hi, NEG_BIG)
        selb_lo = jnp.where(sel, bias_lo, 0.0)
        flag = jnp.where(n_iota < 2, 1.0, 0.0)
        ext = jnp.concatenate(
            [selb_hi, selb_lo, flag, jnp.zeros((HEAD_DIM - 3 * n_blk, s_len), F32)], axis=0)
        q_own = q_t[hd * HEAD_DIM:(hd + 1) * HEAD_DIM]
        halves = [q_own, ext] if hd == 0 else [ext, q_own]
        qaug_scr[hd] = jnp.concatenate(halves, axis=0).astype(BF16)
        vaug_scr[hd] = jnp.concatenate(
            [v_t[hd * HEAD_DIM:(hd + 1) * HEAD_DIM], jnp.ones((2 * SUBLANES, s_len), F32)],
            axis=0).astype(BF16)

    items = [(hd, qb) for hd in range(HEADS_PER_STEP) for qb in range(n_blk)]

    def scores(i):
        hd, qb = items[i]
        slot = i % 2
        q0, kk = qb * MOBA_BLOCK, (qb + 1) * MOBA_BLOCK
        qa = qaug_scr[hd, :, q0:kk]
        m = None
        for r0, r1 in [(r, r + MOBA_BLOCK) for r in range(0, kk, MOBA_BLOCK)]:
            s = jnp.dot(kaug_scr[hd, r0:r1, :], qa, preferred_element_type=F32)
            if r1 > q0:
                d0 = max(r0, q0)
                masked = jnp.where(causal[d0 - q0:r1 - q0], s[d0 - r0:], NEG_BIG)
                s = masked if d0 == r0 else jnp.concatenate([s[0:d0 - r0], masked], axis=0)
            s_scr[slot, r0:r1, :] = s
            mh = jnp.max(s, axis=0, keepdims=True)
            m = mh if m is None else jnp.maximum(m, mh)
        return m

    def probs(i, m):
        hd, qb = items[i]
        slot = i % 2
        q0, kk = qb * MOBA_BLOCK, (qb + 1) * MOBA_BLOCK
        p_scr[slot, 0:kk, :] = jnp.exp2(s_scr[slot, 0:kk, :] - m).astype(BF16)

    def weighted(i):
        hd, qb = items[i]
        slot = i % 2
        q0, kk = qb * MOBA_BLOCK, (qb + 1) * MOBA_BLOCK
        o = None
        for r0, r1 in ((0, kk // 2), (kk // 2, kk)):
            part = jnp.dot(vaug_scr[hd, :, r0:r1], p_scr[slot, r0:r1, :], preferred_element_type=F32)
            o = part if o is None else o + part
        out_scr[hd * HEAD_DIM:(hd + 1) * HEAD_DIM, q0:kk] = o[0:HEAD_DIM] / o[HEAD_DIM:HEAD_DIM + 1]

    n_items = len(items)
    prepare(0)
    col_max = {0: scores(0), 1: scores(1)}
    probs(0, col_max.pop(0))
    for i in range(n_items):
        if i % n_blk == 0 and i // n_blk + 1 < HEADS_PER_STEP:
            prepare(i // n_blk + 1)
        if i + 2 < n_items:
            col_max[i + 2] = scores(i + 2)
        if i + 1 < n_items:
            probs(i + 1, col_max.pop(i + 1))
        weighted(i)
    o_ref[0] = out_scr[...].T.astype(o_ref.dtype)


def _key_side_table(slopes, s_len):
    n_blk = s_len // MOBA_BLOCK
    shape = (N_HEADS, s_len, LANES)
    head = lax.broadcasted_iota(jnp.int32, shape, 0)
    pos = lax.broadcasted_iota(jnp.int32, shape, 1)
    lane = lax.broadcasted_iota(jnp.int32, shape, 2)
    sub = lane % HEAD_DIM
    foreign = lane // HEAD_DIM != head % HEADS_PER_STEP
    onehot = jnp.where(sub % n_blk == pos // MOBA_BLOCK, 1.0, 0.0)
    off = (slopes * LOG2E)[:, None, None] * (pos % MOBA_BLOCK).astype(F32)
    off_hi = lax.bitcast_convert_type(
        lax.bitcast_convert_type(off, jnp.uint32) & jnp.uint32(0xFFFF0000), F32)
    ext = jnp.where(sub < 2 * n_blk, onehot,
                    jnp.where(sub == 2 * n_blk, off_hi,
                              jnp.where(sub == 2 * n_blk + 1, off - off_hi, 0.0)))
    return jnp.where(foreign, ext, 0.0).astype(BF16)


def _attn_call(slopes, kext, q, k, v):
    b, s, _ = q.shape
    spec = pl.BlockSpec((1, s, LANES), lambda j, i, sl: (i, 0, j))
    return pl.pallas_call(
        _attn_kernel,
        grid_spec=pltpu.PrefetchScalarGridSpec(
            num_scalar_prefetch=1,
            grid=(D_ATTN // LANES, b),
            in_specs=[spec, spec, spec,
                      pl.BlockSpec((HEADS_PER_STEP, s, LANES), lambda j, i, sl: (j, 0, 0))],
            out_specs=spec,
            scratch_shapes=[
                pltpu.VMEM((HEADS_PER_STEP, s, LANES), BF16),
                pltpu.VMEM((HEADS_PER_STEP, LANES, s), BF16),
                pltpu.VMEM((HEADS_PER_STEP, HEAD_DIM + 2 * SUBLANES, s), BF16),
                pltpu.VMEM((2, s, MOBA_BLOCK), F32),
                pltpu.VMEM((2, s, MOBA_BLOCK), BF16),
                pltpu.VMEM((LANES, s), F32),
            ],
        ),
        out_shape=jax.ShapeDtypeStruct((b, s, D_ATTN), BF16),
        compiler_params=pltpu.CompilerParams(
            dimension_semantics=("arbitrary", "arbitrary"), vmem_limit_bytes=VMEM_LIMIT_BYTES),
        name="moba_attn",
    )(slopes, q, k, v, kext)


def _lru_kernel(xl_ref, cw_ref, cb_ref, w_ref, ab_ref, xb_ref, lam_ref, o_ref,
                xpad_scr, h_scr, xc_scr, g_scr):
    s_len = xl_ref.shape[1]
    seg = LRU_SEGMENT
    sub_len = seg * SUBLANES
    n_sub = LRU_CHUNK // sub_len

    xpad_scr[0:SUBLANES, :] = jnp.zeros((SUBLANES, LANES), F32)
    xpad_scr[SUBLANES:SUBLANES + s_len, :] = xl_ref[0]
    xpad_scr[SUBLANES + s_len:, :] = jnp.zeros((LRU_CHUNK, LANES), F32)

    neg_lam = -lam_ref[...]
    softplus = jnp.maximum(neg_lam, 0.0) + jnp.log1p(jnp.exp(-jnp.abs(neg_lam)))
    half_rate = (0.5 * RG_C) * softplus
    ab_half = 0.5 * ab_ref[...]
    xb_half = 0.5 * xb_ref[...]
    sub = lax.broadcasted_iota(jnp.int32, (SUBLANES, LANES), 0)
    steps = (1, 2, 4)
    in_vreg = [sub >= step for step in steps]

    def strided(ref, start):
        return ref[pl.ds(start, SUBLANES, stride=seg), :]

    def chunk_start(ci):
        return ci * LRU_CHUNK if isinstance(ci, int) else pl.multiple_of(ci * LRU_CHUNK, LRU_CHUNK)

    def conv_and_gates(ci, c):
        t0 = chunk_start(ci)
        rows = slice(c * sub_len, (c + 1) * sub_len)
        cache = {}
        xc_parts = []
        for g in range(seg):
            acc = cb_ref[...]
            for j in range(CONV_WIDTH):
                off = SUBLANES + c * sub_len + g - (CONV_WIDTH - 1) + j
                if off not in cache:
                    cache[off] = strided(xpad_scr, t0 + off)
                acc = acc + cw_ref[j:j + 1, :] * cache[off]
            xc_parts.append(acc)
        xc = jnp.concatenate(xc_parts, axis=0)
        xc_scr[ci % 2, rows, :] = xc
        g_scr[ci % 2, rows, :] = jnp.dot(xc.astype(BF16), w_ref[0], preferred_element_type=F32)

    def recurrence(ci, c, h_in):
        t0 = chunk_start(ci)
        rows = slice(c * sub_len, (c + 1) * sub_len)
        xc = xc_scr[ci % 2, rows, :]
        t_r = jnp.tanh(g_scr[ci % 2, rows, :LANES] + ab_half)
        t_i = jnp.tanh(g_scr[ci % 2, rows, LANES:] + xb_half)
        neg_log_a = half_rate * t_r + half_rate
        a = jnp.exp(-neg_log_a)
        y = jnp.tanh(neg_log_a) * (a * a + 1.0)
        root = jnp.where(y == 0.0, 0.0, y * lax.rsqrt(y))
        bterm = root * ((t_i + 1.0) * (0.5 * xc))

        comp = []
        for g in range(seg):
            av, bv = a[g * SUBLANES:(g + 1) * SUBLANES], bterm[g * SUBLANES:(g + 1) * SUBLANES]
            if g > 0:
                bv = av * comp[-1][1] + bv
                av = av * comp[-1][0]
            comp.append((av, bv))
        pa, pb = comp[-1]
        for step, valid in zip(steps, in_vreg):
            a_sh = jnp.where(valid, pltpu.roll(pa, step, axis=0), 1.0)
            b_sh = jnp.where(valid, pltpu.roll(pb, step, axis=0), 0.0)
            pb = pa * b_sh + pb
            pa = pa * a_sh
        ea = jnp.where(in_vreg[0], pltpu.roll(pa, 1, axis=0), 1.0)
        eb = jnp.where(in_vreg[0], pltpu.roll(pb, 1, axis=0), 0.0)
        seg_in = ea * h_in + eb
        for g in range(seg):
            h_g = comp[g][0] * seg_in + comp[g][1]
            h_scr[pl.ds(t0 + c * sub_len + g, SUBLANES, stride=seg), :] = h_g
        h_last = pa * h_in + pb
        return jnp.broadcast_to(h_last[SUBLANES - 1:SUBLANES, :], (SUBLANES, LANES))

    def body(ci, h_in):
        for c in range(n_sub):
            h_in = recurrence(ci, c, h_in)
            conv_and_gates(ci + 1, c)
        return h_in

    for c in range(n_sub):
        conv_and_gates(0, c)
    lax.fori_loop(0, s_len // LRU_CHUNK, body, jnp.zeros((SUBLANES, LANES), F32))
    o_ref[0] = h_scr[...].astype(o_ref.dtype)


def _lru_call(xl, conv_w, conv_b, w_gate, a_b, x_b, lam):
    b, s, c = xl.shape
    row = lambda a: a.reshape(1, c)
    seq = pl.BlockSpec((1, s, LANES), lambda i, j: (i, 0, j))
    vec = pl.BlockSpec((1, LANES), lambda i, j: (0, j))
    return pl.pallas_call(
        _lru_kernel,
        grid=(b, c // LANES),
        in_specs=[seq, pl.BlockSpec((CONV_WIDTH, LANES), lambda i, j: (0, j)), vec,
                  pl.BlockSpec((1, LANES, 2 * LANES), lambda i, j: (j, 0, 0)), vec, vec, vec],
        out_specs=seq,
        out_shape=jax.ShapeDtypeStruct((b, s, c), BF16),
        scratch_shapes=[
            pltpu.VMEM((SUBLANES + s + LRU_CHUNK, LANES), F32),
            pltpu.VMEM((s, LANES), F32),
            pltpu.VMEM((2, LRU_CHUNK, LANES), F32),
            pltpu.VMEM((2, LRU_CHUNK, 2 * LANES), F32),
        ],
        compiler_params=pltpu.CompilerParams(
            dimension_semantics=("arbitrary", "arbitrary"), vmem_limit_bytes=VMEM_LIMIT_BYTES),
        name="rg_lru",
    )(xl, conv_w, row(conv_b), w_gate, row(a_b), row(x_b), row(lam))


def _gated_norm(y_ref, z_ref, nw_ref):
    y = y_ref[0].astype(F32)
    ms = jnp.mean(y * y, axis=-1, keepdims=True)
    return ((y * lax.rsqrt(ms + EPS) * nw_ref[...]) * _silu(z_ref[0].astype(F32))).astype(BF16)


def _outproj_kernel(ya_ref, za_ref, yl_ref, zl_ref, x_ref, mod_ref, anw_ref, lnw_ref, wf_ref, fnw_ref,
                    o_ref, w_ref, *, last):
    _cast_weight_once(wf_ref, w_ref)
    y = jnp.dot(_gated_norm(ya_ref, za_ref, anw_ref), w_ref[0:D_ATTN, :], preferred_element_type=F32)
    y = y + jnp.dot(_gated_norm(yl_ref, zl_ref, lnw_ref), w_ref[D_ATTN:, :],
                    preferred_element_type=F32)
    out = x_ref[0] + mod_ref[0, 2:3, :] * y
    if last:
        ms2 = jnp.mean(out * out, axis=-1, keepdims=True)
        out = out * lax.rsqrt(ms2 + EPS) * fnw_ref[...]
    o_ref[0] = out


def _outproj_call(ya, za, yl, zl, x, mod3, anw, lnw, w_out, layer, fnw, last):
    b, s, d = x.shape
    tm = ROW_TILE
    half = pl.BlockSpec((1, tm, D_ATTN), lambda i, j: (i, j, 0))
    full = pl.BlockSpec((1, tm, d), lambda i, j: (i, j, 0))
    nvec = pl.BlockSpec((1, D_ATTN), lambda i, j: (0, 0))
    return pl.pallas_call(
        functools.partial(_outproj_kernel, last=last),
        grid=(b, s // tm),
        in_specs=[
            half, half, half, half, full,
            pl.BlockSpec((1, 3, d), lambda i, j: (i, 0, 0)),
            nvec, nvec,
            pl.BlockSpec((1, d, d), lambda i, j: (layer, 0, 0), pipeline_mode=pl.Buffered(1)),
            pl.BlockSpec((1, d), lambda i, j: (0, 0)),
        ],
        out_specs=full,
        out_shape=jax.ShapeDtypeStruct((b, s, d), F32),
        scratch_shapes=[pltpu.VMEM((d, d), BF16)],
        compiler_params=pltpu.CompilerParams(
            dimension_semantics=("arbitrary", "arbitrary"), vmem_limit_bytes=VMEM_LIMIT_BYTES),
        name="outproj",
    )(ya, za, yl, zl, x, mod3, anw.reshape(1, D_ATTN), lnw.reshape(1, D_LRU), w_out,
      fnw.reshape(1, d))


def _gate_weights(a_w, x_w):
    per_slab = LANES // LRU_BLOCK
    slabs = []
    for j in range(D_LRU // LANES):
        blocks = range(j * per_slab, (j + 1) * per_slab)
        diag = lambda w: jax.scipy.linalg.block_diag(*[w[g] for g in blocks])
        slabs.append(jnp.concatenate([diag(a_w), diag(x_w)], axis=1))
    return (0.5 * jnp.stack(slabs)).astype(BF16)


def kernel(x, c, norm_w, ada_w, ada_b, w_in, conv_w, conv_b, rg_a_w, rg_a_b, rg_x_w, rg_x_b,
           rg_lambda, attn_out_norm_w, lru_out_norm_w, w_out, final_norm_w):
    n_layers = ada_w.shape[0]
    b = x.shape[0]
    mod = _mod_call(c, ada_w, ada_b)
    slopes = jnp.exp2(-8.0 * (jnp.arange(N_HEADS, dtype=F32) + 1.0) / N_HEADS)
    kext = _key_side_table(slopes, x.shape[1])
    for l in range(n_layers):
        mod3 = mod[l].reshape(b, 3, D_MODEL)
        q, k, v, za, xl, zl = _inproj_call(x, mod3, norm_w[l], w_in, l)
        ya = _attn_call(slopes, kext, q, k, v)
        yl = _lru_call(xl, conv_w[l], conv_b[l], _gate_weights(rg_a_w[l], rg_x_w[l]), rg_a_b[l],
                       rg_x_b[l], rg_lambda[l])
        x = _outproj_call(ya, za, yl, zl, x, mod3, attn_out_norm_w[l], lru_out_norm_w[l],
                          w_out, l, final_norm_w, last=(l == n_layers - 1))
    return x
```

```python
import math

import jax
import jax.numpy as jnp
from jax import lax
from jax.experimental import pallas as pl
from jax.experimental.pallas import tpu as pltpu

D_MODEL = 1024
D_ATTN = 512
D_LRU = 512
HEAD_DIM = 64
N_HEADS = D_ATTN // HEAD_DIM
N_LRU_BLOCKS = 8
LRU_BLOCK = D_LRU // N_LRU_BLOCKS
CONV_WIDTH = 4
RG_C = 8.0
MOBA_BLOCK = 256
MOBA_TOPK = 3
D_IN = 4 * D_ATTN + 2 * D_LRU
EPS = 1e-6

F32 = jnp.float32
BF16 = jnp.bfloat16
NEG_BIG = -1e30
LOG2E = math.log2(math.e)

LANES = 128
SUBLANES = 8
VMEM_LIMIT_BYTES = 48 * 1024 * 1024
PROJ_VMEM_LIMIT_BYTES = 56 * 1024 * 1024

ROW_TILE = 512
WEIGHT_CAST_ROWS = 64
LRU_CHUNK = 256
LRU_SEGMENT = 4
HEADS_PER_PAIR = LANES // HEAD_DIM
ATTN_PAIRS_PER_STEP = 1


def _sigmoid(v):
    return 0.5 * jnp.tanh(0.5 * v) + 0.5


def _silu(v):
    return v * _sigmoid(v)


def _split_bf16(v):
    hi = v.astype(BF16).astype(F32)
    lo = (v - hi).astype(BF16).astype(F32)
    return hi, lo


def _mod_kernel(c_ref, w_ref, b_ref, o_ref):
    s = _silu(c_ref[...]).astype(BF16)
    w = w_ref[0].astype(BF16)
    o_ref[0] = jnp.dot(s, w, preferred_element_type=F32) + b_ref[0]


def _mod_call(c, ada_w, ada_b):
    n_layers, d, d3 = ada_w.shape
    b = c.shape[0]
    tn = 1024
    return pl.pallas_call(
        _mod_kernel,
        grid=(n_layers, d3 // tn),
        in_specs=[
            pl.BlockSpec((b, d), lambda l, j: (0, 0)),
            pl.BlockSpec((1, d, tn), lambda l, j: (l, 0, j)),
            pl.BlockSpec((1, 1, tn), lambda l, j: (l, 0, j)),
        ],
        out_specs=pl.BlockSpec((1, b, tn), lambda l, j: (l, 0, j)),
        out_shape=jax.ShapeDtypeStruct((n_layers, b, d3), F32),
        compiler_params=pltpu.CompilerParams(
            dimension_semantics=("arbitrary", "arbitrary"), vmem_limit_bytes=VMEM_LIMIT_BYTES),
        name="adaln_mod",
    )(c, ada_w, ada_b.reshape(n_layers, 1, d3))


def _cast_weight_once(w_ref, wbf_scr):
    rows = w_ref.shape[1]

    @pl.when((pl.program_id(0) == 0) & (pl.program_id(1) == 0))
    def _():
        def body(i, carry):
            r0 = pl.multiple_of(i * WEIGHT_CAST_ROWS, WEIGHT_CAST_ROWS)
            wbf_scr[pl.ds(r0, WEIGHT_CAST_ROWS), :] = w_ref[0, pl.ds(r0, WEIGHT_CAST_ROWS), :].astype(BF16)
            return carry
        lax.fori_loop(0, rows // WEIGHT_CAST_ROWS, body, 0)


def _inproj_math(x, mod_ref, nw_ref, w_ref, out_refs):
    ms = jnp.mean(x * x, axis=-1, keepdims=True)
    y = x * lax.rsqrt(ms + EPS) * nw_ref[...]
    shift = mod_ref[0, 0:1, :]
    scale = mod_ref[0, 1:2, :]
    h = (y * (1.0 + scale) + shift).astype(BF16)
    for j, ref in enumerate(out_refs):
        r = jnp.dot(h, w_ref[:, j * D_ATTN:(j + 1) * D_ATTN], preferred_element_type=F32)
        if j == 0:
            r = r * (HEAD_DIM ** -0.5 * LOG2E)
        ref[0] = r.astype(ref.dtype)


def _inproj_kernel(x_ref, mod_ref, nw_ref, wf_ref, q_ref, k_ref, v_ref, za_ref, xl_ref, zl_ref,
                   w_ref):
    _cast_weight_once(wf_ref, w_ref)
    _inproj_math(x_ref[0], mod_ref, nw_ref, w_ref, (q_ref, k_ref, v_ref, za_ref, xl_ref, zl_ref))


def _inproj_call(x, mod3, norm_w, w_in, layer):
    b, s, d = x.shape
    tm = ROW_TILE
    blk = lambda dt: jax.ShapeDtypeStruct((b, s, D_ATTN), dt)
    ospec = pl.BlockSpec((1, tm, D_ATTN), lambda i, j: (i, j, 0))
    return pl.pallas_call(
        _inproj_kernel,
        grid=(b, s // tm),
        in_specs=[
            pl.BlockSpec((1, tm, d), lambda i, j: (i, j, 0)),
            pl.BlockSpec((1, 3, d), lambda i, j: (i, 0, 0)),
            pl.BlockSpec((1, d), lambda i, j: (0, 0)),
            pl.BlockSpec((1, d, D_IN), lambda i, j: (layer, 0, 0), pipeline_mode=pl.Buffered(1)),
        ],
        out_specs=[ospec] * 6,
        out_shape=[blk(BF16), blk(BF16), blk(BF16), blk(BF16), blk(F32), blk(BF16)],
        scratch_shapes=[pltpu.VMEM((d, D_IN), BF16)],
        compiler_params=pltpu.CompilerParams(
            dimension_semantics=("arbitrary", "arbitrary"), vmem_limit_bytes=VMEM_LIMIT_BYTES),
        name="inproj",
    )(x, mod3, norm_w.reshape(1, d), w_in)


def _attn_kernel(slopes_ref, q_ref, k_ref, v_ref, kext_ref, o_ref,
                 kaug_scr, qaug_scr, vaug_scr, s_scr, p_scr, out_scr):
    s_len = q_ref.shape[1]
    n_blk = s_len // MOBA_BLOCK
    n_pairs = q_ref.shape[2] // LANES
    n_heads = n_pairs * HEADS_PER_PAIR
    first_head = pl.program_id(0) * n_heads

    prow = lax.broadcasted_iota(jnp.int32, (n_blk, s_len), 0)
    pcol = lax.broadcasted_iota(jnp.int32, (n_blk, s_len), 1)
    block_mean = jnp.where(pcol // MOBA_BLOCK == prow, 1.0 / MOBA_BLOCK, 0.0).astype(BF16)
    klane = lax.broadcasted_iota(jnp.int32, (s_len, LANES), 1)
    mlane = lax.broadcasted_iota(jnp.int32, (n_blk, LANES), 1)
    n_iota = prow
    q_blk = pcol // MOBA_BLOCK
    key_off = lax.broadcasted_iota(jnp.int32, (MOBA_BLOCK, MOBA_BLOCK), 0)
    qry_off = lax.broadcasted_iota(jnp.int32, (MOBA_BLOCK, MOBA_BLOCK), 1)
    causal = key_off <= qry_off

    def prepare(pp):
        lanes = slice(pp * LANES, (pp + 1) * LANES)
        kp = k_ref[0, :, lanes]
        q_t = q_ref[0, :, lanes].astype(F32).T
        v_t = v_ref[0, :, lanes].astype(F32).T
        kmean = jnp.dot(block_mean, kp, preferred_element_type=F32)
        terms = []
        for hd in range(HEADS_PER_PAIR):
            rest = jnp.where(mlane // HEAD_DIM == hd, kmean, 0.0)
            for _ in range(3):
                part = rest.astype(BF16).astype(F32)
                terms.append(part)
                rest = rest - part
        gates = jnp.dot(jnp.concatenate(terms, axis=0).astype(BF16), q_t.astype(BF16),
                        preferred_element_type=F32)
        for hd in range(HEADS_PER_PAIR):
            h = pp * HEADS_PER_PAIR + hd
            kaug_scr[h] = jnp.where(klane // HEAD_DIM == hd, kp, kext_ref[h])
            slope2 = slopes_ref[first_head + h] * LOG2E
            g_hi, g_mid, g_lo = (gates[(3 * hd + t) * n_blk:(3 * hd + t + 1) * n_blk]
                                 for t in range(3))
            g_t = g_hi + g_mid + g_lo
            rank = jnp.zeros((n_blk, s_len), jnp.int32)
            for m in range(n_blk):
                gm = g_t[m:m + 1, :]
                beats = (gm > g_t) | ((gm == g_t) & (m < n_iota))
                rank = rank + jnp.where(beats & (m < q_blk), 1, 0)
            sel = ((n_iota < q_blk) & (rank < MOBA_TOPK)) | (n_iota == q_blk)
            bias_hi, bias_lo = _split_bf16(slope2 * ((n_iota - q_blk) * MOBA_BLOCK).astype(F32))
            selb_hi = jnp.where(sel, bias_hi, NEG_BIG)
            selb_lo = jnp.where(sel, bias_lo, 0.0)
            flag = jnp.where(n_iota < 2, 1.0, 0.0)
            ext = jnp.concatenate(
                [selb_hi, selb_lo, flag, jnp.zeros((HEAD_DIM - 3 * n_blk, s_len), F32)], axis=0)
            q_own = q_t[hd * HEAD_DIM:(hd + 1) * HEAD_DIM]
            halves = [q_own, ext] if hd == 0 else [ext, q_own]
            qaug_scr[h] = jnp.concatenate(halves, axis=0).astype(BF16)
            vaug_scr[h] = jnp.concatenate(
                [v_t[hd * HEAD_DIM:(hd + 1) * HEAD_DIM], jnp.ones((2 * SUBLANES, s_len), F32)],
                axis=0).astype(BF16)

    items = [(h, qb) for h in range(n_heads) for qb in range(n_blk)]
    per_pair = HEADS_PER_PAIR * n_blk

    def scores(i):
        h, qb = items[i]
        slot = i % 2
        q0, kk = qb * MOBA_BLOCK, (qb + 1) * MOBA_BLOCK
        qa = qaug_scr[h, :, q0:kk]
        m = None
        for r0 in range(0, kk, MOBA_BLOCK):
            s = jnp.dot(kaug_scr[h, r0:r0 + MOBA_BLOCK, :], qa, preferred_element_type=F32)
            if r0 == q0:
                s = jnp.where(causal, s, NEG_BIG)
            s_scr[slot, r0:r0 + MOBA_BLOCK, :] = s
            mh = jnp.max(s, axis=0, keepdims=True)
            m = mh if m is None else jnp.maximum(m, mh)
        return m

    def probs(i, m):
        _, qb = items[i]
        slot = i % 2
        kk = (qb + 1) * MOBA_BLOCK
        p_scr[slot, 0:kk, :] = jnp.exp2(s_scr[slot, 0:kk, :] - m).astype(BF16)

    def weighted(i):
        h, qb = items[i]
        slot = i % 2
        q0, kk = qb * MOBA_BLOCK, (qb + 1) * MOBA_BLOCK
        o = None
        for r0, r1 in ((0, kk // 2), (kk // 2, kk)):
            part = jnp.dot(vaug_scr[h, :, r0:r1], p_scr[slot, r0:r1, :], preferred_element_type=F32)
            o = part if o is None else o + part
        out_scr[h * HEAD_DIM:(h + 1) * HEAD_DIM, q0:kk] = o[0:HEAD_DIM] / o[HEAD_DIM:HEAD_DIM + 1]

    n_items = len(items)
    prepare(0)
    col_max = {0: scores(0), 1: scores(1)}
    probs(0, col_max.pop(0))
    for i in range(n_items):
        pp, within = divmod(i, per_pair)
        if within == 0 and pp + 1 < n_pairs:
            prepare(pp + 1)
        if i + 2 < n_items:
            col_max[i + 2] = scores(i + 2)
        if i + 1 < n_items:
            probs(i + 1, col_max.pop(i + 1))
        weighted(i)
        if within == per_pair - 1:
            lanes = slice(pp * LANES, (pp + 1) * LANES)
            o_ref[0, :, lanes] = out_scr[lanes, :].T.astype(o_ref.dtype)


def _key_side_table(slopes, s_len):
    n_blk = s_len // MOBA_BLOCK
    shape = (N_HEADS, s_len, LANES)
    head = lax.broadcasted_iota(jnp.int32, shape, 0)
    pos = lax.broadcasted_iota(jnp.int32, shape, 1)
    lane = lax.broadcasted_iota(jnp.int32, shape, 2)
    sub = lane % HEAD_DIM
    foreign = lane // HEAD_DIM != head % HEADS_PER_PAIR
    onehot = jnp.where(sub % n_blk == pos // MOBA_BLOCK, 1.0, 0.0)
    off = (slopes * LOG2E)[:, None, None] * (pos % MOBA_BLOCK).astype(F32)
    off_hi = lax.bitcast_convert_type(
        lax.bitcast_convert_type(off, jnp.uint32) & jnp.uint32(0xFFFF0000), F32)
    ext = jnp.where(sub < 2 * n_blk, onehot,
                    jnp.where(sub == 2 * n_blk, off_hi,
                              jnp.where(sub == 2 * n_blk + 1, off - off_hi, 0.0)))
    return jnp.where(foreign, ext, 0.0).astype(BF16)


def _attn_call(slopes, kext, q, k, v):
    b, s, _ = q.shape
    width = ATTN_PAIRS_PER_STEP * LANES
    heads = ATTN_PAIRS_PER_STEP * HEADS_PER_PAIR
    spec = pl.BlockSpec((1, s, width), lambda j, i, sl: (i, 0, j))
    return pl.pallas_call(
        _attn_kernel,
        grid_spec=pltpu.PrefetchScalarGridSpec(
            num_scalar_prefetch=1,
            grid=(D_ATTN // width, b),
            in_specs=[spec, spec, spec,
                      pl.BlockSpec((heads, s, LANES), lambda j, i, sl: (j, 0, 0),
                                   pipeline_mode=pl.Buffered(1))],
            out_specs=spec,
            scratch_shapes=[
                pltpu.VMEM((heads, s, LANES), BF16),
                pltpu.VMEM((heads, LANES, s), BF16),
                pltpu.VMEM((heads, HEAD_DIM + 2 * SUBLANES, s), BF16),
                pltpu.VMEM((2, s, MOBA_BLOCK), F32),
                pltpu.VMEM((2, s, MOBA_BLOCK), BF16),
                pltpu.VMEM((width, s), F32),
            ],
        ),
        out_shape=jax.ShapeDtypeStruct((b, s, D_ATTN), BF16),
        compiler_params=pltpu.CompilerParams(
            dimension_semantics=("arbitrary", "arbitrary"), vmem_limit_bytes=VMEM_LIMIT_BYTES),
        name="moba_attn",
    )(slopes, q, k, v, kext)


def _lru_kernel(xl_ref, cw_ref, cb_ref, w_ref, ab_ref, xb_ref, lam_ref, o_ref,
                xpad_scr, h_scr, xc_scr, g_scr):
    s_len = xl_ref.shape[1]
    seg = LRU_SEGMENT
    sub_len = seg * SUBLANES
    n_sub = LRU_CHUNK // sub_len

    xpad_scr[0:SUBLANES, :] = jnp.zeros((SUBLANES, LANES), F32)
    xpad_scr[SUBLANES:SUBLANES + s_len, :] = xl_ref[0]
    xpad_scr[SUBLANES + s_len:, :] = jnp.zeros((LRU_CHUNK, LANES), F32)

    neg_lam = -lam_ref[...]
    softplus = jnp.maximum(neg_lam, 0.0) + jnp.log1p(jnp.exp(-jnp.abs(neg_lam)))
    half_rate = (0.5 * RG_C) * softplus
    ab_half = 0.5 * ab_ref[...]
    xb_half = 0.5 * xb_ref[...]
    sub = lax.broadcasted_iota(jnp.int32, (SUBLANES, LANES), 0)
    steps = (1, 2, 4)
    in_vreg = [sub >= step for step in steps]

    def strided(ref, start):
        return ref[pl.ds(start, SUBLANES, stride=seg), :]

    def chunk_start(ci):
        return ci * LRU_CHUNK if isinstance(ci, int) else pl.multiple_of(ci * LRU_CHUNK, LRU_CHUNK)

    def conv_and_gates(ci, c):
        t0 = chunk_start(ci)
        rows = slice(c * sub_len, (c + 1) * sub_len)
        cache = {}
        xc_parts = []
        for g in range(seg):
            acc = cb_ref[...]
            for j in range(CONV_WIDTH):
                off = SUBLANES + c * sub_len + g - (CONV_WIDTH - 1) + j
                if off not in cache:
                    cache[off] = strided(xpad_scr, t0 + off)
                acc = acc + cw_ref[j:j + 1, :] * cache[off]
            xc_parts.append(acc)
        xc = jnp.concatenate(xc_parts, axis=0)
        xc_scr[ci % 2, rows, :] = xc
        g_scr[ci % 2, rows, :] = jnp.dot(xc.astype(BF16), w_ref[0], preferred_element_type=F32)

    def recurrence(ci, c, h_in):
        t0 = chunk_start(ci)
        rows = slice(c * sub_len, (c + 1) * sub_len)
        xc = xc_scr[ci % 2, rows, :]
        t_r = jnp.tanh(g_scr[ci % 2, rows, :LANES] + ab_half)
        t_i = jnp.tanh(g_scr[ci % 2, rows, LANES:] + xb_half)
        neg_log_a = half_rate * t_r + half_rate
        a = jnp.exp(-neg_log_a)
        y = jnp.tanh(neg_log_a) * (a * a + 1.0)
        root = jnp.where(y == 0.0, 0.0, y * lax.rsqrt(y))
        bterm = root * ((t_i + 1.0) * (0.5 * xc))

        comp = []
        for g in range(seg):
            av, bv = a[g * SUBLANES:(g + 1) * SUBLANES], bterm[g * SUBLANES:(g + 1) * SUBLANES]
            if g > 0:
                bv = av * comp[-1][1] + bv
                av = av * comp[-1][0]
            comp.append((av, bv))
        pa, pb = comp[-1]
        for step, valid in zip(steps, in_vreg):
            a_sh = jnp.where(valid, pltpu.roll(pa, step, axis=0), 1.0)
            b_sh = jnp.where(valid, pltpu.roll(pb, step, axis=0), 0.0)
            pb = pa * b_sh + pb
            pa = pa * a_sh
        ea = jnp.where(in_vreg[0], pltpu.roll(pa, 1, axis=0), 1.0)
        eb = jnp.where(in_vreg[0], pltpu.roll(pb, 1, axis=0), 0.0)
        seg_in = ea * h_in + eb
        for g in range(seg):
            h_g = comp[g][0] * seg_in + comp[g][1]
            h_scr[pl.ds(t0 + c * sub_len + g, SUBLANES, stride=seg), :] = h_g
        h_last = pa * h_in + pb
        return jnp.broadcast_to(h_last[SUBLANES - 1:SUBLANES, :], (SUBLANES, LANES))

    def body(ci, h_in):
        for c in range(n_sub):
            h_in = recurrence(ci, c, h_in)
            conv_and_gates(ci + 1, c)
        return h_in

    for c in range(n_sub):
        conv_and_gates(0, c)
    lax.fori_loop(0, s_len // LRU_CHUNK, body, jnp.zeros((SUBLANES, LANES), F32))
    o_ref[0] = h_scr[...].astype(o_ref.dtype)


def _lru_call(xl, conv_w, conv_b, w_gate, a_b, x_b, lam):
    b, s, c = xl.shape
    row = lambda a: a.reshape(1, c)
    seq = pl.BlockSpec((1, s, LANES), lambda i, j: (i, 0, j))
    vec = pl.BlockSpec((1, LANES), lambda i, j: (0, j))
    return pl.pallas_call(
        _lru_kernel,
        grid=(b, c // LANES),
        in_specs=[seq, pl.BlockSpec((CONV_WIDTH, LANES), lambda i, j: (0, j)), vec,
                  pl.BlockSpec((1, LANES, 2 * LANES), lambda i, j: (j, 0, 0)), vec, vec, vec],
        out_specs=seq,
        out_shape=jax.ShapeDtypeStruct((b, s, c), BF16),
        scratch_shapes=[
            pltpu.VMEM((SUBLANES + s + LRU_CHUNK, LANES), F32),
            pltpu.VMEM((s, LANES), F32),
            pltpu.VMEM((2, LRU_CHUNK, LANES), F32),
            pltpu.VMEM((2, LRU_CHUNK, 2 * LANES), F32),
        ],
        compiler_params=pltpu.CompilerParams(
            dimension_semantics=("arbitrary", "arbitrary"), vmem_limit_bytes=VMEM_LIMIT_BYTES),
        name="rg_lru",
    )(xl, conv_w, row(conv_b), w_gate, row(a_b), row(x_b), row(lam))


def _gated_norm(y_ref, z_ref, nw_ref):
    y = y_ref[0].astype(F32)
    ms = jnp.mean(y * y, axis=-1, keepdims=True)
    return ((y * lax.rsqrt(ms + EPS) * nw_ref[...]) * _silu(z_ref[0].astype(F32))).astype(BF16)


def _outproj_math(ya_ref, za_ref, yl_ref, zl_ref, x_ref, mod_ref, anw_ref, lnw_ref, w_ref):
    y = jnp.dot(_gated_norm(ya_ref, za_ref, anw_ref), w_ref[0:D_ATTN, :], preferred_element_type=F32)
    y = y + jnp.dot(_gated_norm(yl_ref, zl_ref, lnw_ref), w_ref[D_ATTN:, :],
                    preferred_element_type=F32)
    return x_ref[0] + mod_ref[0, 2:3, :] * y


def _outproj_kernel(ya_ref, za_ref, yl_ref, zl_ref, x_ref, mod_ref, anw_ref, lnw_ref, wf_ref, fnw_ref,
                    o_ref, w_ref):
    _cast_weight_once(wf_ref, w_ref)
    out = _outproj_math(ya_ref, za_ref, yl_ref, zl_ref, x_ref, mod_ref, anw_ref, lnw_ref, w_ref)
    ms = jnp.mean(out * out, axis=-1, keepdims=True)
    o_ref[0] = out * lax.rsqrt(ms + EPS) * fnw_ref[...]


def _outin_kernel(ya_ref, za_ref, yl_ref, zl_ref, x_ref, mod_ref, anw_ref, lnw_ref, wof_ref,
                  modn_ref, nwn_ref, wif_ref,
                  xo_ref, q_ref, k_ref, v_ref, zao_ref, xlo_ref, zlo_ref, wo_ref, wi_ref):
    _cast_weight_once(wof_ref, wo_ref)
    _cast_weight_once(wif_ref, wi_ref)
    out = _outproj_math(ya_ref, za_ref, yl_ref, zl_ref, x_ref, mod_ref, anw_ref, lnw_ref, wo_ref)
    xo_ref[0] = out
    _inproj_math(out, modn_ref, nwn_ref, wi_ref, (q_ref, k_ref, v_ref, zao_ref, xlo_ref, zlo_ref))


def _proj_specs(d, layer):
    tm = ROW_TILE
    return dict(
        half=pl.BlockSpec((1, tm, D_ATTN), lambda i, j: (i, j, 0)),
        full=pl.BlockSpec((1, tm, d), lambda i, j: (i, j, 0)),
        mod=pl.BlockSpec((1, 3, d), lambda i, j: (i, 0, 0)),
        nvec=pl.BlockSpec((1, D_ATTN), lambda i, j: (0, 0)),
        dvec=pl.BlockSpec((1, d), lambda i, j: (0, 0)),
        w_out=pl.BlockSpec((1, d, d), lambda i, j: (layer, 0, 0), pipeline_mode=pl.Buffered(1)),
        w_in=pl.BlockSpec((1, d, D_IN), lambda i, j: (layer + 1, 0, 0), pipeline_mode=pl.Buffered(1)),
    )


def _outproj_call(ya, za, yl, zl, x, mod3, anw, lnw, w_out, layer, fnw):
    b, s, d = x.shape
    sp = _proj_specs(d, layer)
    return pl.pallas_call(
        _outproj_kernel,
        grid=(b, s // ROW_TILE),
        in_specs=[sp["half"]] * 4 + [sp["full"], sp["mod"], sp["nvec"], sp["nvec"], sp["w_out"],
                                     sp["dvec"]],
        out_specs=sp["full"],
        out_shape=jax.ShapeDtypeStruct((b, s, d), F32),
        scratch_shapes=[pltpu.VMEM((d, d), BF16)],
        compiler_params=pltpu.CompilerParams(
            dimension_semantics=("arbitrary", "arbitrary"), vmem_limit_bytes=VMEM_LIMIT_BYTES),
        name="outproj",
    )(ya, za, yl, zl, x, mod3, anw.reshape(1, D_ATTN), lnw.reshape(1, D_LRU), w_out,
      fnw.reshape(1, d))


def _outin_call(ya, za, yl, zl, x, mod3, anw, lnw, w_out, layer, mod3_next, norm_w_next, w_in):
    b, s, d = x.shape
    sp = _proj_specs(d, layer)
    blk = lambda dt: jax.ShapeDtypeStruct((b, s, D_ATTN), dt)
    return pl.pallas_call(
        _outin_kernel,
        grid=(b, s // ROW_TILE),
        in_specs=[sp["half"]] * 4 + [sp["full"], sp["mod"], sp["nvec"], sp["nvec"], sp["w_out"],
                                     sp["mod"], sp["dvec"], sp["w_in"]],
        out_specs=[sp["full"]] + [sp["half"]] * 6,
        out_shape=[jax.ShapeDtypeStruct((b, s, d), F32),
                   blk(BF16), blk(BF16), blk(BF16), blk(BF16), blk(F32), blk(BF16)],
        scratch_shapes=[pltpu.VMEM((d, d), BF16), pltpu.VMEM((d, D_IN), BF16)],
        compiler_params=pltpu.CompilerParams(
            dimension_semantics=("arbitrary", "arbitrary"), vmem_limit_bytes=PROJ_VMEM_LIMIT_BYTES),
        name="outproj_inproj",
    )(ya, za, yl, zl, x, mod3, anw.reshape(1, D_ATTN), lnw.reshape(1, D_LRU), w_out,
      mod3_next, norm_w_next.reshape(1, d), w_in)


def _gate_weights(a_w, x_w):
    per_slab = LANES // LRU_BLOCK
    slabs = []
    for j in range(D_LRU // LANES):
        blocks = range(j * per_slab, (j + 1) * per_slab)
        diag = lambda w: jax.scipy.linalg.block_diag(*[w[g] for g in blocks])
        slabs.append(jnp.concatenate([diag(a_w), diag(x_w)], axis=1))
    return (0.5 * jnp.stack(slabs)).astype(BF16)


def kernel(x, c, norm_w, ada_w, ada_b, w_in, conv_w, conv_b, rg_a_w, rg_a_b, rg_x_w, rg_x_b,
           rg_lambda, attn_out_norm_w, lru_out_norm_w, w_out, final_norm_w):
    n_layers = ada_w.shape[0]
    b = x.shape[0]
    mod = _mod_call(c, ada_w, ada_b)
    slopes = jnp.exp2(-8.0 * (jnp.arange(N_HEADS, dtype=F32) + 1.0) / N_HEADS)
    kext = _key_side_table(slopes, x.shape[1])
    mod3 = [mod[l].reshape(b, 3, D_MODEL) for l in range(n_layers)]
    q, k, v, za, xl, zl = _inproj_call(x, mod3[0], norm_w[0], w_in, 0)
    for l in range(n_layers):
        ya = _attn_call(slopes, kext, q, k, v)
        yl = _lru_call(xl, conv_w[l], conv_b[l], _gate_weights(rg_a_w[l], rg_x_w[l]), rg_a_b[l],
                       rg_x_b[l], rg_lambda[l])
        if l + 1 < n_layers:
            x, q, k, v, za, xl, zl = _outin_call(
                ya, za, yl, zl, x, mod3[l], attn_out_norm_w[l], lru_out_norm_w[l], w_out, l,
                mod3[l + 1], norm_w[l + 1], w_in)
        else:
            x = _outproj_call(ya, za, yl, zl, x, mod3[l], attn_out_norm_w[l], lru_out_norm_w[l],
                              w_out, l, final_norm_w)
    return x
```

```python
import math

import jax
import jax.numpy as jnp
from jax import lax
from jax.experimental import pallas as pl
from jax.experimental.pallas import tpu as pltpu

D_MODEL = 1024
D_ATTN = 512
D_LRU = 512
HEAD_DIM = 64
N_HEADS = D_ATTN // HEAD_DIM
N_LRU_BLOCKS = 8
LRU_BLOCK = D_LRU // N_LRU_BLOCKS
CONV_WIDTH = 4
RG_C = 8.0
MOBA_BLOCK = 256
MOBA_TOPK = 3
D_IN = 4 * D_ATTN + 2 * D_LRU
EPS = 1e-6

F32 = jnp.float32
BF16 = jnp.bfloat16
NEG_BIG = -1e30
LOG2E = math.log2(math.e)

LANES = 128
SUBLANES = 8
VMEM_LIMIT_BYTES = 48 * 1024 * 1024
PROJ_VMEM_LIMIT_BYTES = 56 * 1024 * 1024

ROW_TILE = 512
WEIGHT_CAST_ROWS = 64
LRU_CHUNK = 256
LRU_SEGMENT = 4
HEADS_PER_PAIR = LANES // HEAD_DIM
ATTN_PAIRS_PER_STEP = 2


def _sigmoid(v):
    return 0.5 * jnp.tanh(0.5 * v) + 0.5


def _silu(v):
    return v * _sigmoid(v)


def _split_bf16(v):
    hi = v.astype(BF16).astype(F32)
    lo = (v - hi).astype(BF16).astype(F32)
    return hi, lo


def _mod_kernel(c_ref, w_ref, b_ref, o_ref):
    s = _silu(c_ref[...]).astype(BF16)
    w = w_ref[0].astype(BF16)
    o_ref[0] = jnp.dot(s, w, preferred_element_type=F32) + b_ref[0]


def _mod_call(c, ada_w, ada_b):
    n_layers, d, d3 = ada_w.shape
    b = c.shape[0]
    tn = 1024
    return pl.pallas_call(
        _mod_kernel,
        grid=(n_layers, d3 // tn),
        in_specs=[
            pl.BlockSpec((b, d), lambda l, j: (0, 0)),
            pl.BlockSpec((1, d, tn), lambda l, j: (l, 0, j)),
            pl.BlockSpec((1, 1, tn), lambda l, j: (l, 0, j)),
        ],
        out_specs=pl.BlockSpec((1, b, tn), lambda l, j: (l, 0, j)),
        out_shape=jax.ShapeDtypeStruct((n_layers, b, d3), F32),
        compiler_params=pltpu.CompilerParams(
            dimension_semantics=("arbitrary", "arbitrary"), vmem_limit_bytes=VMEM_LIMIT_BYTES),
        name="adaln_mod",
    )(c, ada_w, ada_b.reshape(n_layers, 1, d3))


def _cast_weight_once(w_ref, wbf_scr):
    rows = w_ref.shape[1]

    @pl.when((pl.program_id(0) == 0) & (pl.program_id(1) == 0))
    def _():
        def body(i, carry):
            r0 = pl.multiple_of(i * WEIGHT_CAST_ROWS, WEIGHT_CAST_ROWS)
            wbf_scr[pl.ds(r0, WEIGHT_CAST_ROWS), :] = w_ref[0, pl.ds(r0, WEIGHT_CAST_ROWS), :].astype(BF16)
            return carry
        lax.fori_loop(0, rows // WEIGHT_CAST_ROWS, body, 0)


def _inproj_math(x, mod_ref, nw_ref, w_ref, out_refs):
    ms = jnp.mean(x * x, axis=-1, keepdims=True)
    y = x * lax.rsqrt(ms + EPS) * nw_ref[...]
    shift = mod_ref[0, 0:1, :]
    scale = mod_ref[0, 1:2, :]
    h = (y * (1.0 + scale) + shift).astype(BF16)
    for j, ref in enumerate(out_refs):
        r = jnp.dot(h, w_ref[:, j * D_ATTN:(j + 1) * D_ATTN], preferred_element_type=F32)
        if j == 0:
            r = r * (HEAD_DIM ** -0.5 * LOG2E)
        ref[0] = r.astype(ref.dtype)


def _inproj_kernel(x_ref, mod_ref, nw_ref, wf_ref, q_ref, k_ref, v_ref, za_ref, xl_ref, zl_ref,
                   w_ref):
    _cast_weight_once(wf_ref, w_ref)
    _inproj_math(x_ref[0], mod_ref, nw_ref, w_ref, (q_ref, k_ref, v_ref, za_ref, xl_ref, zl_ref))


def _inproj_call(x, mod3, norm_w, w_in, layer):
    b, s, d = x.shape
    tm = ROW_TILE
    blk = lambda dt: jax.ShapeDtypeStruct((b, s, D_ATTN), dt)
    ospec = pl.BlockSpec((1, tm, D_ATTN), lambda i, j: (i, j, 0))
    return pl.pallas_call(
        _inproj_kernel,
        grid=(b, s // tm),
        in_specs=[
            pl.BlockSpec((1, tm, d), lambda i, j: (i, j, 0)),
            pl.BlockSpec((1, 3, d), lambda i, j: (i, 0, 0)),
            pl.BlockSpec((1, d), lambda i, j: (0, 0)),
            pl.BlockSpec((1, d, D_IN), lambda i, j: (layer, 0, 0), pipeline_mode=pl.Buffered(1)),
        ],
        out_specs=[ospec] * 6,
        out_shape=[blk(BF16), blk(BF16), blk(BF16), blk(BF16), blk(F32), blk(BF16)],
        scratch_shapes=[pltpu.VMEM((d, D_IN), BF16)],
        compiler_params=pltpu.CompilerParams(
            dimension_semantics=("arbitrary", "arbitrary"), vmem_limit_bytes=VMEM_LIMIT_BYTES),
        name="inproj",
    )(x, mod3, norm_w.reshape(1, d), w_in)


def _attn_kernel(slopes_ref, q_ref, k_ref, v_ref, kext_ref, o_ref,
                 kaug_scr, qaug_scr, vaug_scr, s_scr, p_scr, out_scr):
    s_len = q_ref.shape[1]
    n_blk = s_len // MOBA_BLOCK
    n_pairs = q_ref.shape[2] // LANES
    n_heads = n_pairs * HEADS_PER_PAIR
    first_head = pl.program_id(0) * n_heads

    prow = lax.broadcasted_iota(jnp.int32, (n_blk, s_len), 0)
    pcol = lax.broadcasted_iota(jnp.int32, (n_blk, s_len), 1)
    block_mean = jnp.where(pcol // MOBA_BLOCK == prow, 1.0 / MOBA_BLOCK, 0.0).astype(BF16)
    klane = lax.broadcasted_iota(jnp.int32, (s_len, LANES), 1)
    mlane = lax.broadcasted_iota(jnp.int32, (n_blk, LANES), 1)
    n_iota = prow
    q_blk = pcol // MOBA_BLOCK
    key_off = lax.broadcasted_iota(jnp.int32, (MOBA_BLOCK, MOBA_BLOCK), 0)
    qry_off = lax.broadcasted_iota(jnp.int32, (MOBA_BLOCK, MOBA_BLOCK), 1)
    causal = key_off <= qry_off

    def prepare(pp):
        lanes = slice(pp * LANES, (pp + 1) * LANES)
        kp = k_ref[0, :, lanes]
        q_t = q_ref[0, :, lanes].astype(F32).T
        v_t = v_ref[0, :, lanes].astype(F32).T
        kmean = jnp.dot(block_mean, kp, preferred_element_type=F32)
        terms = []
        for hd in range(HEADS_PER_PAIR):
            rest = jnp.where(mlane // HEAD_DIM == hd, kmean, 0.0)
            for _ in range(3):
                part = rest.astype(BF16).astype(F32)
                terms.append(part)
                rest = rest - part
        terms = jnp.concatenate(terms, axis=0).astype(BF16)
        q_tb = q_t.astype(BF16)
        half = s_len // 2
        gates = jnp.concatenate(
            [jnp.dot(terms, q_tb[:, :half], preferred_element_type=F32),
             jnp.dot(terms, q_tb[:, half:], preferred_element_type=F32)], axis=1)
        for hd in range(HEADS_PER_PAIR):
            h = pp * HEADS_PER_PAIR + hd
            kaug_scr[h] = jnp.where(klane // HEAD_DIM == hd, kp, kext_ref[h])
            slope2 = slopes_ref[first_head + h] * LOG2E
            g_hi, g_mid, g_lo = (gates[(3 * hd + t) * n_blk:(3 * hd + t + 1) * n_blk]
                                 for t in range(3))
            g_t = g_hi + g_mid + g_lo
            rank = jnp.zeros((n_blk, s_len), jnp.int32)
            for m in range(n_blk):
                gm = g_t[m:m + 1, :]
                beats = (gm > g_t) | ((gm == g_t) & (m < n_iota))
                rank = rank + jnp.where(beats & (m < q_blk), 1, 0)
            sel = ((n_iota < q_blk) & (rank < MOBA_TOPK)) | (n_iota == q_blk)
            bias_hi, bias_lo = _split_bf16(slope2 * ((n_iota - q_blk) * MOBA_BLOCK).astype(F32))
            selb_hi = jnp.where(sel, bias_hi, NEG_BIG)
            selb_lo = jnp.where(sel, bias_lo, 0.0)
            flag = jnp.where(n_iota < 2, 1.0, 0.0)
            ext = jnp.concatenate(
                [selb_hi, selb_lo, flag, jnp.zeros((HEAD_DIM - 3 * n_blk, s_len), F32)], axis=0)
            q_own = q_t[hd * HEAD_DIM:(hd + 1) * HEAD_DIM]
            halves = [q_own, ext] if hd == 0 else [ext, q_own]
            qaug_scr[h] = jnp.concatenate(halves, axis=0).astype(BF16)
            vaug_scr[h] = jnp.concatenate(
                [v_t[hd * HEAD_DIM:(hd + 1) * HEAD_DIM], jnp.ones((2 * SUBLANES, s_len), F32)],
                axis=0).astype(BF16)

    items = [(h, qb) for h in range(n_heads)
             for qb in (range(n_blk) if h % 2 == 0 else reversed(range(n_blk)))]
    per_pair = HEADS_PER_PAIR * n_blk

    def scores(i):
        h, qb = items[i]
        slot = i % 2
        q0, kk = qb * MOBA_BLOCK, (qb + 1) * MOBA_BLOCK
        qa = qaug_scr[h, :, q0:kk]
        m = None
        for r0 in range(0, kk, MOBA_BLOCK):
            s = jnp.dot(kaug_scr[h, r0:r0 + MOBA_BLOCK, :], qa, preferred_element_type=F32)
            if r0 == q0:
                s = jnp.where(causal, s, NEG_BIG)
            s_scr[slot, r0:r0 + MOBA_BLOCK, :] = s
            mh = jnp.max(s, axis=0, keepdims=True)
            m = mh if m is None else jnp.maximum(m, mh)
        return m

    def probs(i, m):
        _, qb = items[i]
        slot = i % 2
        kk = (qb + 1) * MOBA_BLOCK
        p_scr[slot, 0:kk, :] = jnp.exp2(s_scr[slot, 0:kk, :] - m).astype(BF16)

    def weighted(i):
        h, qb = items[i]
        slot = i % 2
        q0, kk = qb * MOBA_BLOCK, (qb + 1) * MOBA_BLOCK
        o = None
        for r0, r1 in ((0, kk // 2), (kk // 2, kk)):
            part = jnp.dot(vaug_scr[h, :, r0:r1], p_scr[slot, r0:r1, :], preferred_element_type=F32)
            o = part if o is None else o + part
        out_scr[h * HEAD_DIM:(h + 1) * HEAD_DIM, q0:kk] = o[0:HEAD_DIM] / o[HEAD_DIM:HEAD_DIM + 1]

    n_items = len(items)
    prepare(0)
    col_max = {0: scores(0), 1: scores(1)}
    probs(0, col_max.pop(0))
    for i in range(n_items):
        pp, within = divmod(i, per_pair)
        if within == 0 and pp + 1 < n_pairs:
            prepare(pp + 1)
        if i + 2 < n_items:
            col_max[i + 2] = scores(i + 2)
        if i + 1 < n_items:
            probs(i + 1, col_max.pop(i + 1))
        weighted(i)
        if within == per_pair - 1:
            lanes = slice(pp * LANES, (pp + 1) * LANES)
            o_ref[0, :, lanes] = out_scr[lanes, :].T.astype(o_ref.dtype)


def _key_side_table(slopes, s_len):
    n_blk = s_len // MOBA_BLOCK
    shape = (N_HEADS, s_len, LANES)
    head = lax.broadcasted_iota(jnp.int32, shape, 0)
    pos = lax.broadcasted_iota(jnp.int32, shape, 1)
    lane = lax.broadcasted_iota(jnp.int32, shape, 2)
    sub = lane % HEAD_DIM
    foreign = lane // HEAD_DIM != head % HEADS_PER_PAIR
    onehot = jnp.where(sub % n_blk == pos // MOBA_BLOCK, 1.0, 0.0)
    off = (slopes * LOG2E)[:, None, None] * (pos % MOBA_BLOCK).astype(F32)
    off_hi = lax.bitcast_convert_type(
        lax.bitcast_convert_type(off, jnp.uint32) & jnp.uint32(0xFFFF0000), F32)
    ext = jnp.where(sub < 2 * n_blk, onehot,
                    jnp.where(sub == 2 * n_blk, off_hi,
                              jnp.where(sub == 2 * n_blk + 1, off - off_hi, 0.0)))
    return jnp.where(foreign, ext, 0.0).astype(BF16)


def _attn_call(slopes, kext, q, k, v):
    b, s, _ = q.shape
    width = ATTN_PAIRS_PER_STEP * LANES
    heads = ATTN_PAIRS_PER_STEP * HEADS_PER_PAIR
    spec = pl.BlockSpec((1, s, width), lambda j, i, sl: (i, 0, j))
    return pl.pallas_call(
        _attn_kernel,
        grid_spec=pltpu.PrefetchScalarGridSpec(
            num_scalar_prefetch=1,
            grid=(D_ATTN // width, b),
            in_specs=[spec, spec, spec,
                      pl.BlockSpec((heads, s, LANES), lambda j, i, sl: (j, 0, 0),
                                   pipeline_mode=pl.Buffered(1))],
            out_specs=spec,
            scratch_shapes=[
                pltpu.VMEM((heads, s, LANES), BF16),
                pltpu.VMEM((heads, LANES, s), BF16),
                pltpu.VMEM((heads, HEAD_DIM + 2 * SUBLANES, s), BF16),
                pltpu.VMEM((2, s, MOBA_BLOCK), F32),
                pltpu.VMEM((2, s, MOBA_BLOCK), BF16),
                pltpu.VMEM((width, s), F32),
            ],
        ),
        out_shape=jax.ShapeDtypeStruct((b, s, D_ATTN), BF16),
        compiler_params=pltpu.CompilerParams(
            dimension_semantics=("arbitrary", "arbitrary"), vmem_limit_bytes=VMEM_LIMIT_BYTES),
        name="moba_attn",
    )(slopes, q, k, v, kext)


def _lru_kernel(xl_ref, cw_ref, cb_ref, w_ref, ab_ref, xb_ref, lam_ref, o_ref,
                head_scr, h_scr, xc_scr, g_scr):
    s_len = xl_ref.shape[1]
    seg = LRU_SEGMENT
    sub_len = seg * SUBLANES
    n_sub = LRU_CHUNK // sub_len

    n_chunks = s_len // LRU_CHUNK
    x_seq = xl_ref.at[0]
    head_scr[0:SUBLANES, :] = jnp.zeros((SUBLANES, LANES), F32)
    head_scr[SUBLANES:, :] = xl_ref[0, 0:sub_len, :]

    neg_lam = -lam_ref[...]
    softplus = jnp.maximum(neg_lam, 0.0) + jnp.log1p(jnp.exp(-jnp.abs(neg_lam)))
    half_rate = (0.5 * RG_C) * softplus
    ab_half = 0.5 * ab_ref[...]
    xb_half = 0.5 * xb_ref[...]
    sub = lax.broadcasted_iota(jnp.int32, (SUBLANES, LANES), 0)
    steps = (1, 2, 4)
    in_vreg = [sub >= step for step in steps]

    def strided(ref, start):
        return ref[pl.ds(start, SUBLANES, stride=seg), :]

    def chunk_start(ci):
        return ci * LRU_CHUNK if isinstance(ci, int) else pl.multiple_of(ci * LRU_CHUNK, LRU_CHUNK)

    def conv_and_gates(ci, c):
        rows = slice(c * sub_len, (c + 1) * sub_len)
        if isinstance(ci, int) and ci == 0 and c == 0:
            src, t0 = head_scr, SUBLANES
        else:
            src, t0 = x_seq, chunk_start(ci if isinstance(ci, int) else jnp.minimum(ci, n_chunks - 1))
        cache = {}
        xc_parts = []
        for g in range(seg):
            acc = cb_ref[...]
            for j in range(CONV_WIDTH):
                off = c * sub_len + g - (CONV_WIDTH - 1) + j
                if off not in cache:
                    cache[off] = strided(src, t0 + off)
                acc = acc + cw_ref[j:j + 1, :] * cache[off]
            xc_parts.append(acc)
        xc = jnp.concatenate(xc_parts, axis=0)
        xc_scr[ci % 2, rows, :] = xc
        g_scr[ci % 2, rows, :] = jnp.dot(xc.astype(BF16), w_ref[0], preferred_element_type=F32)

    def recurrence(ci, c, h_in):
        t0 = chunk_start(ci)
        rows = slice(c * sub_len, (c + 1) * sub_len)
        xc = xc_scr[ci % 2, rows, :]
        t_r = jnp.tanh(g_scr[ci % 2, rows, :LANES] + ab_half)
        t_i = jnp.tanh(g_scr[ci % 2, rows, LANES:] + xb_half)
        neg_log_a = half_rate * t_r + half_rate
        a = jnp.exp(-neg_log_a)
        y = jnp.tanh(neg_log_a) * (a * a + 1.0)
        root = jnp.where(y == 0.0, 0.0, y * lax.rsqrt(y))
        bterm = root * ((t_i + 1.0) * (0.5 * xc))

        comp = []
        for g in range(seg):
            av, bv = a[g * SUBLANES:(g + 1) * SUBLANES], bterm[g * SUBLANES:(g + 1) * SUBLANES]
            if g > 0:
                bv = av * comp[-1][1] + bv
                av = av * comp[-1][0]
            comp.append((av, bv))
        pa, pb = comp[-1]
        for step, valid in zip(steps, in_vreg):
            a_sh = jnp.where(valid, pltpu.roll(pa, step, axis=0), 1.0)
            b_sh = jnp.where(valid, pltpu.roll(pb, step, axis=0), 0.0)
            pb = pa * b_sh + pb
            pa = pa * a_sh
        ea = jnp.where(in_vreg[0], pltpu.roll(pa, 1, axis=0), 1.0)
        eb = jnp.where(in_vreg[0], pltpu.roll(pb, 1, axis=0), 0.0)
        seg_in = ea * h_in + eb
        for g in range(seg):
            h_g = comp[g][0] * seg_in + comp[g][1]
            h_scr[pl.ds(t0 + c * sub_len + g, SUBLANES, stride=seg), :] = h_g
        h_last = pa * h_in + pb
        return jnp.broadcast_to(h_last[SUBLANES - 1:SUBLANES, :], (SUBLANES, LANES))

    def body(ci, h_in):
        for c in range(n_sub):
            h_in = recurrence(ci, c, h_in)
            conv_and_gates(ci + 1, c)
        return h_in

    for c in range(n_sub):
        conv_and_gates(0, c)
    lax.fori_loop(0, s_len // LRU_CHUNK, body, jnp.zeros((SUBLANES, LANES), F32))
    o_ref[0] = h_scr[...].astype(o_ref.dtype)


def _lru_call(xl, conv_w, conv_b, w_gate, a_b, x_b, lam):
    b, s, c = xl.shape
    row = lambda a: a.reshape(1, c)
    seq = pl.BlockSpec((1, s, LANES), lambda i, j: (i, 0, j))
    vec = pl.BlockSpec((1, LANES), lambda i, j: (0, j))
    return pl.pallas_call(
        _lru_kernel,
        grid=(b, c // LANES),
        in_specs=[seq, pl.BlockSpec((CONV_WIDTH, LANES), lambda i, j: (0, j)), vec,
                  pl.BlockSpec((1, LANES, 2 * LANES), lambda i, j: (j, 0, 0)), vec, vec, vec],
        out_specs=seq,
        out_shape=jax.ShapeDtypeStruct((b, s, c), BF16),
        scratch_shapes=[
            pltpu.VMEM((SUBLANES + LRU_SEGMENT * SUBLANES, LANES), F32),
            pltpu.VMEM((s, LANES), F32),
            pltpu.VMEM((2, LRU_CHUNK, LANES), F32),
            pltpu.VMEM((2, LRU_CHUNK, 2 * LANES), F32),
        ],
        compiler_params=pltpu.CompilerParams(
            dimension_semantics=("arbitrary", "arbitrary"), vmem_limit_bytes=VMEM_LIMIT_BYTES),
        name="rg_lru",
    )(xl, conv_w, row(conv_b), w_gate, row(a_b), row(x_b), row(lam))


def _gated_norm(y_ref, z_ref, nw_ref):
    y = y_ref[0].astype(F32)
    ms = jnp.mean(y * y, axis=-1, keepdims=True)
    return ((y * lax.rsqrt(ms + EPS) * nw_ref[...]) * _silu(z_ref[0].astype(F32))).astype(BF16)


def _outproj_math(ya_ref, za_ref, yl_ref, zl_ref, x_ref, mod_ref, anw_ref, lnw_ref, w_ref):
    y = jnp.dot(_gated_norm(ya_ref, za_ref, anw_ref), w_ref[0:D_ATTN, :], preferred_element_type=F32)
    y = y + jnp.dot(_gated_norm(yl_ref, zl_ref, lnw_ref), w_ref[D_ATTN:, :],
                    preferred_element_type=F32)
    return x_ref[0] + mod_ref[0, 2:3, :] * y


def _outproj_kernel(ya_ref, za_ref, yl_ref, zl_ref, x_ref, mod_ref, anw_ref, lnw_ref, wf_ref, fnw_ref,
                    o_ref, w_ref):
    _cast_weight_once(wf_ref, w_ref)
    out = _outproj_math(ya_ref, za_ref, yl_ref, zl_ref, x_ref, mod_ref, anw_ref, lnw_ref, w_ref)
    ms = jnp.mean(out * out, axis=-1, keepdims=True)
    o_ref[0] = out * lax.rsqrt(ms + EPS) * fnw_ref[...]


def _outin_kernel(ya_ref, za_ref, yl_ref, zl_ref, x_ref, mod_ref, anw_ref, lnw_ref, wof_ref,
                  modn_ref, nwn_ref, wif_ref,
                  xo_ref, q_ref, k_ref, v_ref, zao_ref, xlo_ref, zlo_ref, wo_ref, wi_ref):
    _cast_weight_once(wof_ref, wo_ref)
    _cast_weight_once(wif_ref, wi_ref)
    out = _outproj_math(ya_ref, za_ref, yl_ref, zl_ref, x_ref, mod_ref, anw_ref, lnw_ref, wo_ref)
    xo_ref[0] = out
    _inproj_math(out, modn_ref, nwn_ref, wi_ref, (q_ref, k_ref, v_ref, zao_ref, xlo_ref, zlo_ref))


def _proj_specs(d, layer):
    tm = ROW_TILE
    return dict(
        half=pl.BlockSpec((1, tm, D_ATTN), lambda i, j: (i, j, 0)),
        full=pl.BlockSpec((1, tm, d), lambda i, j: (i, j, 0)),
        mod=pl.BlockSpec((1, 3, d), lambda i, j: (i, 0, 0)),
        nvec=pl.BlockSpec((1, D_ATTN), lambda i, j: (0, 0)),
        dvec=pl.BlockSpec((1, d), lambda i, j: (0, 0)),
        w_out=pl.BlockSpec((1, d, d), lambda i, j: (layer, 0, 0), pipeline_mode=pl.Buffered(1)),
        w_in=pl.BlockSpec((1, d, D_IN), lambda i, j: (layer + 1, 0, 0), pipeline_mode=pl.Buffered(1)),
    )


def _outproj_call(ya, za, yl, zl, x, mod3, anw, lnw, w_out, layer, fnw):
    b, s, d = x.shape
    sp = _proj_specs(d, layer)
    return pl.pallas_call(
        _outproj_kernel,
        grid=(b, s // ROW_TILE),
        in_specs=[sp["half"]] * 4 + [sp["full"], sp["mod"], sp["nvec"], sp["nvec"], sp["w_out"],
                                     sp["dvec"]],
        out_specs=sp["full"],
        out_shape=jax.ShapeDtypeStruct((b, s, d), F32),
        scratch_shapes=[pltpu.VMEM((d, d), BF16)],
        compiler_params=pltpu.CompilerParams(
            dimension_semantics=("arbitrary", "arbitrary"), vmem_limit_bytes=VMEM_LIMIT_BYTES),
        name="outproj",
    )(ya, za, yl, zl, x, mod3, anw.reshape(1, D_ATTN), lnw.reshape(1, D_LRU), w_out,
      fnw.reshape(1, d))


def _outin_call(ya, za, yl, zl, x, mod3, anw, lnw, w_out, layer, mod3_next, norm_w_next, w_in):
    b, s, d = x.shape
    sp = _proj_specs(d, layer)
    blk = lambda dt: jax.ShapeDtypeStruct((b, s, D_ATTN), dt)
    return pl.pallas_call(
        _outin_kernel,
        grid=(b, s // ROW_TILE),
        in_specs=[sp["half"]] * 4 + [sp["full"], sp["mod"], sp["nvec"], sp["nvec"], sp["w_out"],
                                     sp["mod"], sp["dvec"], sp["w_in"]],
        out_specs=[sp["full"]] + [sp["half"]] * 6,
        out_shape=[jax.ShapeDtypeStruct((b, s, d), F32),
                   blk(BF16), blk(BF16), blk(BF16), blk(BF16), blk(F32), blk(BF16)],
        scratch_shapes=[pltpu.VMEM((d, d), BF16), pltpu.VMEM((d, D_IN), BF16)],
        compiler_params=pltpu.CompilerParams(
            dimension_semantics=("arbitrary", "arbitrary"), vmem_limit_bytes=PROJ_VMEM_LIMIT_BYTES),
        name="outproj_inproj",
    )(ya, za, yl, zl, x, mod3, anw.reshape(1, D_ATTN), lnw.reshape(1, D_LRU), w_out,
      mod3_next, norm_w_next.reshape(1, d), w_in)


def _gate_weights(a_w, x_w):
    per_slab = LANES // LRU_BLOCK
    slabs = []
    for j in range(D_LRU // LANES):
        blocks = range(j * per_slab, (j + 1) * per_slab)
        diag = lambda w: jax.scipy.linalg.block_diag(*[w[g] for g in blocks])
        slabs.append(jnp.concatenate([diag(a_w), diag(x_w)], axis=1))
    return (0.5 * jnp.stack(slabs)).astype(BF16)


def kernel(x, c, norm_w, ada_w, ada_b, w_in, conv_w, conv_b, rg_a_w, rg_a_b, rg_x_w, rg_x_b,
           rg_lambda, attn_out_norm_w, lru_out_norm_w, w_out, final_norm_w):
    n_layers = ada_w.shape[0]
    b = x.shape[0]
    mod = _mod_call(c, ada_w, ada_b)
    slopes = jnp.exp2(-8.0 * (jnp.arange(N_HEADS, dtype=F32) + 1.0) / N_HEADS)
    kext = _key_side_table(slopes, x.shape[1])
    mod3 = [mod[l].reshape(b, 3, D_MODEL) for l in range(n_layers)]
    q, k, v, za, xl, zl = _inproj_call(x, mod3[0], norm_w[0], w_in, 0)
    for l in range(n_layers):
        ya = _attn_call(slopes, kext, q, k, v)
        yl = _lru_call(xl, conv_w[l], conv_b[l], _gate_weights(rg_a_w[l], rg_x_w[l]), rg_a_b[l],
                       rg_x_b[l], rg_lambda[l])
        if l + 1 < n_layers:
            x, q, k, v, za, xl, zl = _outin_call(
                ya, za, yl, zl, x, mod3[l], attn_out_norm_w[l], lru_out_norm_w[l], w_out, l,
                mod3[l + 1], norm_w[l + 1], w_in)
        else:
            x = _outproj_call(ya, za, yl, zl, x, mod3[l], attn_out_norm_w[l], lru_out_norm_w[l],
                              w_out, l, final_norm_w)
    return x
```

```python
import math

import jax
import jax.numpy as jnp
from jax import lax
from jax.experimental import pallas as pl
from jax.experimental.pallas import tpu as pltpu

D_MODEL = 1024
D_ATTN = 512
D_LRU = 512
HEAD_DIM = 64
N_HEADS = D_ATTN // HEAD_DIM
N_LRU_BLOCKS = 8
LRU_BLOCK = D_LRU // N_LRU_BLOCKS
CONV_WIDTH = 4
RG_C = 8.0
MOBA_BLOCK = 256
MOBA_TOPK = 3
D_IN = 4 * D_ATTN + 2 * D_LRU
EPS = 1e-6

F32 = jnp.float32
BF16 = jnp.bfloat16
NEG_BIG = -1e30
LOG2E = math.log2(math.e)

LANES = 128
SUBLANES = 8
VMEM_LIMIT_BYTES = 48 * 1024 * 1024
PROJ_VMEM_LIMIT_BYTES = 56 * 1024 * 1024

ROW_TILE = 512
OUT_ROW_TILE = 1024
WEIGHT_CAST_ROWS = 64
LRU_CHUNK = 256
LRU_SEGMENT = 4
HEADS_PER_PAIR = LANES // HEAD_DIM
FEATURE_MAJOR_OUTPUTS = (0, 2)
ATTN_PAIRS_PER_STEP = 2
PREPARE_AT = 6


def _sigmoid(v):
    return 0.5 * jnp.tanh(0.5 * v) + 0.5


def _silu(v):
    return v * _sigmoid(v)


def _split_bf16(v):
    hi = v.astype(BF16).astype(F32)
    lo = (v - hi).astype(BF16).astype(F32)
    return hi, lo


def _mod_kernel(c_ref, w_ref, b_ref, o_ref):
    s = _silu(c_ref[...]).astype(BF16)
    w = w_ref[0].astype(BF16)
    o_ref[0] = jnp.dot(s, w, preferred_element_type=F32) + b_ref[0]


def _mod_call(c, ada_w, ada_b):
    n_layers, d, d3 = ada_w.shape
    b = c.shape[0]
    tn = 1024
    return pl.pallas_call(
        _mod_kernel,
        grid=(n_layers, d3 // tn),
        in_specs=[
            pl.BlockSpec((b, d), lambda l, j: (0, 0)),
            pl.BlockSpec((1, d, tn), lambda l, j: (l, 0, j)),
            pl.BlockSpec((1, 1, tn), lambda l, j: (l, 0, j)),
        ],
        out_specs=pl.BlockSpec((1, b, tn), lambda l, j: (l, 0, j)),
        out_shape=jax.ShapeDtypeStruct((n_layers, b, d3), F32),
        compiler_params=pltpu.CompilerParams(
            dimension_semantics=("arbitrary", "arbitrary"), vmem_limit_bytes=VMEM_LIMIT_BYTES),
        name="adaln_mod",
    )(c, ada_w, ada_b.reshape(n_layers, 1, d3))


def _cast_weight_once(w_ref, wbf_scr):
    rows = w_ref.shape[1]

    @pl.when((pl.program_id(0) == 0) & (pl.program_id(1) == 0))
    def _():
        def body(i, carry):
            r0 = pl.multiple_of(i * WEIGHT_CAST_ROWS, WEIGHT_CAST_ROWS)
            wbf_scr[pl.ds(r0, WEIGHT_CAST_ROWS), :] = w_ref[0, pl.ds(r0, WEIGHT_CAST_ROWS), :].astype(BF16)
            return carry
        lax.fori_loop(0, rows // WEIGHT_CAST_ROWS, body, 0)


def _inproj_math(x, mod_ref, nw_ref, w_ref, out_refs):
    ms = jnp.mean(x * x, axis=-1, keepdims=True)
    y = x * lax.rsqrt(ms + EPS) * nw_ref[...]
    shift = mod_ref[0, 0:1, :]
    scale = mod_ref[0, 1:2, :]
    h = (y * (1.0 + scale) + shift).astype(BF16)
    for j, ref in enumerate(out_refs):
        r = jnp.dot(h, w_ref[:, j * D_ATTN:(j + 1) * D_ATTN], preferred_element_type=F32)
        if j == 0:
            r = r * (HEAD_DIM ** -0.5 * LOG2E)
        if j in FEATURE_MAJOR_OUTPUTS:
            r = r.T
        ref[0] = r.astype(ref.dtype)


def _inproj_kernel(x_ref, mod_ref, nw_ref, wf_ref, q_ref, k_ref, v_ref, za_ref, xl_ref, zl_ref,
                   w_ref):
    _cast_weight_once(wf_ref, w_ref)
    _inproj_math(x_ref[0], mod_ref, nw_ref, w_ref, (q_ref, k_ref, v_ref, za_ref, xl_ref, zl_ref))


def _inproj_outputs(b, s):
    tm = ROW_TILE
    token_major = pl.BlockSpec((1, tm, D_ATTN), lambda i, j: (i, j, 0))
    feature_major = pl.BlockSpec((1, D_ATTN, tm), lambda i, j: (i, 0, j))
    dtypes = (BF16, BF16, BF16, BF16, F32, BF16)
    specs, shapes = [], []
    for j, dt in enumerate(dtypes):
        fm = j in FEATURE_MAJOR_OUTPUTS
        specs.append(feature_major if fm else token_major)
        shapes.append(jax.ShapeDtypeStruct((b, D_ATTN, s) if fm else (b, s, D_ATTN), dt))
    return specs, shapes


def _inproj_call(x, mod3, norm_w, w_in, layer):
    b, s, d = x.shape
    tm = ROW_TILE
    out_specs, out_shape = _inproj_outputs(b, s)
    return pl.pallas_call(
        _inproj_kernel,
        grid=(b, s // tm),
        in_specs=[
            pl.BlockSpec((1, tm, d), lambda i, j: (i, j, 0)),
            pl.BlockSpec((1, 3, d), lambda i, j: (i, 0, 0)),
            pl.BlockSpec((1, d), lambda i, j: (0, 0)),
            pl.BlockSpec((1, d, D_IN), lambda i, j: (layer, 0, 0), pipeline_mode=pl.Buffered(1)),
        ],
        out_specs=out_specs,
        out_shape=out_shape,
        scratch_shapes=[pltpu.VMEM((d, D_IN), BF16)],
        compiler_params=pltpu.CompilerParams(
            dimension_semantics=("arbitrary", "arbitrary"), vmem_limit_bytes=VMEM_LIMIT_BYTES),
        name="inproj",
    )(x, mod3, norm_w.reshape(1, d), w_in)


def _attn_kernel(slopes_ref, q_ref, k_ref, v_ref, kext_ref, o_ref,
                 kaug_scr, qaug_scr, vaug_scr, s_scr, p_scr, out_scr):
    s_len = k_ref.shape[1]
    n_blk = s_len // MOBA_BLOCK
    n_pairs = k_ref.shape[2] // LANES
    n_heads = n_pairs * HEADS_PER_PAIR
    first_head = pl.program_id(0) * n_heads

    prow = lax.broadcasted_iota(jnp.int32, (n_blk, s_len), 0)
    pcol = lax.broadcasted_iota(jnp.int32, (n_blk, s_len), 1)
    block_mean = jnp.where(pcol // MOBA_BLOCK == prow, 1.0 / MOBA_BLOCK, 0.0).astype(BF16)
    klane = lax.broadcasted_iota(jnp.int32, (s_len, LANES), 1)
    mlane = lax.broadcasted_iota(jnp.int32, (n_blk, LANES), 1)
    n_iota = prow
    q_blk = pcol // MOBA_BLOCK
    key_off = lax.broadcasted_iota(jnp.int32, (MOBA_BLOCK, MOBA_BLOCK), 0)
    qry_off = lax.broadcasted_iota(jnp.int32, (MOBA_BLOCK, MOBA_BLOCK), 1)
    causal = key_off <= qry_off

    def prepare(pp):
        lanes = slice(pp * LANES, (pp + 1) * LANES)
        kp = k_ref[0, :, lanes]
        q_t = q_ref[0, lanes, :]
        v_t = v_ref[0, lanes, :]
        kmean = jnp.dot(block_mean, kp, preferred_element_type=F32)
        terms = []
        for hd in range(HEADS_PER_PAIR):
            rest = jnp.where(mlane // HEAD_DIM == hd, kmean, 0.0)
            for _ in range(3):
                part = rest.astype(BF16).astype(F32)
                terms.append(part)
                rest = rest - part
        terms = jnp.concatenate(terms, axis=0).astype(BF16)
        half = s_len // 2
        gates = jnp.concatenate(
            [jnp.dot(terms, q_t[:, :half], preferred_element_type=F32),
             jnp.dot(terms, q_t[:, half:], preferred_element_type=F32)], axis=1)
        for hd in range(HEADS_PER_PAIR):
            h = pp * HEADS_PER_PAIR + hd
            kaug_scr[h] = jnp.where(klane // HEAD_DIM == hd, kp, kext_ref[h])
            slope2 = slopes_ref[first_head + h] * LOG2E
            g_hi, g_mid, g_lo = (gates[(3 * hd + t) * n_blk:(3 * hd + t + 1) * n_blk]
                                 for t in range(3))
            g_t = g_hi + g_mid + g_lo
            rank = jnp.zeros((n_blk, s_len), jnp.int32)
            for m in range(n_blk):
                gm = g_t[m:m + 1, :]
                beats = (gm > g_t) | ((gm == g_t) & (m < n_iota))
                rank = rank + jnp.where(beats & (m < q_blk), 1, 0)
            sel = ((n_iota < q_blk) & (rank < MOBA_TOPK)) | (n_iota == q_blk)
            bias_hi, bias_lo = _split_bf16(slope2 * ((n_iota - q_blk) * MOBA_BLOCK).astype(F32))
            selb_hi = jnp.where(sel, bias_hi, NEG_BIG)
            selb_lo = jnp.where(sel, bias_lo, 0.0)
            flag = jnp.where(n_iota < 2, 1.0, 0.0)
            ext = jnp.concatenate(
                [selb_hi, selb_lo, flag, jnp.zeros((HEAD_DIM - 3 * n_blk, s_len), F32)], axis=0)
            own = slice(hd * HEAD_DIM, (hd + 1) * HEAD_DIM)
            other = slice((1 - hd) * HEAD_DIM, (2 - hd) * HEAD_DIM)
            qaug_scr[h, own, :] = q_t[own]
            qaug_scr[h, other, :] = ext.astype(BF16)
            vaug_scr[h, 0:HEAD_DIM, :] = v_t[own]
            vaug_scr[h, HEAD_DIM:, :] = jnp.ones((2 * SUBLANES, s_len), BF16)

    items = [(h, qb) for h in range(n_heads)
             for qb in (range(n_blk) if h % 2 == 0 else reversed(range(n_blk)))]
    per_pair = HEADS_PER_PAIR * n_blk

    def scores(i):
        h, qb = items[i]
        slot = i % 2
        q0, kk = qb * MOBA_BLOCK, (qb + 1) * MOBA_BLOCK
        qa = qaug_scr[h, :, q0:kk]
        m = None
        for r0 in range(0, kk, MOBA_BLOCK):
            s = jnp.dot(kaug_scr[h, r0:r0 + MOBA_BLOCK, :], qa, preferred_element_type=F32)
            if r0 == q0:
                s = jnp.where(causal, s, NEG_BIG)
            s_scr[slot, r0:r0 + MOBA_BLOCK, :] = s
            mh = jnp.max(s, axis=0, keepdims=True)
            m = mh if m is None else jnp.maximum(m, mh)
        return m

    def probs(i, m):
        _, qb = items[i]
        slot = i % 2
        kk = (qb + 1) * MOBA_BLOCK
        p_scr[slot, 0:kk, :] = jnp.exp2(s_scr[slot, 0:kk, :] - m).astype(BF16)

    def weighted(i):
        h, qb = items[i]
        slot = i % 2
        q0, kk = qb * MOBA_BLOCK, (qb + 1) * MOBA_BLOCK
        o = None
        for r0, r1 in ((0, kk // 2), (kk // 2, kk)):
            part = jnp.dot(vaug_scr[h, :, r0:r1], p_scr[slot, r0:r1, :], preferred_element_type=F32)
            o = part if o is None else o + part
        out_scr[h * HEAD_DIM:(h + 1) * HEAD_DIM, q0:kk] = o[0:HEAD_DIM] / o[HEAD_DIM:HEAD_DIM + 1]

    n_items = len(items)
    prepare(0)
    col_max = {0: scores(0), 1: scores(1)}
    probs(0, col_max.pop(0))
    for i in range(n_items):
        pp, within = divmod(i, per_pair)
        if within == PREPARE_AT and pp + 1 < n_pairs:
            prepare(pp + 1)
        if i + 2 < n_items:
            col_max[i + 2] = scores(i + 2)
        if i + 1 < n_items:
            probs(i + 1, col_max.pop(i + 1))
        weighted(i)
        if within == per_pair - 1:
            lanes = slice(pp * LANES, (pp + 1) * LANES)
            o_ref[0, :, lanes] = out_scr[lanes, :].T.astype(o_ref.dtype)


def _key_side_table(slopes, s_len):
    n_blk = s_len // MOBA_BLOCK
    shape = (N_HEADS, s_len, LANES)
    head = lax.broadcasted_iota(jnp.int32, shape, 0)
    pos = lax.broadcasted_iota(jnp.int32, shape, 1)
    lane = lax.broadcasted_iota(jnp.int32, shape, 2)
    sub = lane % HEAD_DIM
    foreign = lane // HEAD_DIM != head % HEADS_PER_PAIR
    onehot = jnp.where(sub % n_blk == pos // MOBA_BLOCK, 1.0, 0.0)
    off = (slopes * LOG2E)[:, None, None] * (pos % MOBA_BLOCK).astype(F32)
    off_hi = lax.bitcast_convert_type(
        lax.bitcast_convert_type(off, jnp.uint32) & jnp.uint32(0xFFFF0000), F32)
    ext = jnp.where(sub < 2 * n_blk, onehot,
                    jnp.where(sub == 2 * n_blk, off_hi,
                              jnp.where(sub == 2 * n_blk + 1, off - off_hi, 0.0)))
    return jnp.where(foreign, ext, 0.0).astype(BF16)


def _attn_call(slopes, kext, q_t, k, v_t):
    b, s, _ = k.shape
    width = ATTN_PAIRS_PER_STEP * LANES
    heads = ATTN_PAIRS_PER_STEP * HEADS_PER_PAIR
    spec = pl.BlockSpec((1, s, width), lambda j, i, sl: (i, 0, j))
    spec_t = pl.BlockSpec((1, width, s), lambda j, i, sl: (i, j, 0))
    return pl.pallas_call(
        _attn_kernel,
        grid_spec=pltpu.PrefetchScalarGridSpec(
            num_scalar_prefetch=1,
            grid=(D_ATTN // width, b),
            in_specs=[spec_t, spec, spec_t,
                      pl.BlockSpec((heads, s, LANES), lambda j, i, sl: (j, 0, 0),
                                   pipeline_mode=pl.Buffered(1))],
            out_specs=spec,
            scratch_shapes=[
                pltpu.VMEM((heads, s, LANES), BF16),
                pltpu.VMEM((heads, LANES, s), BF16),
                pltpu.VMEM((heads, HEAD_DIM + 2 * SUBLANES, s), BF16),
                pltpu.VMEM((2, s, MOBA_BLOCK), F32),
                pltpu.VMEM((2, s, MOBA_BLOCK), BF16),
                pltpu.VMEM((width, s), F32),
            ],
        ),
        out_shape=jax.ShapeDtypeStruct((b, s, D_ATTN), BF16),
        compiler_params=pltpu.CompilerParams(
            dimension_semantics=("arbitrary", "arbitrary"), vmem_limit_bytes=VMEM_LIMIT_BYTES),
        name="moba_attn",
    )(slopes, q_t, k, v_t, kext)


def _lru_kernel(xl_ref, cw_ref, cb_ref, w_ref, ab_ref, xb_ref, lam_ref, o_ref,
                head_scr, h_scr, xc_scr, g_scr):
    s_len = xl_ref.shape[1]
    seg = LRU_SEGMENT
    sub_len = seg * SUBLANES
    n_sub = LRU_CHUNK // sub_len

    n_chunks = s_len // LRU_CHUNK
    x_seq = xl_ref.at[0]
    head_scr[0:SUBLANES, :] = jnp.zeros((SUBLANES, LANES), F32)
    head_scr[SUBLANES:, :] = xl_ref[0, 0:sub_len, :]

    neg_lam = -lam_ref[...]
    softplus = jnp.maximum(neg_lam, 0.0) + jnp.log1p(jnp.exp(-jnp.abs(neg_lam)))
    half_rate = (0.5 * RG_C) * softplus
    ab_half = 0.5 * ab_ref[...]
    xb_half = 0.5 * xb_ref[...]
    sub = lax.broadcasted_iota(jnp.int32, (SUBLANES, LANES), 0)
    steps = (1, 2, 4)
    in_vreg = [sub >= step for step in steps]

    def strided(ref, start):
        return ref[pl.ds(start, SUBLANES, stride=seg), :]

    def chunk_start(ci):
        return ci * LRU_CHUNK if isinstance(ci, int) else pl.multiple_of(ci * LRU_CHUNK, LRU_CHUNK)

    def conv_and_gates(ci, c):
        rows = slice(c * sub_len, (c + 1) * sub_len)
        if isinstance(ci, int) and ci == 0 and c == 0:
            src, t0 = head_scr, SUBLANES
        else:
            last = n_chunks - 1
            src, t0 = x_seq, chunk_start(min(ci, last) if isinstance(ci, int) else jnp.minimum(ci, last))
        cache = {}
        xc_parts = []
        for g in range(seg):
            acc = cb_ref[...]
            for j in range(CONV_WIDTH):
                off = c * sub_len + g - (CONV_WIDTH - 1) + j
                if off not in cache:
                    cache[off] = strided(src, t0 + off)
                acc = acc + cw_ref[j:j + 1, :] * cache[off]
            xc_parts.append(acc)
        xc = jnp.concatenate(xc_parts, axis=0)
        xc_scr[ci % 2, rows, :] = xc
        g_scr[ci % 2, rows, :] = jnp.dot(xc.astype(BF16), w_ref[0], preferred_element_type=F32)

    def recurrence(ci, c, h_in):
        t0 = chunk_start(ci)
        rows = slice(c * sub_len, (c + 1) * sub_len)
        xc = xc_scr[ci % 2, rows, :]
        t_r = jnp.tanh(g_scr[ci % 2, rows, :LANES] + ab_half)
        t_i = jnp.tanh(g_scr[ci % 2, rows, LANES:] + xb_half)
        neg_log_a = half_rate * t_r + half_rate
        a = jnp.exp(-neg_log_a)
        y = jnp.tanh(neg_log_a) * (a * a + 1.0)
        root = jnp.where(y == 0.0, 0.0, y * lax.rsqrt(y))
        bterm = root * ((t_i + 1.0) * (0.5 * xc))

        comp = []
        for g in range(seg):
            av, bv = a[g * SUBLANES:(g + 1) * SUBLANES], bterm[g * SUBLANES:(g + 1) * SUBLANES]
            if g > 0:
                bv = av * comp[-1][1] + bv
                av = av * comp[-1][0]
            comp.append((av, bv))
        pa, pb = comp[-1]
        for step, valid in zip(steps, in_vreg):
            a_sh = jnp.where(valid, pltpu.roll(pa, step, axis=0), 1.0)
            b_sh = jnp.where(valid, pltpu.roll(pb, step, axis=0), 0.0)
            pb = pa * b_sh + pb
            pa = pa * a_sh
        ea = jnp.where(in_vreg[0], pltpu.roll(pa, 1, axis=0), 1.0)
        eb = jnp.where(in_vreg[0], pltpu.roll(pb, 1, axis=0), 0.0)
        seg_in = ea * h_in + eb
        for g in range(seg):
            h_g = comp[g][0] * seg_in + comp[g][1]
            h_scr[pl.ds(t0 + c * sub_len + g, SUBLANES, stride=seg), :] = h_g
        h_last = pa * h_in + pb
        return jnp.broadcast_to(h_last[SUBLANES - 1:SUBLANES, :], (SUBLANES, LANES))

    def body(ci, h_in):
        for c in range(n_sub):
            h_in = recurrence(ci, c, h_in)
            conv_and_gates(ci + 1, c)
        return h_in

    for c in range(n_sub):
        conv_and_gates(0, c)
    lax.fori_loop(0, s_len // LRU_CHUNK, body, jnp.zeros((SUBLANES, LANES), F32))
    o_ref[0] = h_scr[...].astype(o_ref.dtype)


def _lru_call(xl, conv_w, conv_b, w_gate, a_b, x_b, lam):
    b, s, c = xl.shape
    row = lambda a: a.reshape(1, c)
    seq = pl.BlockSpec((1, s, LANES), lambda i, j: (i, 0, j))
    vec = pl.BlockSpec((1, LANES), lambda i, j: (0, j))
    return pl.pallas_call(
        _lru_kernel,
        grid=(b, c // LANES),
        in_specs=[seq, pl.BlockSpec((CONV_WIDTH, LANES), lambda i, j: (0, j)), vec,
                  pl.BlockSpec((1, LANES, 2 * LANES), lambda i, j: (j, 0, 0)), vec, vec, vec],
        out_specs=seq,
        out_shape=jax.ShapeDtypeStruct((b, s, c), BF16),
        scratch_shapes=[
            pltpu.VMEM((SUBLANES + LRU_SEGMENT * SUBLANES, LANES), F32),
            pltpu.VMEM((s, LANES), F32),
            pltpu.VMEM((2, LRU_CHUNK, LANES), F32),
            pltpu.VMEM((2, LRU_CHUNK, 2 * LANES), F32),
        ],
        compiler_params=pltpu.CompilerParams(
            dimension_semantics=("arbitrary", "arbitrary"), vmem_limit_bytes=VMEM_LIMIT_BYTES),
        name="rg_lru",
    )(xl, conv_w, row(conv_b), w_gate, row(a_b), row(x_b), row(lam))


def _gated_norm(y_ref, z_ref, nw_ref):
    y = y_ref[0].astype(F32)
    ms = jnp.mean(y * y, axis=-1, keepdims=True)
    return ((y * lax.rsqrt(ms + EPS) * nw_ref[...]) * _silu(z_ref[0].astype(F32))).astype(BF16)


def _outproj_math(ya_ref, za_ref, yl_ref, zl_ref, x_ref, mod_ref, anw_ref, lnw_ref, w_ref):
    y = jnp.dot(_gated_norm(ya_ref, za_ref, anw_ref), w_ref[0:D_ATTN, :], preferred_element_type=F32)
    y = y + jnp.dot(_gated_norm(yl_ref, zl_ref, lnw_ref), w_ref[D_ATTN:, :],
                    preferred_element_type=F32)
    return x_ref[0] + mod_ref[0, 2:3, :] * y


def _outproj_kernel(ya_ref, za_ref, yl_ref, zl_ref, x_ref, mod_ref, anw_ref, lnw_ref, wf_ref, fnw_ref,
                    o_ref, w_ref):
    _cast_weight_once(wf_ref, w_ref)
    out = _outproj_math(ya_ref, za_ref, yl_ref, zl_ref, x_ref, mod_ref, anw_ref, lnw_ref, w_ref)
    ms = jnp.mean(out * out, axis=-1, keepdims=True)
    o_ref[0] = out * lax.rsqrt(ms + EPS) * fnw_ref[...]


def _outin_kernel(ya_ref, za_ref, yl_ref, zl_ref, x_ref, mod_ref, anw_ref, lnw_ref, wof_ref,
                  modn_ref, nwn_ref, wif_ref,
                  xo_ref, q_ref, k_ref, v_ref, zao_ref, xlo_ref, zlo_ref, wo_ref, wi_ref):
    _cast_weight_once(wof_ref, wo_ref)
    _cast_weight_once(wif_ref, wi_ref)
    out = _outproj_math(ya_ref, za_ref, yl_ref, zl_ref, x_ref, mod_ref, anw_ref, lnw_ref, wo_ref)
    xo_ref[0] = out
    _inproj_math(out, modn_ref, nwn_ref, wi_ref, (q_ref, k_ref, v_ref, zao_ref, xlo_ref, zlo_ref))


def _proj_specs(d, layer, tm=ROW_TILE):
    return dict(
        half=pl.BlockSpec((1, tm, D_ATTN), lambda i, j: (i, j, 0)),
        full=pl.BlockSpec((1, tm, d), lambda i, j: (i, j, 0)),
        mod=pl.BlockSpec((1, 3, d), lambda i, j: (i, 0, 0)),
        nvec=pl.BlockSpec((1, D_ATTN), lambda i, j: (0, 0)),
        dvec=pl.BlockSpec((1, d), lambda i, j: (0, 0)),
        w_out=pl.BlockSpec((1, d, d), lambda i, j: (layer, 0, 0), pipeline_mode=pl.Buffered(1)),
        w_in=pl.BlockSpec((1, d, D_IN), lambda i, j: (layer + 1, 0, 0), pipeline_mode=pl.Buffered(1)),
    )


def _outproj_call(ya, za, yl, zl, x, mod3, anw, lnw, w_out, layer, fnw):
    b, s, d = x.shape
    sp = _proj_specs(d, layer, OUT_ROW_TILE)
    return pl.pallas_call(
        _outproj_kernel,
        grid=(b, s // OUT_ROW_TILE),
        in_specs=[sp["half"]] * 4 + [sp["full"], sp["mod"], sp["nvec"], sp["nvec"], sp["w_out"],
                                     sp["dvec"]],
        out_specs=sp["full"],
        out_shape=jax.ShapeDtypeStruct((b, s, d), F32),
        scratch_shapes=[pltpu.VMEM((d, d), BF16)],
        compiler_params=pltpu.CompilerParams(
            dimension_semantics=("arbitrary", "arbitrary"), vmem_limit_bytes=VMEM_LIMIT_BYTES),
        name="outproj",
    )(ya, za, yl, zl, x, mod3, anw.reshape(1, D_ATTN), lnw.reshape(1, D_LRU), w_out,
      fnw.reshape(1, d))


def _outin_call(ya, za, yl, zl, x, mod3, anw, lnw, w_out, layer, mod3_next, norm_w_next, w_in):
    b, s, d = x.shape
    sp = _proj_specs(d, layer)
    proj_specs, proj_shapes = _inproj_outputs(b, s)
    return pl.pallas_call(
        _outin_kernel,
        grid=(b, s // ROW_TILE),
        in_specs=[sp["half"]] * 4 + [sp["full"], sp["mod"], sp["nvec"], sp["nvec"], sp["w_out"],
                                     sp["mod"], sp["dvec"], sp["w_in"]],
        out_specs=[sp["full"]] + proj_specs,
        out_shape=[jax.ShapeDtypeStruct((b, s, d), F32)] + proj_shapes,
        scratch_shapes=[pltpu.VMEM((d, d), BF16), pltpu.VMEM((d, D_IN), BF16)],
        compiler_params=pltpu.CompilerParams(
            dimension_semantics=("arbitrary", "arbitrary"), vmem_limit_bytes=PROJ_VMEM_LIMIT_BYTES),
        name="outproj_inproj",
    )(ya, za, yl, zl, x, mod3, anw.reshape(1, D_ATTN), lnw.reshape(1, D_LRU), w_out,
      mod3_next, norm_w_next.reshape(1, d), w_in)


def _gate_weights(a_w, x_w):
    per_slab = LANES // LRU_BLOCK
    slabs = []
    for j in range(D_LRU // LANES):
        blocks = range(j * per_slab, (j + 1) * per_slab)
        diag = lambda w: jax.scipy.linalg.block_diag(*[w[g] for g in blocks])
        slabs.append(jnp.concatenate([diag(a_w), diag(x_w)], axis=1))
    return (0.5 * jnp.stack(slabs)).astype(BF16)


def kernel(x, c, norm_w, ada_w, ada_b, w_in, conv_w, conv_b, rg_a_w, rg_a_b, rg_x_w, rg_x_b,
           rg_lambda, attn_out_norm_w, lru_out_norm_w, w_out, final_norm_w):
    n_layers = ada_w.shape[0]
    b = x.shape[0]
    mod = _mod_call(c, ada_w, ada_b)
    slopes = jnp.exp2(-8.0 * (jnp.arange(N_HEADS, dtype=F32) + 1.0) / N_HEADS)
    kext = _key_side_table(slopes, x.shape[1])
    mod3 = [mod[l].reshape(b, 3, D_MODEL) for l in range(n_layers)]
    q, k, v, za, xl, zl = _inproj_call(x, mod3[0], norm_w[0], w_in, 0)
    for l in range(n_layers):
        ya = _attn_call(slopes, kext, q, k, v)
        yl = _lru_call(xl, conv_w[l], conv_b[l], _gate_weights(rg_a_w[l], rg_x_w[l]), rg_a_b[l],
                       rg_x_b[l], rg_lambda[l])
        if l + 1 < n_layers:
            x, q, k, v, za, xl, zl = _outin_call(
                ya, za, yl, zl, x, mod3[l], attn_out_norm_w[l], lru_out_norm_w[l], w_out, l,
                mod3[l + 1], norm_w[l + 1], w_in)
        else:
            x = _outproj_call(ya, za, yl, zl, x, mod3[l], attn_out_norm_w[l], lru_out_norm_w[l],
                              w_out, l, final_norm_w)
    return x
```

```python
import math

import jax
import jax.numpy as jnp
from jax import lax
from jax.experimental import pallas as pl
from jax.experimental.pallas import tpu as pltpu

D_MODEL = 1024
D_ATTN = 512
D_LRU = 512
HEAD_DIM = 64
N_HEADS = D_ATTN // HEAD_DIM
N_LRU_BLOCKS = 8
LRU_BLOCK = D_LRU // N_LRU_BLOCKS
CONV_WIDTH = 4
RG_C = 8.0
MOBA_BLOCK = 256
MOBA_TOPK = 3
D_IN = 4 * D_ATTN + 2 * D_LRU
EPS = 1e-6

F32 = jnp.float32
BF16 = jnp.bfloat16
NEG_BIG = -1e30
LOG2E = math.log2(math.e)

LANES = 128
SUBLANES = 8
VMEM_LIMIT_BYTES = 48 * 1024 * 1024
PROJ_VMEM_LIMIT_BYTES = 56 * 1024 * 1024

ROW_TILE = 512
OUT_ROW_TILE = 1024
WEIGHT_CAST_ROWS = 64
LRU_CHUNK = 256
LRU_SEGMENT = 4
HEADS_PER_PAIR = LANES // HEAD_DIM
FEATURE_MAJOR_OUTPUTS = (0, 2)
ATTN_PAIRS_PER_STEP = 2
PREPARE_AT = 6


def _sigmoid(v):
    return 0.5 * jnp.tanh(0.5 * v) + 0.5


def _silu(v):
    return v * _sigmoid(v)


def _split_bf16(v):
    hi = v.astype(BF16).astype(F32)
    lo = (v - hi).astype(BF16).astype(F32)
    return hi, lo


def _mod_kernel(c_ref, w_ref, b_ref, o_ref):
    s = _silu(c_ref[...]).astype(BF16)
    w = w_ref[0].astype(BF16)
    o_ref[0] = jnp.dot(s, w, preferred_element_type=F32) + b_ref[0]


def _mod_call(c, ada_w, ada_b):
    n_layers, d, d3 = ada_w.shape
    b = c.shape[0]
    tn = 1024
    return pl.pallas_call(
        _mod_kernel,
        grid=(n_layers, d3 // tn),
        in_specs=[
            pl.BlockSpec((b, d), lambda l, j: (0, 0)),
            pl.BlockSpec((1, d, tn), lambda l, j: (l, 0, j)),
            pl.BlockSpec((1, 1, tn), lambda l, j: (l, 0, j)),
        ],
        out_specs=pl.BlockSpec((1, b, tn), lambda l, j: (l, 0, j)),
        out_shape=jax.ShapeDtypeStruct((n_layers, b, d3), F32),
        compiler_params=pltpu.CompilerParams(
            dimension_semantics=("arbitrary", "arbitrary"), vmem_limit_bytes=VMEM_LIMIT_BYTES),
        name="adaln_mod",
    )(c, ada_w, ada_b.reshape(n_layers, 1, d3))


def _cast_weight_once(w_ref, wbf_scr):
    rows = w_ref.shape[1]

    @pl.when((pl.program_id(0) == 0) & (pl.program_id(1) == 0))
    def _():
        def body(i, carry):
            r0 = pl.multiple_of(i * WEIGHT_CAST_ROWS, WEIGHT_CAST_ROWS)
            wbf_scr[pl.ds(r0, WEIGHT_CAST_ROWS), :] = w_ref[0, pl.ds(r0, WEIGHT_CAST_ROWS), :].astype(BF16)
            return carry
        lax.fori_loop(0, rows // WEIGHT_CAST_ROWS, body, 0)


def _inproj_math(x, mod_ref, nw_ref, w_ref, out_refs):
    ms = jnp.mean(x * x, axis=-1, keepdims=True)
    y = x * lax.rsqrt(ms + EPS) * nw_ref[...]
    shift = mod_ref[0, 0:1, :]
    scale = mod_ref[0, 1:2, :]
    h = (y * (1.0 + scale) + shift).astype(BF16)
    for j, ref in enumerate(out_refs):
        r = jnp.dot(h, w_ref[:, j * D_ATTN:(j + 1) * D_ATTN], preferred_element_type=F32)
        if j == 0:
            r = r * (HEAD_DIM ** -0.5 * LOG2E)
        if j in FEATURE_MAJOR_OUTPUTS:
            r = r.T
        ref[0] = r.astype(ref.dtype)


def _inproj_kernel(x_ref, mod_ref, nw_ref, wf_ref, q_ref, k_ref, v_ref, za_ref, xl_ref, zl_ref,
                   w_ref):
    _cast_weight_once(wf_ref, w_ref)
    _inproj_math(x_ref[0], mod_ref, nw_ref, w_ref, (q_ref, k_ref, v_ref, za_ref, xl_ref, zl_ref))


def _inproj_outputs(b, s):
    tm = ROW_TILE
    token_major = pl.BlockSpec((1, tm, D_ATTN), lambda i, j: (i, j, 0))
    feature_major = pl.BlockSpec((1, D_ATTN, tm), lambda i, j: (i, 0, j))
    dtypes = (BF16, BF16, BF16, BF16, F32, BF16)
    specs, shapes = [], []
    for j, dt in enumerate(dtypes):
        fm = j in FEATURE_MAJOR_OUTPUTS
        specs.append(feature_major if fm else token_major)
        shapes.append(jax.ShapeDtypeStruct((b, D_ATTN, s) if fm else (b, s, D_ATTN), dt))
    return specs, shapes


def _inproj_call(x, mod3, norm_w, w_in, layer):
    b, s, d = x.shape
    tm = ROW_TILE
    out_specs, out_shape = _inproj_outputs(b, s)
    return pl.pallas_call(
        _inproj_kernel,
        grid=(b, s // tm),
        in_specs=[
            pl.BlockSpec((1, tm, d), lambda i, j: (i, j, 0)),
            pl.BlockSpec((1, 3, d), lambda i, j: (i, 0, 0)),
            pl.BlockSpec((1, d), lambda i, j: (0, 0)),
            pl.BlockSpec((1, d, D_IN), lambda i, j: (layer, 0, 0), pipeline_mode=pl.Buffered(1)),
        ],
        out_specs=out_specs,
        out_shape=out_shape,
        scratch_shapes=[pltpu.VMEM((d, D_IN), BF16)],
        compiler_params=pltpu.CompilerParams(
            dimension_semantics=("arbitrary", "arbitrary"), vmem_limit_bytes=VMEM_LIMIT_BYTES),
        name="inproj",
    )(x, mod3, norm_w.reshape(1, d), w_in)


def _mixer_kernel(slopes_ref, q_ref, k_ref, v_ref, kext_ref, *rest):
    n_slabs = k_ref.shape[2] // LANES
    xl_refs = rest[:n_slabs]
    (cw_ref, cb_ref, wg_ref, ab_ref, xb_ref, lam_ref, o_ref, ol_ref,
     kaug_scr, qaug_scr, vaug_scr, s_scr, p_scr, out_scr,
     head_scr, h_scr, xc_scr, g_scr) = rest[n_slabs:]
    s_len = k_ref.shape[1]
    n_blk = s_len // MOBA_BLOCK
    n_pairs = k_ref.shape[2] // LANES
    n_heads = n_pairs * HEADS_PER_PAIR
    first_head = pl.program_id(0) * n_heads

    prow = lax.broadcasted_iota(jnp.int32, (n_blk, s_len), 0)
    pcol = lax.broadcasted_iota(jnp.int32, (n_blk, s_len), 1)
    block_mean = jnp.where(pcol // MOBA_BLOCK == prow, 1.0 / MOBA_BLOCK, 0.0).astype(BF16)
    klane = lax.broadcasted_iota(jnp.int32, (s_len, LANES), 1)
    mlane = lax.broadcasted_iota(jnp.int32, (n_blk, LANES), 1)
    n_iota = prow
    q_blk = pcol // MOBA_BLOCK
    key_off = lax.broadcasted_iota(jnp.int32, (MOBA_BLOCK, MOBA_BLOCK), 0)
    qry_off = lax.broadcasted_iota(jnp.int32, (MOBA_BLOCK, MOBA_BLOCK), 1)
    causal = key_off <= qry_off

    def prepare(pp):
        lanes = slice(pp * LANES, (pp + 1) * LANES)
        kp = k_ref[0, :, lanes]
        q_t = q_ref[0, lanes, :]
        v_t = v_ref[0, lanes, :]
        kmean = jnp.dot(block_mean, kp, preferred_element_type=F32)
        terms = []
        for hd in range(HEADS_PER_PAIR):
            rest = jnp.where(mlane // HEAD_DIM == hd, kmean, 0.0)
            for _ in range(3):
                part = rest.astype(BF16).astype(F32)
                terms.append(part)
                rest = rest - part
        terms = jnp.concatenate(terms, axis=0).astype(BF16)
        half = s_len // 2
        gates = jnp.concatenate(
            [jnp.dot(terms, q_t[:, :half], preferred_element_type=F32),
             jnp.dot(terms, q_t[:, half:], preferred_element_type=F32)], axis=1)
        for hd in range(HEADS_PER_PAIR):
            h = pp * HEADS_PER_PAIR + hd
            kaug_scr[h] = jnp.where(klane // HEAD_DIM == hd, kp, kext_ref[h])
            slope2 = slopes_ref[first_head + h] * LOG2E
            g_hi, g_mid, g_lo = (gates[(3 * hd + t) * n_blk:(3 * hd + t + 1) * n_blk]
                                 for t in range(3))
            g_t = g_hi + g_mid + g_lo
            rank = jnp.zeros((n_blk, s_len), jnp.int32)
            for m in range(n_blk):
                gm = g_t[m:m + 1, :]
                beats = (gm > g_t) | ((gm == g_t) & (m < n_iota))
                rank = rank + jnp.where(beats & (m < q_blk), 1, 0)
            sel = ((n_iota < q_blk) & (rank < MOBA_TOPK)) | (n_iota == q_blk)
            bias_hi, bias_lo = _split_bf16(slope2 * ((n_iota - q_blk) * MOBA_BLOCK).astype(F32))
            selb_hi = jnp.where(sel, bias_hi, NEG_BIG)
            selb_lo = jnp.where(sel, bias_lo, 0.0)
            flag = jnp.where(n_iota < 2, 1.0, 0.0)
            ext = jnp.concatenate(
                [selb_hi, selb_lo, flag, jnp.zeros((HEAD_DIM - 3 * n_blk, s_len), F32)], axis=0)
            own = slice(hd * HEAD_DIM, (hd + 1) * HEAD_DIM)
            other = slice((1 - hd) * HEAD_DIM, (2 - hd) * HEAD_DIM)
            qaug_scr[h, own, :] = q_t[own]
            qaug_scr[h, other, :] = ext.astype(BF16)
            vaug_scr[h, 0:HEAD_DIM, :] = v_t[own]
            vaug_scr[h, HEAD_DIM:, :] = jnp.ones((2 * SUBLANES, s_len), BF16)

    items = [(h, qb) for h in range(n_heads)
             for qb in (range(n_blk) if h % 2 == 0 else reversed(range(n_blk)))]
    per_pair = HEADS_PER_PAIR * n_blk

    def scores(i):
        h, qb = items[i]
        slot = i % 2
        q0, kk = qb * MOBA_BLOCK, (qb + 1) * MOBA_BLOCK
        qa = qaug_scr[h, :, q0:kk]
        m = None
        for r0 in range(0, kk, MOBA_BLOCK):
            s = jnp.dot(kaug_scr[h, r0:r0 + MOBA_BLOCK, :], qa, preferred_element_type=F32)
            if r0 == q0:
                s = jnp.where(causal, s, NEG_BIG)
            s_scr[slot, r0:r0 + MOBA_BLOCK, :] = s
            mh = jnp.max(s, axis=0, keepdims=True)
            m = mh if m is None else jnp.maximum(m, mh)
        return m

    def probs(i, m):
        _, qb = items[i]
        slot = i % 2
        kk = (qb + 1) * MOBA_BLOCK
        p_scr[slot, 0:kk, :] = jnp.exp2(s_scr[slot, 0:kk, :] - m).astype(BF16)

    def weighted(i):
        h, qb = items[i]
        slot = i % 2
        q0, kk = qb * MOBA_BLOCK, (qb + 1) * MOBA_BLOCK
        o = None
        for r0, r1 in ((0, kk // 2), (kk // 2, kk)):
            part = jnp.dot(vaug_scr[h, :, r0:r1], p_scr[slot, r0:r1, :], preferred_element_type=F32)
            o = part if o is None else o + part
        out_scr[h * HEAD_DIM:(h + 1) * HEAD_DIM, q0:kk] = o[0:HEAD_DIM] / o[HEAD_DIM:HEAD_DIM + 1]

    lru_units = []
    for sl in range(n_slabs):
        lanes = slice(sl * LANES, (sl + 1) * LANES)

        def store_h(value, lanes=lanes):
            ol_ref[0, :, lanes] = value

        lru_units += _lru_units(
            xl_refs[sl].at[0], cw_ref[:, lanes], cb_ref[:, lanes], wg_ref.at[sl], ab_ref[:, lanes],
            xb_ref[:, lanes], lam_ref[:, lanes], head_scr.at[sl], h_scr.at[sl], xc_scr.at[sl],
            g_scr.at[sl], store_h)
    lru_done = 0

    n_items = len(items)
    prepare(0)
    col_max = {0: scores(0), 1: scores(1)}
    probs(0, col_max.pop(0))
    for i in range(n_items):
        pp, within = divmod(i, per_pair)
        if within == PREPARE_AT and pp + 1 < n_pairs:
            prepare(pp + 1)
        if i + 2 < n_items:
            col_max[i + 2] = scores(i + 2)
        if i + 1 < n_items:
            probs(i + 1, col_max.pop(i + 1))
        weighted(i)
        if within == per_pair - 1:
            lanes = slice(pp * LANES, (pp + 1) * LANES)
            o_ref[0, :, lanes] = out_scr[lanes, :].T.astype(o_ref.dtype)
        lru_target = (i + 1) * len(lru_units) // n_items
        while lru_done < lru_target:
            lru_units[lru_done]()
            lru_done += 1


def _key_side_table(slopes, s_len):
    n_blk = s_len // MOBA_BLOCK
    shape = (N_HEADS, s_len, LANES)
    head = lax.broadcasted_iota(jnp.int32, shape, 0)
    pos = lax.broadcasted_iota(jnp.int32, shape, 1)
    lane = lax.broadcasted_iota(jnp.int32, shape, 2)
    sub = lane % HEAD_DIM
    foreign = lane // HEAD_DIM != head % HEADS_PER_PAIR
    onehot = jnp.where(sub % n_blk == pos // MOBA_BLOCK, 1.0, 0.0)
    off = (slopes * LOG2E)[:, None, None] * (pos % MOBA_BLOCK).astype(F32)
    off_hi = lax.bitcast_convert_type(
        lax.bitcast_convert_type(off, jnp.uint32) & jnp.uint32(0xFFFF0000), F32)
    ext = jnp.where(sub < 2 * n_blk, onehot,
                    jnp.where(sub == 2 * n_blk, off_hi,
                              jnp.where(sub == 2 * n_blk + 1, off - off_hi, 0.0)))
    return jnp.where(foreign, ext, 0.0).astype(BF16)


def _mixer_call(slopes, kext, q_t, k, v_t, xl, conv_w, conv_b, w_gate, a_b, x_b, lam):
    b, s, _ = k.shape
    pairs = ATTN_PAIRS_PER_STEP
    width = pairs * LANES
    heads = pairs * HEADS_PER_PAIR
    row = lambda a: a.reshape(1, D_LRU)
    spec = pl.BlockSpec((1, s, width), lambda j, i, sl: (i, 0, j))
    spec_t = pl.BlockSpec((1, width, s), lambda j, i, sl: (i, j, 0))
    vec = pl.BlockSpec((1, width), lambda j, i, sl: (0, j))
    slab_specs = [pl.BlockSpec((1, s, LANES), lambda j, i, sl, n=n: (i, 0, j * pairs + n))
                  for n in range(pairs)]
    sub_len = LRU_SEGMENT * SUBLANES
    return pl.pallas_call(
        _mixer_kernel,
        grid_spec=pltpu.PrefetchScalarGridSpec(
            num_scalar_prefetch=1,
            grid=(D_ATTN // width, b),
            in_specs=[spec_t, spec, spec_t,
                      pl.BlockSpec((heads, s, LANES), lambda j, i, sl: (j, 0, 0),
                                   pipeline_mode=pl.Buffered(1))]
            + slab_specs
            + [pl.BlockSpec((CONV_WIDTH, width), lambda j, i, sl: (0, j)), vec,
               pl.BlockSpec((pairs, LANES, 2 * LANES), lambda j, i, sl: (j, 0, 0)), vec, vec, vec],
            out_specs=[spec, spec],
            scratch_shapes=[
                pltpu.VMEM((heads, s, LANES), BF16),
                pltpu.VMEM((heads, LANES, s), BF16),
                pltpu.VMEM((heads, HEAD_DIM + 2 * SUBLANES, s), BF16),
                pltpu.VMEM((2, s, MOBA_BLOCK), F32),
                pltpu.VMEM((2, s, MOBA_BLOCK), BF16),
                pltpu.VMEM((width, s), F32),
                pltpu.VMEM((pairs, SUBLANES + sub_len, LANES), F32),
                pltpu.VMEM((pairs, s, LANES), F32),
                pltpu.VMEM((pairs, 2, LRU_CHUNK, LANES), F32),
                pltpu.VMEM((pairs, 2, LRU_CHUNK, 2 * LANES), F32),
            ],
        ),
        out_shape=[jax.ShapeDtypeStruct((b, s, D_ATTN), BF16), jax.ShapeDtypeStruct((b, s, D_LRU), BF16)],
        compiler_params=pltpu.CompilerParams(
            dimension_semantics=("arbitrary", "arbitrary"), vmem_limit_bytes=VMEM_LIMIT_BYTES),
        name="mixer",
    )(slopes, q_t, k, v_t, kext, *([xl] * pairs), conv_w, row(conv_b), w_gate, row(a_b), row(x_b),
      row(lam))


def _lru_units(x_seq, cw, cb, w_gate, a_b, x_b, lam, head_scr, h_scr, xc_scr, g_scr, store_h):
    s_len = x_seq.shape[0]
    seg = LRU_SEGMENT
    sub_len = seg * SUBLANES
    n_sub = LRU_CHUNK // sub_len
    n_chunks = s_len // LRU_CHUNK

    neg_lam = -lam
    softplus = jnp.maximum(neg_lam, 0.0) + jnp.log1p(jnp.exp(-jnp.abs(neg_lam)))
    half_rate = (0.5 * RG_C) * softplus
    ab_half = 0.5 * a_b
    xb_half = 0.5 * x_b
    sub = lax.broadcasted_iota(jnp.int32, (SUBLANES, LANES), 0)
    steps = (1, 2, 4)
    in_vreg = [sub >= step for step in steps]
    state = {"h": jnp.zeros((SUBLANES, LANES), F32)}

    def strided(ref, start):
        return ref[pl.ds(start, SUBLANES, stride=seg), :]

    def init():
        head_scr[0:SUBLANES, :] = jnp.zeros((SUBLANES, LANES), F32)
        head_scr[SUBLANES:, :] = x_seq[0:sub_len, :]

    def conv_and_gates(ci):
        xc_parts = []
        for c in range(n_sub):
            if ci == 0 and c == 0:
                src, t0 = head_scr, SUBLANES
            else:
                src, t0 = x_seq, ci * LRU_CHUNK
            cache = {}
            for g in range(seg):
                acc = cb
                for j in range(CONV_WIDTH):
                    off = c * sub_len + g - (CONV_WIDTH - 1) + j
                    if off not in cache:
                        cache[off] = strided(src, t0 + off)
                    acc = acc + cw[j:j + 1, :] * cache[off]
                xc_parts.append(acc)
        xc = jnp.concatenate(xc_parts, axis=0)
        xc_scr[ci % 2] = xc
        g_scr[ci % 2] = jnp.dot(xc.astype(BF16), w_gate[...], preferred_element_type=F32)

    def recurrence(ci, c):
        t0 = ci * LRU_CHUNK
        rows = slice(c * sub_len, (c + 1) * sub_len)
        h_in = state["h"]
        xc = xc_scr[ci % 2, rows, :]
        t_r = jnp.tanh(g_scr[ci % 2, rows, :LANES] + ab_half)
        t_i = jnp.tanh(g_scr[ci % 2, rows, LANES:] + xb_half)
        neg_log_a = half_rate * t_r + half_rate
        a = jnp.exp(-neg_log_a)
        y = jnp.tanh(neg_log_a) * (a * a + 1.0)
        root = jnp.where(y == 0.0, 0.0, y * lax.rsqrt(y))
        bterm = root * ((t_i + 1.0) * (0.5 * xc))

        comp = []
        for g in range(seg):
            av, bv = a[g * SUBLANES:(g + 1) * SUBLANES], bterm[g * SUBLANES:(g + 1) * SUBLANES]
            if g > 0:
                bv = av * comp[-1][1] + bv
                av = av * comp[-1][0]
            comp.append((av, bv))
        pa, pb = comp[-1]
        for step, valid in zip(steps, in_vreg):
            a_sh = jnp.where(valid, pltpu.roll(pa, step, axis=0), 1.0)
            b_sh = jnp.where(valid, pltpu.roll(pb, step, axis=0), 0.0)
            pb = pa * b_sh + pb
            pa = pa * a_sh
        ea = jnp.where(in_vreg[0], pltpu.roll(pa, 1, axis=0), 1.0)
        eb = jnp.where(in_vreg[0], pltpu.roll(pb, 1, axis=0), 0.0)
        seg_in = ea * h_in + eb
        for g in range(seg):
            h_g = comp[g][0] * seg_in + comp[g][1]
            h_scr[pl.ds(t0 + c * sub_len + g, SUBLANES, stride=seg), :] = h_g
        h_last = pa * h_in + pb
        state["h"] = jnp.broadcast_to(h_last[SUBLANES - 1:SUBLANES, :], (SUBLANES, LANES))

    def finish():
        store_h(h_scr[...].astype(BF16))

    units = [init, lambda: conv_and_gates(0)]
    for ci in range(n_chunks):
        for c in range(n_sub):
            units.append(lambda ci=ci, c=c: recurrence(ci, c))
            if c == n_sub // 2 - 1 and ci + 1 < n_chunks:
                units.append(lambda ci=ci: conv_and_gates(ci + 1))
    units.append(finish)
    return units


def _gated_norm(y_ref, z_ref, nw_ref):
    y = y_ref[0].astype(F32)
    ms = jnp.mean(y * y, axis=-1, keepdims=True)
    return ((y * lax.rsqrt(ms + EPS) * nw_ref[...]) * _silu(z_ref[0].astype(F32))).astype(BF16)


def _outproj_math(ya_ref, za_ref, yl_ref, zl_ref, x_ref, mod_ref, anw_ref, lnw_ref, w_ref):
    y = jnp.dot(_gated_norm(ya_ref, za_ref, anw_ref), w_ref[0:D_ATTN, :], preferred_element_type=F32)
    y = y + jnp.dot(_gated_norm(yl_ref, zl_ref, lnw_ref), w_ref[D_ATTN:, :],
                    preferred_element_type=F32)
    return x_ref[0] + mod_ref[0, 2:3, :] * y


def _outproj_kernel(ya_ref, za_ref, yl_ref, zl_ref, x_ref, mod_ref, anw_ref, lnw_ref, wf_ref, fnw_ref,
                    o_ref, w_ref):
    _cast_weight_once(wf_ref, w_ref)
    out = _outproj_math(ya_ref, za_ref, yl_ref, zl_ref, x_ref, mod_ref, anw_ref, lnw_ref, w_ref)
    ms = jnp.mean(out * out, axis=-1, keepdims=True)
    o_ref[0] = out * lax.rsqrt(ms + EPS) * fnw_ref[...]


def _outin_kernel(ya_ref, za_ref, yl_ref, zl_ref, x_ref, mod_ref, anw_ref, lnw_ref, wof_ref,
                  modn_ref, nwn_ref, wif_ref,
                  xo_ref, q_ref, k_ref, v_ref, zao_ref, xlo_ref, zlo_ref, wo_ref, wi_ref):
    _cast_weight_once(wof_ref, wo_ref)
    _cast_weight_once(wif_ref, wi_ref)
    out = _outproj_math(ya_ref, za_ref, yl_ref, zl_ref, x_ref, mod_ref, anw_ref, lnw_ref, wo_ref)
    xo_ref[0] = out
    _inproj_math(out, modn_ref, nwn_ref, wi_ref, (q_ref, k_ref, v_ref, zao_ref, xlo_ref, zlo_ref))


def _proj_specs(d, layer, tm=ROW_TILE):
    return dict(
        half=pl.BlockSpec((1, tm, D_ATTN), lambda i, j: (i, j, 0)),
        full=pl.BlockSpec((1, tm, d), lambda i, j: (i, j, 0)),
        mod=pl.BlockSpec((1, 3, d), lambda i, j: (i, 0, 0)),
        nvec=pl.BlockSpec((1, D_ATTN), lambda i, j: (0, 0)),
        dvec=pl.BlockSpec((1, d), lambda i, j: (0, 0)),
        w_out=pl.BlockSpec((1, d, d), lambda i, j: (layer, 0, 0), pipeline_mode=pl.Buffered(1)),
        w_in=pl.BlockSpec((1, d, D_IN), lambda i, j: (layer + 1, 0, 0), pipeline_mode=pl.Buffered(1)),
    )


def _outproj_call(ya, za, yl, zl, x, mod3, anw, lnw, w_out, layer, fnw):
    b, s, d = x.shape
    sp = _proj_specs(d, layer, OUT_ROW_TILE)
    return pl.pallas_call(
        _outproj_kernel,
        grid=(b, s // OUT_ROW_TILE),
        in_specs=[sp["half"]] * 4 + [sp["full"], sp["mod"], sp["nvec"], sp["nvec"], sp["w_out"],
                                     sp["dvec"]],
        out_specs=sp["full"],
        out_shape=jax.ShapeDtypeStruct((b, s, d), F32),
        scratch_shapes=[pltpu.VMEM((d, d), BF16)],
        compiler_params=pltpu.CompilerParams(
            dimension_semantics=("arbitrary", "arbitrary"), vmem_limit_bytes=VMEM_LIMIT_BYTES),
        name="outproj",
    )(ya, za, yl, zl, x, mod3, anw.reshape(1, D_ATTN), lnw.reshape(1, D_LRU), w_out,
      fnw.reshape(1, d))


def _outin_call(ya, za, yl, zl, x, mod3, anw, lnw, w_out, layer, mod3_next, norm_w_next, w_in):
    b, s, d = x.shape
    sp = _proj_specs(d, layer)
    proj_specs, proj_shapes = _inproj_outputs(b, s)
    return pl.pallas_call(
        _outin_kernel,
        grid=(b, s // ROW_TILE),
        in_specs=[sp["half"]] * 4 + [sp["full"], sp["mod"], sp["nvec"], sp["nvec"], sp["w_out"],
                                     sp["mod"], sp["dvec"], sp["w_in"]],
        out_specs=[sp["full"]] + proj_specs,
        out_shape=[jax.ShapeDtypeStruct((b, s, d), F32)] + proj_shapes,
        scratch_shapes=[pltpu.VMEM((d, d), BF16), pltpu.VMEM((d, D_IN), BF16)],
        compiler_params=pltpu.CompilerParams(
            dimension_semantics=("arbitrary", "arbitrary"), vmem_limit_bytes=PROJ_VMEM_LIMIT_BYTES),
        name="outproj_inproj",
    )(ya, za, yl, zl, x, mod3, anw.reshape(1, D_ATTN), lnw.reshape(1, D_LRU), w_out,
      mod3_next, norm_w_next.reshape(1, d), w_in)


def _gate_weights(a_w, x_w):
    per_slab = LANES // LRU_BLOCK
    slabs = []
    for j in range(D_LRU // LANES):
        blocks = range(j * per_slab, (j + 1) * per_slab)
        diag = lambda w: jax.scipy.linalg.block_diag(*[w[g] for g in blocks])
        slabs.append(jnp.concatenate([diag(a_w), diag(x_w)], axis=1))
    return (0.5 * jnp.stack(slabs)).astype(BF16)


def kernel(x, c, norm_w, ada_w, ada_b, w_in, conv_w, conv_b, rg_a_w, rg_a_b, rg_x_w, rg_x_b,
           rg_lambda, attn_out_norm_w, lru_out_norm_w, w_out, final_norm_w):
    n_layers = ada_w.shape[0]
    b = x.shape[0]
    mod = _mod_call(c, ada_w, ada_b)
    slopes = jnp.exp2(-8.0 * (jnp.arange(N_HEADS, dtype=F32) + 1.0) / N_HEADS)
    kext = _key_side_table(slopes, x.shape[1])
    mod3 = [mod[l].reshape(b, 3, D_MODEL) for l in range(n_layers)]
    q, k, v, za, xl, zl = _inproj_call(x, mod3[0], norm_w[0], w_in, 0)
    for l in range(n_layers):
        ya, yl = _mixer_call(slopes, kext, q, k, v, xl, conv_w[l], conv_b[l],
                             _gate_weights(rg_a_w[l], rg_x_w[l]), rg_a_b[l], rg_x_b[l], rg_lambda[l])
        if l + 1 < n_layers:
            x, q, k, v, za, xl, zl = _outin_call(
                ya, za, yl, zl, x, mod3[l], attn_out_norm_w[l], lru_out_norm_w[l], w_out, l,
                mod3[l + 1], norm_w[l + 1], w_in)
        else:
            x = _outproj_call(ya, za, yl, zl, x, mod3[l], attn_out_norm_w[l], lru_out_norm_w[l],
                              w_out, l, final_norm_w)
    return x
```

```python
import math

import jax
import jax.numpy as jnp
from jax import lax
from jax.experimental import pallas as pl
from jax.experimental.pallas import tpu as pltpu

D_MODEL = 1024
D_ATTN = 512
D_LRU = 512
HEAD_DIM = 64
N_HEADS = D_ATTN // HEAD_DIM
N_LRU_BLOCKS = 8
LRU_BLOCK = D_LRU // N_LRU_BLOCKS
CONV_WIDTH = 4
RG_C = 8.0
MOBA_BLOCK = 256
MOBA_TOPK = 3
D_IN = 4 * D_ATTN + 2 * D_LRU
EPS = 1e-6

F32 = jnp.float32
BF16 = jnp.bfloat16
NEG_BIG = -1e30
LOG2E = math.log2(math.e)

LANES = 128
SUBLANES = 8
VMEM_LIMIT_BYTES = 48 * 1024 * 1024
PROJ_VMEM_LIMIT_BYTES = 56 * 1024 * 1024

ROW_TILE = 512
IN_ROW_TILE = 1024
OUT_ROW_TILE = 1024
WEIGHT_CAST_ROWS = 64
LRU_CHUNK = 256
LRU_SEGMENT = 4
HEADS_PER_PAIR = LANES // HEAD_DIM
FEATURE_MAJOR_OUTPUTS = (0, 2)
ATTN_PAIRS_PER_STEP = 2
PREPARE_AT = 6


def _sigmoid(v):
    return 0.5 * jnp.tanh(0.5 * v) + 0.5


def _silu(v):
    return v * _sigmoid(v)


def _split_bf16(v):
    hi = v.astype(BF16).astype(F32)
    lo = (v - hi).astype(BF16).astype(F32)
    return hi, lo


def _mod_kernel(c_ref, w_ref, b_ref, o_ref):
    s = _silu(c_ref[...]).astype(BF16)
    w = w_ref[0].astype(BF16)
    o_ref[0] = jnp.dot(s, w, preferred_element_type=F32) + b_ref[0]


def _mod_call(c, ada_w, ada_b):
    n_layers, d, d3 = ada_w.shape
    b = c.shape[0]
    tn = 1024
    return pl.pallas_call(
        _mod_kernel,
        grid=(n_layers, d3 // tn),
        in_specs=[
            pl.BlockSpec((b, d), lambda l, j: (0, 0)),
            pl.BlockSpec((1, d, tn), lambda l, j: (l, 0, j)),
            pl.BlockSpec((1, 1, tn), lambda l, j: (l, 0, j)),
        ],
        out_specs=pl.BlockSpec((1, b, tn), lambda l, j: (l, 0, j)),
        out_shape=jax.ShapeDtypeStruct((n_layers, b, d3), F32),
        compiler_params=pltpu.CompilerParams(
            dimension_semantics=("arbitrary", "arbitrary"), vmem_limit_bytes=VMEM_LIMIT_BYTES),
        name="adaln_mod",
    )(c, ada_w, ada_b.reshape(n_layers, 1, d3))


def _cast_weight_once(w_ref, wbf_scr):
    rows = w_ref.shape[1]

    @pl.when((pl.program_id(0) == 0) & (pl.program_id(1) == 0))
    def _():
        def body(i, carry):
            r0 = pl.multiple_of(i * WEIGHT_CAST_ROWS, WEIGHT_CAST_ROWS)
            wbf_scr[pl.ds(r0, WEIGHT_CAST_ROWS), :] = w_ref[0, pl.ds(r0, WEIGHT_CAST_ROWS), :].astype(BF16)
            return carry
        lax.fori_loop(0, rows // WEIGHT_CAST_ROWS, body, 0)


def _inproj_math(x, mod_ref, nw_ref, w_ref, out_refs):
    ms = jnp.mean(x * x, axis=-1, keepdims=True)
    y = x * lax.rsqrt(ms + EPS) * nw_ref[...]
    shift = mod_ref[0, 0:1, :]
    scale = mod_ref[0, 1:2, :]
    h = (y * (1.0 + scale) + shift).astype(BF16)
    for j, ref in enumerate(out_refs):
        r = jnp.dot(h, w_ref[:, j * D_ATTN:(j + 1) * D_ATTN], preferred_element_type=F32)
        if j == 0:
            r = r * (HEAD_DIM ** -0.5 * LOG2E)
        if j in FEATURE_MAJOR_OUTPUTS:
            r = r.T
        ref[0] = r.astype(ref.dtype)


def _inproj_kernel(x_ref, mod_ref, nw_ref, wf_ref, q_ref, k_ref, v_ref, za_ref, xl_ref, zl_ref,
                   w_ref):
    _cast_weight_once(wf_ref, w_ref)
    _inproj_math(x_ref[0], mod_ref, nw_ref, w_ref, (q_ref, k_ref, v_ref, za_ref, xl_ref, zl_ref))


def _inproj_outputs(b, s, tm=ROW_TILE):
    token_major = pl.BlockSpec((1, tm, D_ATTN), lambda i, j: (i, j, 0))
    feature_major = pl.BlockSpec((1, D_ATTN, tm), lambda i, j: (i, 0, j))
    dtypes = (BF16, BF16, BF16, BF16, F32, BF16)
    specs, shapes = [], []
    for j, dt in enumerate(dtypes):
        fm = j in FEATURE_MAJOR_OUTPUTS
        specs.append(feature_major if fm else token_major)
        shapes.append(jax.ShapeDtypeStruct((b, D_ATTN, s) if fm else (b, s, D_ATTN), dt))
    return specs, shapes


def _inproj_call(x, mod3, norm_w, w_in, layer):
    b, s, d = x.shape
    tm = IN_ROW_TILE
    out_specs, out_shape = _inproj_outputs(b, s, tm)
    return pl.pallas_call(
        _inproj_kernel,
        grid=(b, s // tm),
        in_specs=[
            pl.BlockSpec((1, tm, d), lambda i, j: (i, j, 0)),
            pl.BlockSpec((1, 3, d), lambda i, j: (i, 0, 0)),
            pl.BlockSpec((1, d), lambda i, j: (0, 0)),
            pl.BlockSpec((1, d, D_IN), lambda i, j: (layer, 0, 0), pipeline_mode=pl.Buffered(1)),
        ],
        out_specs=out_specs,
        out_shape=out_shape,
        scratch_shapes=[pltpu.VMEM((d, D_IN), BF16)],
        compiler_params=pltpu.CompilerParams(
            dimension_semantics=("arbitrary", "arbitrary"), vmem_limit_bytes=VMEM_LIMIT_BYTES),
        name="inproj",
    )(x, mod3, norm_w.reshape(1, d), w_in)


def _mixer_kernel(slopes_ref, q_ref, k_ref, v_ref, kext_ref, *rest):
    n_slabs = k_ref.shape[2] // LANES
    xl_refs = rest[:n_slabs]
    (cw_ref, cb_ref, wg_ref, ab_ref, xb_ref, lam_ref, o_ref, ol_ref,
     kaug_scr, qaug_scr, vaug_scr, s_scr, p_scr, out_scr,
     head_scr, h_scr, xc_scr, g_scr) = rest[n_slabs:]
    s_len = k_ref.shape[1]
    n_blk = s_len // MOBA_BLOCK
    n_pairs = k_ref.shape[2] // LANES
    n_heads = n_pairs * HEADS_PER_PAIR
    first_head = pl.program_id(0) * n_heads

    prow = lax.broadcasted_iota(jnp.int32, (n_blk, s_len), 0)
    pcol = lax.broadcasted_iota(jnp.int32, (n_blk, s_len), 1)
    block_mean = jnp.where(pcol // MOBA_BLOCK == prow, 1.0 / MOBA_BLOCK, 0.0).astype(BF16)
    klane = lax.broadcasted_iota(jnp.int32, (s_len, LANES), 1)
    mlane = lax.broadcasted_iota(jnp.int32, (n_blk, LANES), 1)
    n_iota = prow
    q_blk = pcol // MOBA_BLOCK
    key_off = lax.broadcasted_iota(jnp.int32, (MOBA_BLOCK, MOBA_BLOCK), 0)
    qry_off = lax.broadcasted_iota(jnp.int32, (MOBA_BLOCK, MOBA_BLOCK), 1)
    causal = key_off <= qry_off

    def prepare(pp):
        lanes = slice(pp * LANES, (pp + 1) * LANES)
        kp = k_ref[0, :, lanes]
        q_t = q_ref[0, lanes, :]
        v_t = v_ref[0, lanes, :]
        kmean = jnp.dot(block_mean, kp, preferred_element_type=F32)
        terms = []
        for hd in range(HEADS_PER_PAIR):
            rest = jnp.where(mlane // HEAD_DIM == hd, kmean, 0.0)
            for _ in range(3):
                part = rest.astype(BF16).astype(F32)
                terms.append(part)
                rest = rest - part
        terms = jnp.concatenate(terms, axis=0).astype(BF16)
        half = s_len // 2
        gates = jnp.concatenate(
            [jnp.dot(terms, q_t[:, :half], preferred_element_type=F32),
             jnp.dot(terms, q_t[:, half:], preferred_element_type=F32)], axis=1)
        for hd in range(HEADS_PER_PAIR):
            h = pp * HEADS_PER_PAIR + hd
            kaug_scr[h] = jnp.where(klane // HEAD_DIM == hd, kp, kext_ref[h])
            slope2 = slopes_ref[first_head + h] * LOG2E
            g_hi, g_mid, g_lo = (gates[(3 * hd + t) * n_blk:(3 * hd + t + 1) * n_blk]
                                 for t in range(3))
            g_t = g_hi + g_mid + g_lo
            rank = jnp.zeros((n_blk, s_len), jnp.int32)
            for m in range(n_blk):
                gm = g_t[m:m + 1, :]
                beats = (gm > g_t) | ((gm == g_t) & (m < n_iota))
                rank = rank + jnp.where(beats & (m < q_blk), 1, 0)
            sel = ((n_iota < q_blk) & (rank < MOBA_TOPK)) | (n_iota == q_blk)
            bias_hi, bias_lo = _split_bf16(slope2 * ((n_iota - q_blk) * MOBA_BLOCK).astype(F32))
            selb_hi = jnp.where(sel, bias_hi, NEG_BIG)
            selb_lo = jnp.where(sel, bias_lo, 0.0)
            flag = jnp.where(n_iota < 2, 1.0, 0.0)
            ext = jnp.concatenate(
                [selb_hi, selb_lo, flag, jnp.zeros((HEAD_DIM - 3 * n_blk, s_len), F32)], axis=0)
            own = slice(hd * HEAD_DIM, (hd + 1) * HEAD_DIM)
            other = slice((1 - hd) * HEAD_DIM, (2 - hd) * HEAD_DIM)
            qaug_scr[h, own, :] = q_t[own]
            qaug_scr[h, other, :] = ext.astype(BF16)
            vaug_scr[h, 0:HEAD_DIM, :] = v_t[own]
            vaug_scr[h, HEAD_DIM:, :] = jnp.ones((2 * SUBLANES, s_len), BF16)

    items = [(h, qb) for h in range(n_heads)
             for qb in (range(n_blk) if h % 2 == 0 else reversed(range(n_blk)))]
    per_pair = HEADS_PER_PAIR * n_blk

    def scores(i):
        h, qb = items[i]
        slot = i % 2
        q0, kk = qb * MOBA_BLOCK, (qb + 1) * MOBA_BLOCK
        qa = qaug_scr[h, :, q0:kk]
        m = None
        for r0 in range(0, kk, MOBA_BLOCK):
            s = jnp.dot(kaug_scr[h, r0:r0 + MOBA_BLOCK, :], qa, preferred_element_type=F32)
            if r0 == q0:
                s = jnp.where(causal, s, NEG_BIG)
            s_scr[slot, r0:r0 + MOBA_BLOCK, :] = s
            mh = jnp.max(s, axis=0, keepdims=True)
            m = mh if m is None else jnp.maximum(m, mh)
        return m

    def probs(i, m):
        _, qb = items[i]
        slot = i % 2
        kk = (qb + 1) * MOBA_BLOCK
        p_scr[slot, 0:kk, :] = jnp.exp2(s_scr[slot, 0:kk, :] - m).astype(BF16)

    def weighted(i):
        h, qb = items[i]
        slot = i % 2
        q0, kk = qb * MOBA_BLOCK, (qb + 1) * MOBA_BLOCK
        o = None
        for r0, r1 in ((0, kk // 2), (kk // 2, kk)):
            part = jnp.dot(vaug_scr[h, :, r0:r1], p_scr[slot, r0:r1, :], preferred_element_type=F32)
            o = part if o is None else o + part
        out_scr[h * HEAD_DIM:(h + 1) * HEAD_DIM, q0:kk] = o[0:HEAD_DIM] / o[HEAD_DIM:HEAD_DIM + 1]

    lru_units = []
    for sl in range(n_slabs):
        lanes = slice(sl * LANES, (sl + 1) * LANES)

        def store_h(value, lanes=lanes):
            ol_ref[0, :, lanes] = value

        lru_units += _lru_units(
            xl_refs[sl].at[0], cw_ref[:, lanes], cb_ref[:, lanes], wg_ref.at[sl], ab_ref[:, lanes],
            xb_ref[:, lanes], lam_ref[:, lanes], head_scr.at[sl], h_scr.at[sl], xc_scr.at[sl],
            g_scr.at[sl], store_h)
    lru_done = 0

    n_items = len(items)
    prepare(0)
    col_max = {0: scores(0), 1: scores(1)}
    probs(0, col_max.pop(0))
    for i in range(n_items):
        pp, within = divmod(i, per_pair)
        if within == PREPARE_AT and pp + 1 < n_pairs:
            prepare(pp + 1)
        if i + 2 < n_items:
            col_max[i + 2] = scores(i + 2)
        if i + 1 < n_items:
            probs(i + 1, col_max.pop(i + 1))
        weighted(i)
        if within == per_pair - 1:
            lanes = slice(pp * LANES, (pp + 1) * LANES)
            o_ref[0, :, lanes] = out_scr[lanes, :].T.astype(o_ref.dtype)
        lru_target = (i + 1) * len(lru_units) // n_items
        while lru_done < lru_target:
            lru_units[lru_done]()
            lru_done += 1


def _key_side_table(slopes, s_len):
    n_blk = s_len // MOBA_BLOCK
    shape = (N_HEADS, s_len, LANES)
    head = lax.broadcasted_iota(jnp.int32, shape, 0)
    pos = lax.broadcasted_iota(jnp.int32, shape, 1)
    lane = lax.broadcasted_iota(jnp.int32, shape, 2)
    sub = lane % HEAD_DIM
    foreign = lane // HEAD_DIM != head % HEADS_PER_PAIR
    onehot = jnp.where(sub % n_blk == pos // MOBA_BLOCK, 1.0, 0.0)
    off = (slopes * LOG2E)[:, None, None] * (pos % MOBA_BLOCK).astype(F32)
    off_hi = lax.bitcast_convert_type(
        lax.bitcast_convert_type(off, jnp.uint32) & jnp.uint32(0xFFFF0000), F32)
    ext = jnp.where(sub < 2 * n_blk, onehot,
                    jnp.where(sub == 2 * n_blk, off_hi,
                              jnp.where(sub == 2 * n_blk + 1, off - off_hi, 0.0)))
    return jnp.where(foreign, ext, 0.0).astype(BF16)


def _mixer_call(slopes, kext, q_t, k, v_t, xl, conv_w, conv_b, w_gate, a_b, x_b, lam):
    b, s, _ = k.shape
    pairs = ATTN_PAIRS_PER_STEP
    width = pairs * LANES
    heads = pairs * HEADS_PER_PAIR
    row = lambda a: a.reshape(1, D_LRU)
    spec = pl.BlockSpec((1, s, width), lambda j, i, sl: (i, 0, j))
    spec_t = pl.BlockSpec((1, width, s), lambda j, i, sl: (i, j, 0))
    vec = pl.BlockSpec((1, width), lambda j, i, sl: (0, j))
    slab_specs = [pl.BlockSpec((1, s, LANES), lambda j, i, sl, n=n: (i, 0, j * pairs + n))
                  for n in range(pairs)]
    sub_len = LRU_SEGMENT * SUBLANES
    return pl.pallas_call(
        _mixer_kernel,
        grid_spec=pltpu.PrefetchScalarGridSpec(
            num_scalar_prefetch=1,
            grid=(D_ATTN // width, b),
            in_specs=[spec_t, spec, spec_t,
                      pl.BlockSpec((heads, s, LANES), lambda j, i, sl: (j, 0, 0),
                                   pipeline_mode=pl.Buffered(1))]
            + slab_specs
            + [pl.BlockSpec((CONV_WIDTH, width), lambda j, i, sl: (0, j)), vec,
               pl.BlockSpec((pairs, LANES, 2 * LANES), lambda j, i, sl: (j, 0, 0)), vec, vec, vec],
            out_specs=[spec, spec],
            scratch_shapes=[
                pltpu.VMEM((heads, s, LANES), BF16),
                pltpu.VMEM((heads, LANES, s), BF16),
                pltpu.VMEM((heads, HEAD_DIM + 2 * SUBLANES, s), BF16),
                pltpu.VMEM((2, s, MOBA_BLOCK), F32),
                pltpu.VMEM((2, s, MOBA_BLOCK), BF16),
                pltpu.VMEM((width, s), F32),
                pltpu.VMEM((pairs, SUBLANES + sub_len, LANES), F32),
                pltpu.VMEM((pairs, s, LANES), F32),
                pltpu.VMEM((pairs, 2, LRU_CHUNK, LANES), F32),
                pltpu.VMEM((pairs, 2, LRU_CHUNK, 2 * LANES), F32),
            ],
        ),
        out_shape=[jax.ShapeDtypeStruct((b, s, D_ATTN), BF16), jax.ShapeDtypeStruct((b, s, D_LRU), BF16)],
        compiler_params=pltpu.CompilerParams(
            dimension_semantics=("arbitrary", "arbitrary"), vmem_limit_bytes=VMEM_LIMIT_BYTES),
        name="mixer",
    )(slopes, q_t, k, v_t, kext, *([xl] * pairs), conv_w, row(conv_b), w_gate, row(a_b), row(x_b),
      row(lam))


def _lru_units(x_seq, cw, cb, w_gate, a_b, x_b, lam, head_scr, h_scr, xc_scr, g_scr, store_h):
    s_len = x_seq.shape[0]
    seg = LRU_SEGMENT
    sub_len = seg * SUBLANES
    n_sub = LRU_CHUNK // sub_len
    n_chunks = s_len // LRU_CHUNK

    neg_lam = -lam
    softplus = jnp.maximum(neg_lam, 0.0) + jnp.log1p(jnp.exp(-jnp.abs(neg_lam)))
    half_rate = (0.5 * RG_C) * softplus
    ab_half = 0.5 * a_b
    xb_half = 0.5 * x_b
    sub = lax.broadcasted_iota(jnp.int32, (SUBLANES, LANES), 0)
    steps = (1, 2, 4)
    in_vreg = [sub >= step for step in steps]
    state = {"h": jnp.zeros((SUBLANES, LANES), F32)}

    def strided(ref, start):
        return ref[pl.ds(start, SUBLANES, stride=seg), :]

    def init():
        head_scr[0:SUBLANES, :] = jnp.zeros((SUBLANES, LANES), F32)
        head_scr[SUBLANES:, :] = x_seq[0:sub_len, :]

    def conv_and_gates(ci):
        xc_parts = []
        for c in range(n_sub):
            if ci == 0 and c == 0:
                src, t0 = head_scr, SUBLANES
            else:
                src, t0 = x_seq, ci * LRU_CHUNK
            cache = {}
            for g in range(seg):
                acc = cb
                for j in range(CONV_WIDTH):
                    off = c * sub_len + g - (CONV_WIDTH - 1) + j
                    if off not in cache:
                        cache[off] = strided(src, t0 + off)
                    acc = acc + cw[j:j + 1, :] * cache[off]
                xc_parts.append(acc)
        xc = jnp.concatenate(xc_parts, axis=0)
        xc_scr[ci % 2] = xc
        g_scr[ci % 2] = jnp.dot(xc.astype(BF16), w_gate[...], preferred_element_type=F32)

    def recurrence(ci, c):
        t0 = ci * LRU_CHUNK
        rows = slice(c * sub_len, (c + 1) * sub_len)
        h_in = state["h"]
        xc = xc_scr[ci % 2, rows, :]
        t_r = jnp.tanh(g_scr[ci % 2, rows, :LANES] + ab_half)
        t_i = jnp.tanh(g_scr[ci % 2, rows, LANES:] + xb_half)
        neg_log_a = half_rate * t_r + half_rate
        a = jnp.exp(-neg_log_a)
        y = jnp.tanh(neg_log_a) * (a * a + 1.0)
        root = jnp.where(y == 0.0, 0.0, y * lax.rsqrt(y))
        bterm = root * ((t_i + 1.0) * (0.5 * xc))

        comp = []
        for g in range(seg):
            av, bv = a[g * SUBLANES:(g + 1) * SUBLANES], bterm[g * SUBLANES:(g + 1) * SUBLANES]
            if g > 0:
                bv = av * comp[-1][1] + bv
                av = av * comp[-1][0]
            comp.append((av, bv))
        pa, pb = comp[-1]
        for step, valid in zip(steps, in_vreg):
            a_sh = jnp.where(valid, pltpu.roll(pa, step, axis=0), 1.0)
            b_sh = jnp.where(valid, pltpu.roll(pb, step, axis=0), 0.0)
            pb = pa * b_sh + pb
            pa = pa * a_sh
        ea = jnp.where(in_vreg[0], pltpu.roll(pa, 1, axis=0), 1.0)
        eb = jnp.where(in_vreg[0], pltpu.roll(pb, 1, axis=0), 0.0)
        seg_in = ea * h_in + eb
        for g in range(seg):
            h_g = comp[g][0] * seg_in + comp[g][1]
            h_scr[pl.ds(t0 + c * sub_len + g, SUBLANES, stride=seg), :] = h_g
        h_last = pa * h_in + pb
        state["h"] = jnp.broadcast_to(h_last[SUBLANES - 1:SUBLANES, :], (SUBLANES, LANES))

    def finish():
        store_h(h_scr[...].astype(BF16))

    units = [init, lambda: conv_and_gates(0)]
    for ci in range(n_chunks):
        for c in range(n_sub):
            units.append(lambda ci=ci, c=c: recurrence(ci, c))
            if c == n_sub // 2 - 1 and ci + 1 < n_chunks:
                units.append(lambda ci=ci: conv_and_gates(ci + 1))
    units.append(finish)
    return units


def _gated_norm(y_ref, z_ref, nw_ref):
    y = y_ref[0].astype(F32)
    ms = jnp.mean(y * y, axis=-1, keepdims=True)
    return ((y * lax.rsqrt(ms + EPS) * nw_ref[...]) * _silu(z_ref[0].astype(F32))).astype(BF16)


def _outproj_math(ya_ref, za_ref, yl_ref, zl_ref, x_ref, mod_ref, anw_ref, lnw_ref, w_ref):
    y = jnp.dot(_gated_norm(ya_ref, za_ref, anw_ref), w_ref[0:D_ATTN, :], preferred_element_type=F32)
    y = y + jnp.dot(_gated_norm(yl_ref, zl_ref, lnw_ref), w_ref[D_ATTN:, :],
                    preferred_element_type=F32)
    return x_ref[0] + mod_ref[0, 2:3, :] * y


def _outproj_kernel(ya_ref, za_ref, yl_ref, zl_ref, x_ref, mod_ref, anw_ref, lnw_ref, wf_ref, fnw_ref,
                    o_ref, w_ref):
    _cast_weight_once(wf_ref, w_ref)
    out = _outproj_math(ya_ref, za_ref, yl_ref, zl_ref, x_ref, mod_ref, anw_ref, lnw_ref, w_ref)
    ms = jnp.mean(out * out, axis=-1, keepdims=True)
    o_ref[0] = out * lax.rsqrt(ms + EPS) * fnw_ref[...]


def _outin_kernel(ya_ref, za_ref, yl_ref, zl_ref, x_ref, mod_ref, anw_ref, lnw_ref, wof_ref,
                  modn_ref, nwn_ref, wif_ref,
                  xo_ref, q_ref, k_ref, v_ref, zao_ref, xlo_ref, zlo_ref, wo_ref, wi_ref):
    _cast_weight_once(wof_ref, wo_ref)
    _cast_weight_once(wif_ref, wi_ref)
    out = _outproj_math(ya_ref, za_ref, yl_ref, zl_ref, x_ref, mod_ref, anw_ref, lnw_ref, wo_ref)
    xo_ref[0] = out
    _inproj_math(out, modn_ref, nwn_ref, wi_ref, (q_ref, k_ref, v_ref, zao_ref, xlo_ref, zlo_ref))


def _proj_specs(d, layer, tm=ROW_TILE):
    return dict(
        half=pl.BlockSpec((1, tm, D_ATTN), lambda i, j: (i, j, 0)),
        full=pl.BlockSpec((1, tm, d), lambda i, j: (i, j, 0)),
        mod=pl.BlockSpec((1, 3, d), lambda i, j: (i, 0, 0)),
        nvec=pl.BlockSpec((1, D_ATTN), lambda i, j: (0, 0)),
        dvec=pl.BlockSpec((1, d), lambda i, j: (0, 0)),
        w_out=pl.BlockSpec((1, d, d), lambda i, j: (layer, 0, 0), pipeline_mode=pl.Buffered(1)),
        w_in=pl.BlockSpec((1, d, D_IN), lambda i, j: (layer + 1, 0, 0), pipeline_mode=pl.Buffered(1)),
    )


def _outproj_call(ya, za, yl, zl, x, mod3, anw, lnw, w_out, layer, fnw):
    b, s, d = x.shape
    sp = _proj_specs(d, layer, OUT_ROW_TILE)
    return pl.pallas_call(
        _outproj_kernel,
        grid=(b, s // OUT_ROW_TILE),
        in_specs=[sp["half"]] * 4 + [sp["full"], sp["mod"], sp["nvec"], sp["nvec"], sp["w_out"],
                                     sp["dvec"]],
        out_specs=sp["full"],
        out_shape=jax.ShapeDtypeStruct((b, s, d), F32),
        scratch_shapes=[pltpu.VMEM((d, d), BF16)],
        compiler_params=pltpu.CompilerParams(
            dimension_semantics=("arbitrary", "arbitrary"), vmem_limit_bytes=VMEM_LIMIT_BYTES),
        name="outproj",
    )(ya, za, yl, zl, x, mod3, anw.reshape(1, D_ATTN), lnw.reshape(1, D_LRU), w_out,
      fnw.reshape(1, d))


def _outin_call(ya, za, yl, zl, x, mod3, anw, lnw, w_out, layer, mod3_next, norm_w_next, w_in):
    b, s, d = x.shape
    sp = _proj_specs(d, layer)
    proj_specs, proj_shapes = _inproj_outputs(b, s)
    return pl.pallas_call(
        _outin_kernel,
        grid=(b, s // ROW_TILE),
        in_specs=[sp["half"]] * 4 + [sp["full"], sp["mod"], sp["nvec"], sp["nvec"], sp["w_out"],
                                     sp["mod"], sp["dvec"], sp["w_in"]],
        out_specs=[sp["full"]] + proj_specs,
        out_shape=[jax.ShapeDtypeStruct((b, s, d), F32)] + proj_shapes,
        scratch_shapes=[pltpu.VMEM((d, d), BF16), pltpu.VMEM((d, D_IN), BF16)],
        compiler_params=pltpu.CompilerParams(
            dimension_semantics=("arbitrary", "arbitrary"), vmem_limit_bytes=PROJ_VMEM_LIMIT_BYTES),
        name="outproj_inproj",
    )(ya, za, yl, zl, x, mod3, anw.reshape(1, D_ATTN), lnw.reshape(1, D_LRU), w_out,
      mod3_next, norm_w_next.reshape(1, d), w_in)


def _gate_weights(a_w, x_w):
    per_slab = LANES // LRU_BLOCK
    slabs = []
    for j in range(D_LRU // LANES):
        blocks = range(j * per_slab, (j + 1) * per_slab)
        diag = lambda w: jax.scipy.linalg.block_diag(*[w[g] for g in blocks])
        slabs.append(jnp.concatenate([diag(a_w), diag(x_w)], axis=1))
    return (0.5 * jnp.stack(slabs)).astype(BF16)


def kernel(x, c, norm_w, ada_w, ada_b, w_in, conv_w, conv_b, rg_a_w, rg_a_b, rg_x_w, rg_x_b,
           rg_lambda, attn_out_norm_w, lru_out_norm_w, w_out, final_norm_w):
    n_layers = ada_w.shape[0]
    b = x.shape[0]
    mod = _mod_call(c, ada_w, ada_b)
    slopes = jnp.exp2(-8.0 * (jnp.arange(N_HEADS, dtype=F32) + 1.0) / N_HEADS)
    kext = _key_side_table(slopes, x.shape[1])
    mod3 = [mod[l].reshape(b, 3, D_MODEL) for l in range(n_layers)]
    q, k, v, za, xl, zl = _inproj_call(x, mod3[0], norm_w[0], w_in, 0)
    for l in range(n_layers):
        ya, yl = _mixer_call(slopes, kext, q, k, v, xl, conv_w[l], conv_b[l],
                             _gate_weights(rg_a_w[l], rg_x_w[l]), rg_a_b[l], rg_x_b[l], rg_lambda[l])
        if l + 1 < n_layers:
            x, q, k, v, za, xl, zl = _outin_call(
                ya, za, yl, zl, x, mod3[l], attn_out_norm_w[l], lru_out_norm_w[l], w_out, l,
                mod3[l + 1], norm_w[l + 1], w_in)
        else:
            x = _outproj_call(ya, za, yl, zl, x, mod3[l], attn_out_norm_w[l], lru_out_norm_w[l],
                              w_out, l, final_norm_w)
    return x
```

```python
import math

import jax
import jax.numpy as jnp
from jax import lax
from jax.experimental import pallas as pl
from jax.experimental.pallas import tpu as pltpu

D_MODEL = 1024
D_ATTN = 512
D_LRU = 512
HEAD_DIM = 64
N_HEADS = D_ATTN // HEAD_DIM
N_LRU_BLOCKS = 8
LRU_BLOCK = D_LRU // N_LRU_BLOCKS
CONV_WIDTH = 4
RG_C = 8.0
MOBA_BLOCK = 256
MOBA_TOPK = 3
D_IN = 4 * D_ATTN + 2 * D_LRU
EPS = 1e-6

F32 = jnp.float32
BF16 = jnp.bfloat16
NEG_BIG = -1e30
LOG2E = math.log2(math.e)

LANES = 128
SUBLANES = 8
VMEM_LIMIT_BYTES = 48 * 1024 * 1024
PROJ_VMEM_LIMIT_BYTES = 56 * 1024 * 1024

ROW_TILE = 512
IN_ROW_TILE = 1024
OUT_ROW_TILE = 1024
WEIGHT_CAST_ROWS = 64
LRU_CHUNK = 256
LRU_SEGMENT = 4
HEADS_PER_PAIR = LANES // HEAD_DIM
FEATURE_MAJOR_OUTPUTS = (0, 2)
ATTN_PAIRS_PER_STEP = 2
SCORE_SLOTS = 3
PREPARE_AT = 6


def _sigmoid(v):
    return 0.5 * jnp.tanh(0.5 * v) + 0.5


def _silu(v):
    return v * _sigmoid(v)


def _split_bf16(v):
    hi = v.astype(BF16).astype(F32)
    lo = (v - hi).astype(BF16).astype(F32)
    return hi, lo


def _mod_kernel(c_ref, w_ref, b_ref, o_ref):
    s = _silu(c_ref[...]).astype(BF16)
    w = w_ref[0].astype(BF16)
    o_ref[0] = jnp.dot(s, w, preferred_element_type=F32) + b_ref[0]


def _mod_call(c, ada_w, ada_b):
    n_layers, d, d3 = ada_w.shape
    b = c.shape[0]
    tn = 1024
    return pl.pallas_call(
        _mod_kernel,
        grid=(n_layers, d3 // tn),
        in_specs=[
            pl.BlockSpec((b, d), lambda l, j: (0, 0)),
            pl.BlockSpec((1, d, tn), lambda l, j: (l, 0, j)),
            pl.BlockSpec((1, 1, tn), lambda l, j: (l, 0, j)),
        ],
        out_specs=pl.BlockSpec((1, b, tn), lambda l, j: (l, 0, j)),
        out_shape=jax.ShapeDtypeStruct((n_layers, b, d3), F32),
        compiler_params=pltpu.CompilerParams(
            dimension_semantics=("arbitrary", "arbitrary"), vmem_limit_bytes=VMEM_LIMIT_BYTES),
        name="adaln_mod",
    )(c, ada_w, ada_b.reshape(n_layers, 1, d3))


def _cast_weight_once(w_ref, wbf_scr):
    rows = w_ref.shape[1]

    @pl.when((pl.program_id(0) == 0) & (pl.program_id(1) == 0))
    def _():
        def body(i, carry):
            r0 = pl.multiple_of(i * WEIGHT_CAST_ROWS, WEIGHT_CAST_ROWS)
            wbf_scr[pl.ds(r0, WEIGHT_CAST_ROWS), :] = w_ref[0, pl.ds(r0, WEIGHT_CAST_ROWS), :].astype(BF16)
            return carry
        lax.fori_loop(0, rows // WEIGHT_CAST_ROWS, body, 0)


def _inproj_math(x, mod_ref, nw_ref, w_ref, out_refs):
    ms = jnp.mean(x * x, axis=-1, keepdims=True)
    y = x * lax.rsqrt(ms + EPS) * nw_ref[...]
    shift = mod_ref[0, 0:1, :]
    scale = mod_ref[0, 1:2, :]
    h = (y * (1.0 + scale) + shift).astype(BF16)
    for j, ref in enumerate(out_refs):
        r = jnp.dot(h, w_ref[:, j * D_ATTN:(j + 1) * D_ATTN], preferred_element_type=F32)
        if j == 0:
            r = r * (HEAD_DIM ** -0.5 * LOG2E)
        if j in FEATURE_MAJOR_OUTPUTS:
            r = r.T
        ref[0] = r.astype(ref.dtype)


def _inproj_kernel(x_ref, mod_ref, nw_ref, wf_ref, q_ref, k_ref, v_ref, za_ref, xl_ref, zl_ref,
                   w_ref):
    _cast_weight_once(wf_ref, w_ref)
    _inproj_math(x_ref[0], mod_ref, nw_ref, w_ref, (q_ref, k_ref, v_ref, za_ref, xl_ref, zl_ref))


def _inproj_outputs(b, s, tm=ROW_TILE):
    token_major = pl.BlockSpec((1, tm, D_ATTN), lambda i, j: (i, j, 0))
    feature_major = pl.BlockSpec((1, D_ATTN, tm), lambda i, j: (i, 0, j))
    dtypes = (BF16, BF16, BF16, BF16, F32, BF16)
    specs, shapes = [], []
    for j, dt in enumerate(dtypes):
        fm = j in FEATURE_MAJOR_OUTPUTS
        specs.append(feature_major if fm else token_major)
        shapes.append(jax.ShapeDtypeStruct((b, D_ATTN, s) if fm else (b, s, D_ATTN), dt))
    return specs, shapes


def _inproj_call(x, mod3, norm_w, w_in, layer):
    b, s, d = x.shape
    tm = IN_ROW_TILE
    out_specs, out_shape = _inproj_outputs(b, s, tm)
    return pl.pallas_call(
        _inproj_kernel,
        grid=(b, s // tm),
        in_specs=[
            pl.BlockSpec((1, tm, d), lambda i, j: (i, j, 0)),
            pl.BlockSpec((1, 3, d), lambda i, j: (i, 0, 0)),
            pl.BlockSpec((1, d), lambda i, j: (0, 0)),
            pl.BlockSpec((1, d, D_IN), lambda i, j: (layer, 0, 0), pipeline_mode=pl.Buffered(1)),
        ],
        out_specs=out_specs,
        out_shape=out_shape,
        scratch_shapes=[pltpu.VMEM((d, D_IN), BF16)],
        compiler_params=pltpu.CompilerParams(
            dimension_semantics=("arbitrary", "arbitrary"), vmem_limit_bytes=VMEM_LIMIT_BYTES),
        name="inproj",
    )(x, mod3, norm_w.reshape(1, d), w_in)


def _mixer_kernel(slopes_ref, q_ref, k_ref, v_ref, kext_ref, *rest):
    n_slabs = k_ref.shape[2] // LANES
    xl_refs = rest[:n_slabs]
    (cw_ref, cb_ref, wg_ref, ab_ref, xb_ref, lam_ref, o_ref, ol_ref,
     kaug_scr, qaug_scr, vaug_scr, s_scr, p_scr, out_scr,
     head_scr, h_scr, xc_scr, g_scr) = rest[n_slabs:]
    s_len = k_ref.shape[1]
    n_blk = s_len // MOBA_BLOCK
    n_pairs = k_ref.shape[2] // LANES
    n_heads = n_pairs * HEADS_PER_PAIR
    first_head = pl.program_id(0) * n_heads

    prow = lax.broadcasted_iota(jnp.int32, (n_blk, s_len), 0)
    pcol = lax.broadcasted_iota(jnp.int32, (n_blk, s_len), 1)
    block_mean = jnp.where(pcol // MOBA_BLOCK == prow, 1.0 / MOBA_BLOCK, 0.0).astype(BF16)
    klane = lax.broadcasted_iota(jnp.int32, (s_len, LANES), 1)
    mlane = lax.broadcasted_iota(jnp.int32, (n_blk, LANES), 1)
    n_iota = prow
    q_blk = pcol // MOBA_BLOCK
    key_off = lax.broadcasted_iota(jnp.int32, (MOBA_BLOCK, MOBA_BLOCK), 0)
    qry_off = lax.broadcasted_iota(jnp.int32, (MOBA_BLOCK, MOBA_BLOCK), 1)
    causal = key_off <= qry_off

    def prepare(pp):
        lanes = slice(pp * LANES, (pp + 1) * LANES)
        kp = k_ref[0, :, lanes]
        q_t = q_ref[0, lanes, :]
        v_t = v_ref[0, lanes, :]
        kmean = jnp.dot(block_mean, kp, preferred_element_type=F32)
        terms = []
        for hd in range(HEADS_PER_PAIR):
            rest = jnp.where(mlane // HEAD_DIM == hd, kmean, 0.0)
            for _ in range(3):
                part = rest.astype(BF16).astype(F32)
                terms.append(part)
                rest = rest - part
        terms = jnp.concatenate(terms, axis=0).astype(BF16)
        half = s_len // 2
        gates = jnp.concatenate(
            [jnp.dot(terms, q_t[:, :half], preferred_element_type=F32),
             jnp.dot(terms, q_t[:, half:], preferred_element_type=F32)], axis=1)
        for hd in range(HEADS_PER_PAIR):
            h = pp * HEADS_PER_PAIR + hd
            kaug_scr[h] = jnp.where(klane // HEAD_DIM == hd, kp, kext_ref[h])
            slope2 = slopes_ref[first_head + h] * LOG2E
            g_hi, g_mid, g_lo = (gates[(3 * hd + t) * n_blk:(3 * hd + t + 1) * n_blk]
                                 for t in range(3))
            g_t = g_hi + g_mid + g_lo
            rank = jnp.zeros((n_blk, s_len), jnp.int32)
            for m in range(n_blk):
                gm = g_t[m:m + 1, :]
                beats = (gm > g_t) | ((gm == g_t) & (m < n_iota))
                rank = rank + jnp.where(beats & (m < q_blk), 1, 0)
            sel = ((n_iota < q_blk) & (rank < MOBA_TOPK)) | (n_iota == q_blk)
            bias_hi, bias_lo = _split_bf16(slope2 * ((n_iota - q_blk) * MOBA_BLOCK).astype(F32))
            selb_hi = jnp.where(sel, bias_hi, NEG_BIG)
            selb_lo = jnp.where(sel, bias_lo, 0.0)
            flag = jnp.where(n_iota < 2, 1.0, 0.0)
            ext = jnp.concatenate(
                [selb_hi, selb_lo, flag, jnp.zeros((HEAD_DIM - 3 * n_blk, s_len), F32)], axis=0)
            own = slice(hd * HEAD_DIM, (hd + 1) * HEAD_DIM)
            other = slice((1 - hd) * HEAD_DIM, (2 - hd) * HEAD_DIM)
            qaug_scr[h, own, :] = q_t[own]
            qaug_scr[h, other, :] = ext.astype(BF16)
            vaug_scr[h, 0:HEAD_DIM, :] = v_t[own]
            vaug_scr[h, HEAD_DIM:, :] = jnp.ones((2 * SUBLANES, s_len), BF16)

    items = [(h, qb) for h in range(n_heads)
             for qb in (range(n_blk) if h % 2 == 0 else reversed(range(n_blk)))]
    per_pair = HEADS_PER_PAIR * n_blk

    def scores(i):
        h, qb = items[i]
        slot = i % SCORE_SLOTS
        q0, kk = qb * MOBA_BLOCK, (qb + 1) * MOBA_BLOCK
        qa = qaug_scr[h, :, q0:kk]
        m = None
        for r0 in range(0, kk, MOBA_BLOCK):
            s = jnp.dot(kaug_scr[h, r0:r0 + MOBA_BLOCK, :], qa, preferred_element_type=F32)
            if r0 == q0:
                s = jnp.where(causal, s, NEG_BIG)
            s_scr[slot, r0:r0 + MOBA_BLOCK, :] = s
            mh = jnp.max(s.reshape(MOBA_BLOCK // SUBLANES, SUBLANES, MOBA_BLOCK), axis=0)
            m = mh if m is None else jnp.maximum(m, mh)
        return jnp.max(m, axis=0, keepdims=True)

    def probs(i, m):
        _, qb = items[i]
        kk = (qb + 1) * MOBA_BLOCK
        p_scr[i % 2, 0:kk, :] = jnp.exp2(s_scr[i % SCORE_SLOTS, 0:kk, :] - m).astype(BF16)

    def weighted(i):
        h, qb = items[i]
        slot = i % 2
        q0, kk = qb * MOBA_BLOCK, (qb + 1) * MOBA_BLOCK
        o = None
        for r0, r1 in ((0, kk // 2), (kk // 2, kk)):
            part = jnp.dot(vaug_scr[h, :, r0:r1], p_scr[slot, r0:r1, :], preferred_element_type=F32)
            o = part if o is None else o + part
        out_scr[h * HEAD_DIM:(h + 1) * HEAD_DIM, q0:kk] = o[0:HEAD_DIM] / o[HEAD_DIM:HEAD_DIM + 1]

    lru_units = []
    for sl in range(n_slabs):
        lanes = slice(sl * LANES, (sl + 1) * LANES)

        def store_h(value, lanes=lanes):
            ol_ref[0, :, lanes] = value

        lru_units += _lru_units(
            xl_refs[sl].at[0], cw_ref[:, lanes], cb_ref[:, lanes], wg_ref.at[sl], ab_ref[:, lanes],
            xb_ref[:, lanes], lam_ref[:, lanes], head_scr.at[sl], h_scr.at[sl], xc_scr.at[sl],
            g_scr.at[sl], store_h)
    lru_done = 0

    n_items = len(items)
    prepare(0)
    col_max = {i: scores(i) for i in range(SCORE_SLOTS)}
    probs(0, col_max.pop(0))
    for i in range(n_items):
        pp, within = divmod(i, per_pair)
        if within == PREPARE_AT and pp + 1 < n_pairs:
            prepare(pp + 1)
        if i + SCORE_SLOTS < n_items:
            col_max[i + SCORE_SLOTS] = scores(i + SCORE_SLOTS)
        if i + 1 < n_items:
            probs(i + 1, col_max.pop(i + 1))
        weighted(i)
        if within == per_pair - 1:
            lanes = slice(pp * LANES, (pp + 1) * LANES)
            o_ref[0, :, lanes] = out_scr[lanes, :].T.astype(o_ref.dtype)
        lru_target = (i + 1) * len(lru_units) // n_items
        while lru_done < lru_target:
            lru_units[lru_done]()
            lru_done += 1


def _key_side_table(slopes, s_len):
    n_blk = s_len // MOBA_BLOCK
    shape = (N_HEADS, s_len, LANES)
    head = lax.broadcasted_iota(jnp.int32, shape, 0)
    pos = lax.broadcasted_iota(jnp.int32, shape, 1)
    lane = lax.broadcasted_iota(jnp.int32, shape, 2)
    sub = lane % HEAD_DIM
    foreign = lane // HEAD_DIM != head % HEADS_PER_PAIR
    onehot = jnp.where(sub % n_blk == pos // MOBA_BLOCK, 1.0, 0.0)
    off = (slopes * LOG2E)[:, None, None] * (pos % MOBA_BLOCK).astype(F32)
    off_hi = lax.bitcast_convert_type(
        lax.bitcast_convert_type(off, jnp.uint32) & jnp.uint32(0xFFFF0000), F32)
    ext = jnp.where(sub < 2 * n_blk, onehot,
                    jnp.where(sub == 2 * n_blk, off_hi,
                              jnp.where(sub == 2 * n_blk + 1, off - off_hi, 0.0)))
    return jnp.where(foreign, ext, 0.0).astype(BF16)


def _mixer_call(slopes, kext, q_t, k, v_t, xl, conv_w, conv_b, w_gate, a_b, x_b, lam):
    b, s, _ = k.shape
    pairs = ATTN_PAIRS_PER_STEP
    width = pairs * LANES
    heads = pairs * HEADS_PER_PAIR
    row = lambda a: a.reshape(1, D_LRU)
    spec = pl.BlockSpec((1, s, width), lambda j, i, sl: (i, 0, j))
    spec_t = pl.BlockSpec((1, width, s), lambda j, i, sl: (i, j, 0))
    vec = pl.BlockSpec((1, width), lambda j, i, sl: (0, j))
    slab_specs = [pl.BlockSpec((1, s, LANES), lambda j, i, sl, n=n: (i, 0, j * pairs + n))
                  for n in range(pairs)]
    sub_len = LRU_SEGMENT * SUBLANES
    return pl.pallas_call(
        _mixer_kernel,
        grid_spec=pltpu.PrefetchScalarGridSpec(
            num_scalar_prefetch=1,
            grid=(D_ATTN // width, b),
            in_specs=[spec_t, spec, spec_t,
                      pl.BlockSpec((heads, s, LANES), lambda j, i, sl: (j, 0, 0),
                                   pipeline_mode=pl.Buffered(1))]
            + slab_specs
            + [pl.BlockSpec((CONV_WIDTH, width), lambda j, i, sl: (0, j)), vec,
               pl.BlockSpec((pairs, LANES, 2 * LANES), lambda j, i, sl: (j, 0, 0)), vec, vec, vec],
            out_specs=[spec, spec],
            scratch_shapes=[
                pltpu.VMEM((heads, s, LANES), BF16),
                pltpu.VMEM((heads, LANES, s), BF16),
                pltpu.VMEM((heads, HEAD_DIM + 2 * SUBLANES, s), BF16),
                pltpu.VMEM((SCORE_SLOTS, s, MOBA_BLOCK), F32),
                pltpu.VMEM((2, s, MOBA_BLOCK), BF16),
                pltpu.VMEM((width, s), F32),
                pltpu.VMEM((pairs, SUBLANES + sub_len, LANES), F32),
                pltpu.VMEM((pairs, s, LANES), F32),
                pltpu.VMEM((pairs, 2, LRU_CHUNK, LANES), F32),
                pltpu.VMEM((pairs, 2, LRU_CHUNK, 2 * LANES), F32),
            ],
        ),
        out_shape=[jax.ShapeDtypeStruct((b, s, D_ATTN), BF16), jax.ShapeDtypeStruct((b, s, D_LRU), BF16)],
        compiler_params=pltpu.CompilerParams(
            dimension_semantics=("arbitrary", "arbitrary"), vmem_limit_bytes=VMEM_LIMIT_BYTES),
        name="mixer",
    )(slopes, q_t, k, v_t, kext, *([xl] * pairs), conv_w, row(conv_b), w_gate, row(a_b), row(x_b),
      row(lam))


def _lru_units(x_seq, cw, cb, w_gate, a_b, x_b, lam, head_scr, h_scr, xc_scr, g_scr, store_h):
    s_len = x_seq.shape[0]
    seg = LRU_SEGMENT
    sub_len = seg * SUBLANES
    n_sub = LRU_CHUNK // sub_len
    n_chunks = s_len // LRU_CHUNK

    neg_lam = -lam
    softplus = jnp.maximum(neg_lam, 0.0) + jnp.log1p(jnp.exp(-jnp.abs(neg_lam)))
    half_rate = (0.5 * RG_C) * softplus
    rate_log2 = -LOG2E * half_rate
    ab_half = 0.5 * a_b
    xb_half = 0.5 * x_b
    cw_half = 0.5 * cw
    cb_half = 0.5 * cb
    sub = lax.broadcasted_iota(jnp.int32, (SUBLANES, LANES), 0)
    steps = (1, 2, 4)
    in_vreg = [sub >= step for step in steps]
    state = {"h": jnp.zeros((SUBLANES, LANES), F32)}

    def strided(ref, start):
        return ref[pl.ds(start, SUBLANES, stride=seg), :]

    def init():
        head_scr[0:SUBLANES, :] = jnp.zeros((SUBLANES, LANES), F32)
        head_scr[SUBLANES:, :] = x_seq[0:sub_len, :]

    def conv_and_gates(ci):
        xc_parts = []
        for c in range(n_sub):
            if ci == 0 and c == 0:
                src, t0 = head_scr, SUBLANES
            else:
                src, t0 = x_seq, ci * LRU_CHUNK
            cache = {}
            for g in range(seg):
                acc = cb_half
                for j in range(CONV_WIDTH):
                    off = c * sub_len + g - (CONV_WIDTH - 1) + j
                    if off not in cache:
                        cache[off] = strided(src, t0 + off)
                    acc = acc + cw_half[j:j + 1, :] * cache[off]
                xc_parts.append(acc)
        hx = jnp.concatenate(xc_parts, axis=0)
        xc_scr[ci % 2] = hx
        g_scr[ci % 2] = jnp.dot(hx.astype(BF16), w_gate[...], preferred_element_type=F32)

    def recurrence(ci, c):
        t0 = ci * LRU_CHUNK
        rows = slice(c * sub_len, (c + 1) * sub_len)
        h_in = state["h"]
        hx = xc_scr[ci % 2, rows, :]
        t_r = jnp.tanh(g_scr[ci % 2, rows, :LANES] + ab_half)
        t_i = jnp.tanh(g_scr[ci % 2, rows, LANES:] + xb_half)
        two_r = t_r + 1.0
        neg_log_a = half_rate * two_r
        a = jnp.exp2(rate_log2 * two_r)
        y = jnp.tanh(neg_log_a) * (a * a + 1.0)
        root = jnp.where(y == 0.0, 0.0, y * lax.rsqrt(y))
        bterm = root * ((t_i + 1.0) * hx)

        comp = []
        for g in range(seg):
            av, bv = a[g * SUBLANES:(g + 1) * SUBLANES], bterm[g * SUBLANES:(g + 1) * SUBLANES]
            if g > 0:
                bv = av * comp[-1][1] + bv
                av = av * comp[-1][0]
            comp.append((av, bv))
        pa, pb = comp[-1]
        for step, valid in zip(steps, in_vreg):
            a_sh = jnp.where(valid, pltpu.roll(pa, step, axis=0), 1.0)
            b_sh = jnp.where(valid, pltpu.roll(pb, step, axis=0), 0.0)
            pb = pa * b_sh + pb
            pa = pa * a_sh
        ea = jnp.where(in_vreg[0], pltpu.roll(pa, 1, axis=0), 1.0)
        eb = jnp.where(in_vreg[0], pltpu.roll(pb, 1, axis=0), 0.0)
        seg_in = ea * h_in + eb
        for g in range(seg):
            h_g = comp[g][0] * seg_in + comp[g][1]
            h_scr[pl.ds(t0 + c * sub_len + g, SUBLANES, stride=seg), :] = h_g
        h_last = pa * h_in + pb
        state["h"] = jnp.broadcast_to(h_last[SUBLANES - 1:SUBLANES, :], (SUBLANES, LANES))

    def finish():
        store_h(h_scr[...].astype(BF16))

    units = [init, lambda: conv_and_gates(0)]
    for ci in range(n_chunks):
        for c in range(n_sub):
            units.append(lambda ci=ci, c=c: recurrence(ci, c))
            if c == n_sub // 2 - 1 and ci + 1 < n_chunks:
                units.append(lambda ci=ci: conv_and_gates(ci + 1))
    units.append(finish)
    return units


def _gated_norm(y_ref, z_ref, nw_ref):
    y = y_ref[0].astype(F32)
    ms = jnp.mean(y * y, axis=-1, keepdims=True)
    return ((y * lax.rsqrt(ms + EPS) * nw_ref[...]) * _silu(z_ref[0].astype(F32))).astype(BF16)


def _outproj_math(ya_ref, za_ref, yl_ref, zl_ref, x_ref, mod_ref, anw_ref, lnw_ref, w_ref):
    y = jnp.dot(_gated_norm(ya_ref, za_ref, anw_ref), w_ref[0:D_ATTN, :], preferred_element_type=F32)
    y = y + jnp.dot(_gated_norm(yl_ref, zl_ref, lnw_ref), w_ref[D_ATTN:, :],
                    preferred_element_type=F32)
    return x_ref[0] + mod_ref[0, 2:3, :] * y


def _outproj_kernel(ya_ref, za_ref, yl_ref, zl_ref, x_ref, mod_ref, anw_ref, lnw_ref, wf_ref, fnw_ref,
                    o_ref, w_ref):
    _cast_weight_once(wf_ref, w_ref)
    out = _outproj_math(ya_ref, za_ref, yl_ref, zl_ref, x_ref, mod_ref, anw_ref, lnw_ref, w_ref)
    ms = jnp.mean(out * out, axis=-1, keepdims=True)
    o_ref[0] = out * lax.rsqrt(ms + EPS) * fnw_ref[...]


def _outin_kernel(ya_ref, za_ref, yl_ref, zl_ref, x_ref, mod_ref, anw_ref, lnw_ref, wof_ref,
                  modn_ref, nwn_ref, wif_ref,
                  xo_ref, q_ref, k_ref, v_ref, zao_ref, xlo_ref, zlo_ref, wo_ref, wi_ref):
    _cast_weight_once(wof_ref, wo_ref)
    _cast_weight_once(wif_ref, wi_ref)
    out = _outproj_math(ya_ref, za_ref, yl_ref, zl_ref, x_ref, mod_ref, anw_ref, lnw_ref, wo_ref)
    xo_ref[0] = out
    _inproj_math(out, modn_ref, nwn_ref, wi_ref, (q_ref, k_ref, v_ref, zao_ref, xlo_ref, zlo_ref))


def _proj_specs(d, layer, tm=ROW_TILE):
    return dict(
        half=pl.BlockSpec((1, tm, D_ATTN), lambda i, j: (i, j, 0)),
        full=pl.BlockSpec((1, tm, d), lambda i, j: (i, j, 0)),
        mod=pl.BlockSpec((1, 3, d), lambda i, j: (i, 0, 0)),
        nvec=pl.BlockSpec((1, D_ATTN), lambda i, j: (0, 0)),
        dvec=pl.BlockSpec((1, d), lambda i, j: (0, 0)),
        w_out=pl.BlockSpec((1, d, d), lambda i, j: (layer, 0, 0), pipeline_mode=pl.Buffered(1)),
        w_in=pl.BlockSpec((1, d, D_IN), lambda i, j: (layer + 1, 0, 0), pipeline_mode=pl.Buffered(1)),
    )


def _outproj_call(ya, za, yl, zl, x, mod3, anw, lnw, w_out, layer, fnw):
    b, s, d = x.shape
    sp = _proj_specs(d, layer, OUT_ROW_TILE)
    return pl.pallas_call(
        _outproj_kernel,
        grid=(b, s // OUT_ROW_TILE),
        in_specs=[sp["half"]] * 4 + [sp["full"], sp["mod"], sp["nvec"], sp["nvec"], sp["w_out"],
                                     sp["dvec"]],
        out_specs=sp["full"],
        out_shape=jax.ShapeDtypeStruct((b, s, d), F32),
        scratch_shapes=[pltpu.VMEM((d, d), BF16)],
        compiler_params=pltpu.CompilerParams(
            dimension_semantics=("arbitrary", "arbitrary"), vmem_limit_bytes=VMEM_LIMIT_BYTES),
        name="outproj",
    )(ya, za, yl, zl, x, mod3, anw.reshape(1, D_ATTN), lnw.reshape(1, D_LRU), w_out,
      fnw.reshape(1, d))


def _outin_call(ya, za, yl, zl, x, mod3, anw, lnw, w_out, layer, mod3_next, norm_w_next, w_in):
    b, s, d = x.shape
    sp = _proj_specs(d, layer)
    proj_specs, proj_shapes = _inproj_outputs(b, s)
    return pl.pallas_call(
        _outin_kernel,
        grid=(b, s // ROW_TILE),
        in_specs=[sp["half"]] * 4 + [sp["full"], sp["mod"], sp["nvec"], sp["nvec"], sp["w_out"],
                                     sp["mod"], sp["dvec"], sp["w_in"]],
        out_specs=[sp["full"]] + proj_specs,
        out_shape=[jax.ShapeDtypeStruct((b, s, d), F32)] + proj_shapes,
        scratch_shapes=[pltpu.VMEM((d, d), BF16), pltpu.VMEM((d, D_IN), BF16)],
        compiler_params=pltpu.CompilerParams(
            dimension_semantics=("arbitrary", "arbitrary"), vmem_limit_bytes=PROJ_VMEM_LIMIT_BYTES),
        name="outproj_inproj",
    )(ya, za, yl, zl, x, mod3, anw.reshape(1, D_ATTN), lnw.reshape(1, D_LRU), w_out,
      mod3_next, norm_w_next.reshape(1, d), w_in)


def _gate_weights(a_w, x_w):
    per_slab = LANES // LRU_BLOCK
    slabs = []
    for j in range(D_LRU // LANES):
        blocks = range(j * per_slab, (j + 1) * per_slab)
        diag = lambda w: jax.scipy.linalg.block_diag(*[w[g] for g in blocks])
        slabs.append(jnp.concatenate([diag(a_w), diag(x_w)], axis=1))
    return jnp.stack(slabs).astype(BF16)


def kernel(x, c, norm_w, ada_w, ada_b, w_in, conv_w, conv_b, rg_a_w, rg_a_b, rg_x_w, rg_x_b,
           rg_lambda, attn_out_norm_w, lru_out_norm_w, w_out, final_norm_w):
    n_layers = ada_w.shape[0]
    b = x.shape[0]
    mod = _mod_call(c, ada_w, ada_b)
    slopes = jnp.exp2(-8.0 * (jnp.arange(N_HEADS, dtype=F32) + 1.0) / N_HEADS)
    kext = _key_side_table(slopes, x.shape[1])
    mod3 = [mod[l].reshape(b, 3, D_MODEL) for l in range(n_layers)]
    q, k, v, za, xl, zl = _inproj_call(x, mod3[0], norm_w[0], w_in, 0)
    for l in range(n_layers):
        ya, yl = _mixer_call(slopes, kext, q, k, v, xl, conv_w[l], conv_b[l],
                             _gate_weights(rg_a_w[l], rg_x_w[l]), rg_a_b[l], rg_x_b[l], rg_lambda[l])
        if l + 1 < n_layers:
            x, q, k, v, za, xl, zl = _outin_call(
                ya, za, yl, zl, x, mod3[l], attn_out_norm_w[l], lru_out_norm_w[l], w_out, l,
                mod3[l + 1], norm_w[l + 1], w_in)
        else:
            x = _outproj_call(ya, za, yl, zl, x, mod3[l], attn_out_norm_w[l], lru_out_norm_w[l],
                              w_out, l, final_norm_w)
    return x
```

```python
import math

import jax
import jax.numpy as jnp
import numpy as np
from jax import lax
from jax.experimental import pallas as pl
from jax.experimental.pallas import tpu as pltpu

D_MODEL = 1024
D_ATTN = 512
D_LRU = 512
HEAD_DIM = 64
N_HEADS = D_ATTN // HEAD_DIM
N_LRU_BLOCKS = 8
LRU_BLOCK = D_LRU // N_LRU_BLOCKS
CONV_WIDTH = 4
RG_C = 8.0
MOBA_BLOCK = 256
MOBA_TOPK = 3
D_IN = 4 * D_ATTN + 2 * D_LRU
EPS = 1e-6

F32 = jnp.float32
BF16 = jnp.bfloat16
NEG_BIG = -1e30
LOG2E = math.log2(math.e)

LANES = 128
SUBLANES = 8
VMEM_LIMIT_BYTES = 48 * 1024 * 1024
PROJ_VMEM_LIMIT_BYTES = 56 * 1024 * 1024

ROW_TILE = 512
IN_ROW_TILE = 1024
OUT_ROW_TILE = 1024
WEIGHT_CAST_ROWS = 64
LRU_CHUNK = 256
LRU_SEGMENT = 4
HEADS_PER_PAIR = LANES // HEAD_DIM
BIAS_TERMS = 3
FEATURE_MAJOR_OUTPUTS = (0, 2)
ATTN_PAIRS_PER_STEP = 2
SCORE_SLOTS = 3
PREPARE_AT = 6


def _sigmoid(v):
    return 0.5 * jnp.tanh(0.5 * v) + 0.5


def _silu(v):
    return v * _sigmoid(v)


def _split_bf16(v):
    parts = []
    for _ in range(BIAS_TERMS):
        part = v.astype(BF16).astype(F32)
        parts.append(part)
        v = v - part
    return parts


def _mod_kernel(c_ref, w_ref, b_ref, o_ref):
    s = _silu(c_ref[...]).astype(BF16)
    w = w_ref[0].astype(BF16)
    o_ref[0] = jnp.dot(s, w, preferred_element_type=F32) + b_ref[0]


def _mod_call(c, ada_w, ada_b):
    n_layers, d, d3 = ada_w.shape
    b = c.shape[0]
    tn = 1024
    return pl.pallas_call(
        _mod_kernel,
        grid=(n_layers, d3 // tn),
        in_specs=[
            pl.BlockSpec((b, d), lambda l, j: (0, 0)),
            pl.BlockSpec((1, d, tn), lambda l, j: (l, 0, j)),
            pl.BlockSpec((1, 1, tn), lambda l, j: (l, 0, j)),
        ],
        out_specs=pl.BlockSpec((1, b, tn), lambda l, j: (l, 0, j)),
        out_shape=jax.ShapeDtypeStruct((n_layers, b, d3), F32),
        compiler_params=pltpu.CompilerParams(
            dimension_semantics=("arbitrary", "arbitrary"), vmem_limit_bytes=VMEM_LIMIT_BYTES),
        name="adaln_mod",
    )(c, ada_w, ada_b.reshape(n_layers, 1, d3))


def _cast_weight_once(w_ref, wbf_scr):
    rows = w_ref.shape[1]

    @pl.when((pl.program_id(0) == 0) & (pl.program_id(1) == 0))
    def _():
        def body(i, carry):
            r0 = pl.multiple_of(i * WEIGHT_CAST_ROWS, WEIGHT_CAST_ROWS)
            wbf_scr[pl.ds(r0, WEIGHT_CAST_ROWS), :] = w_ref[0, pl.ds(r0, WEIGHT_CAST_ROWS), :].astype(BF16)
            return carry
        lax.fori_loop(0, rows // WEIGHT_CAST_ROWS, body, 0)


def _inproj_math(x, mod_ref, nw_ref, w_ref, out_refs):
    ms = jnp.mean(x * x, axis=-1, keepdims=True)
    y = x * lax.rsqrt(ms + EPS) * nw_ref[...]
    shift = mod_ref[0, 0:1, :]
    scale = mod_ref[0, 1:2, :]
    h = (y * (1.0 + scale) + shift).astype(BF16)
    for j, ref in enumerate(out_refs):
        r = jnp.dot(h, w_ref[:, j * D_ATTN:(j + 1) * D_ATTN], preferred_element_type=F32)
        if j == 0:
            r = r * (HEAD_DIM ** -0.5 * LOG2E)
        if j in FEATURE_MAJOR_OUTPUTS:
            r = r.T
        ref[0] = r.astype(ref.dtype)


def _inproj_kernel(x_ref, mod_ref, nw_ref, wf_ref, q_ref, k_ref, v_ref, za_ref, xl_ref, zl_ref,
                   w_ref):
    _cast_weight_once(wf_ref, w_ref)
    _inproj_math(x_ref[0], mod_ref, nw_ref, w_ref, (q_ref, k_ref, v_ref, za_ref, xl_ref, zl_ref))


def _inproj_outputs(b, s, tm=ROW_TILE):
    token_major = pl.BlockSpec((1, tm, D_ATTN), lambda i, j: (i, j, 0))
    feature_major = pl.BlockSpec((1, D_ATTN, tm), lambda i, j: (i, 0, j))
    dtypes = (BF16, BF16, BF16, BF16, F32, BF16)
    specs, shapes = [], []
    for j, dt in enumerate(dtypes):
        fm = j in FEATURE_MAJOR_OUTPUTS
        specs.append(feature_major if fm else token_major)
        shapes.append(jax.ShapeDtypeStruct((b, D_ATTN, s) if fm else (b, s, D_ATTN), dt))
    return specs, shapes


def _inproj_call(x, mod3, norm_w, w_in, layer):
    b, s, d = x.shape
    tm = IN_ROW_TILE
    out_specs, out_shape = _inproj_outputs(b, s, tm)
    return pl.pallas_call(
        _inproj_kernel,
        grid=(b, s // tm),
        in_specs=[
            pl.BlockSpec((1, tm, d), lambda i, j: (i, j, 0)),
            pl.BlockSpec((1, 3, d), lambda i, j: (i, 0, 0)),
            pl.BlockSpec((1, d), lambda i, j: (0, 0)),
            pl.BlockSpec((1, d, D_IN), lambda i, j: (layer, 0, 0), pipeline_mode=pl.Buffered(1)),
        ],
        out_specs=out_specs,
        out_shape=out_shape,
        scratch_shapes=[pltpu.VMEM((d, D_IN), BF16)],
        compiler_params=pltpu.CompilerParams(
            dimension_semantics=("arbitrary", "arbitrary"), vmem_limit_bytes=VMEM_LIMIT_BYTES),
        name="inproj",
    )(x, mod3, norm_w.reshape(1, d), w_in)


def _mixer_kernel(slopes_ref, q_ref, k_ref, v_ref, kext_ref, *rest):
    n_slabs = k_ref.shape[2] // LANES
    xl_refs = rest[:n_slabs]
    (cw_ref, cb_ref, wg_ref, ab_ref, xb_ref, lam_ref, o_ref, ol_ref,
     kaug_scr, qaug_scr, vaug_scr, s_scr, p_scr, out_scr,
     head_scr, h_scr, xc_scr, g_scr) = rest[n_slabs:]
    s_len = k_ref.shape[1]
    n_blk = s_len // MOBA_BLOCK
    n_pairs = k_ref.shape[2] // LANES
    n_heads = n_pairs * HEADS_PER_PAIR
    first_head = pl.program_id(0) * n_heads

    prow = lax.broadcasted_iota(jnp.int32, (n_blk, s_len), 0)
    pcol = lax.broadcasted_iota(jnp.int32, (n_blk, s_len), 1)
    block_mean = jnp.where(pcol // MOBA_BLOCK == prow, 1.0 / MOBA_BLOCK, 0.0).astype(BF16)
    klane = lax.broadcasted_iota(jnp.int32, (s_len, LANES), 1)
    mlane = lax.broadcasted_iota(jnp.int32, (n_blk, LANES), 1)
    n_iota = prow
    q_blk = pcol // MOBA_BLOCK
    key_off = lax.broadcasted_iota(jnp.int32, (MOBA_BLOCK, MOBA_BLOCK), 0)
    qry_off = lax.broadcasted_iota(jnp.int32, (MOBA_BLOCK, MOBA_BLOCK), 1)
    causal = key_off <= qry_off

    def prepare(pp):
        lanes = slice(pp * LANES, (pp + 1) * LANES)
        kp = k_ref[0, :, lanes]
        q_t = q_ref[0, lanes, :]
        v_t = v_ref[0, lanes, :]
        kmean = jnp.dot(block_mean, kp, preferred_element_type=F32)
        terms = []
        for hd in range(HEADS_PER_PAIR):
            rest = jnp.where(mlane // HEAD_DIM == hd, kmean, 0.0)
            for _ in range(3):
                part = rest.astype(BF16).astype(F32)
                terms.append(part)
                rest = rest - part
        terms = jnp.concatenate(terms, axis=0).astype(BF16)
        half = s_len // 2
        gates = jnp.concatenate(
            [jnp.dot(terms, q_t[:, :half], preferred_element_type=F32),
             jnp.dot(terms, q_t[:, half:], preferred_element_type=F32)], axis=1)
        for hd in range(HEADS_PER_PAIR):
            h = pp * HEADS_PER_PAIR + hd
            kaug_scr[h] = jnp.where(klane // HEAD_DIM == hd, kp, kext_ref[h])
            slope2 = slopes_ref[first_head + h] * LOG2E
            g_hi, g_mid, g_lo = (gates[(3 * hd + t) * n_blk:(3 * hd + t + 1) * n_blk]
                                 for t in range(3))
            g_t = g_hi + g_mid + g_lo
            rank = jnp.zeros((n_blk, s_len), jnp.int32)
            for m in range(n_blk):
                gm = g_t[m:m + 1, :]
                beats = (gm > g_t) | ((gm == g_t) & (m < n_iota))
                rank = rank + jnp.where(beats & (m < q_blk), 1, 0)
            sel = ((n_iota < q_blk) & (rank < MOBA_TOPK)) | (n_iota == q_blk)
            bias = _split_bf16(slope2 * ((n_iota - q_blk) * MOBA_BLOCK).astype(F32))
            selb = [jnp.where(sel, part, NEG_BIG if t == 0 else 0.0)
                    for t, part in enumerate(bias)]
            flag = jnp.where(n_iota < BIAS_TERMS, 1.0, 0.0)
            pad = jnp.zeros((HEAD_DIM - (BIAS_TERMS + 1) * n_blk, s_len), F32)
            ext = jnp.concatenate(selb + [flag, pad], axis=0)
            own = slice(hd * HEAD_DIM, (hd + 1) * HEAD_DIM)
            other = slice((1 - hd) * HEAD_DIM, (2 - hd) * HEAD_DIM)
            qaug_scr[h, own, :] = q_t[own]
            qaug_scr[h, other, :] = ext.astype(BF16)
            vaug_scr[h, 0:HEAD_DIM, :] = v_t[own]
            vaug_scr[h, HEAD_DIM:, :] = jnp.ones((2 * SUBLANES, s_len), BF16)

    items = [(h, qb) for h in range(n_heads)
             for qb in (range(n_blk) if h % 2 == 0 else reversed(range(n_blk)))]
    per_pair = HEADS_PER_PAIR * n_blk

    def scores(i):
        h, qb = items[i]
        slot = i % SCORE_SLOTS
        q0, kk = qb * MOBA_BLOCK, (qb + 1) * MOBA_BLOCK
        qa = qaug_scr[h, :, q0:kk]
        m = None
        for r0 in range(0, kk, MOBA_BLOCK):
            s = jnp.dot(kaug_scr[h, r0:r0 + MOBA_BLOCK, :], qa, preferred_element_type=F32)
            if r0 == q0:
                s = jnp.where(causal, s, NEG_BIG)
            s_scr[slot, r0:r0 + MOBA_BLOCK, :] = s
            mh = jnp.max(s.reshape(MOBA_BLOCK // SUBLANES, SUBLANES, MOBA_BLOCK), axis=0)
            m = mh if m is None else jnp.maximum(m, mh)
        return jnp.max(m, axis=0, keepdims=True)

    def probs(i, m):
        _, qb = items[i]
        kk = (qb + 1) * MOBA_BLOCK
        p_scr[i % 2, 0:kk, :] = jnp.exp2(s_scr[i % SCORE_SLOTS, 0:kk, :] - m).astype(BF16)

    def weighted(i):
        h, qb = items[i]
        slot = i % 2
        q0, kk = qb * MOBA_BLOCK, (qb + 1) * MOBA_BLOCK
        o = None
        for r0, r1 in ((0, kk // 2), (kk // 2, kk)):
            part = jnp.dot(vaug_scr[h, :, r0:r1], p_scr[slot, r0:r1, :], preferred_element_type=F32)
            o = part if o is None else o + part
        out_scr[h * HEAD_DIM:(h + 1) * HEAD_DIM, q0:kk] = o[0:HEAD_DIM] / o[HEAD_DIM:HEAD_DIM + 1]

    lru_units = []
    for sl in range(n_slabs):
        lanes = slice(sl * LANES, (sl + 1) * LANES)

        def store_h(value, lanes=lanes):
            ol_ref[0, :, lanes] = value

        lru_units += _lru_units(
            xl_refs[sl].at[0], cw_ref[:, lanes], cb_ref[:, lanes], wg_ref.at[sl], ab_ref[:, lanes],
            xb_ref[:, lanes], lam_ref[:, lanes], head_scr.at[sl], h_scr.at[sl], xc_scr.at[sl],
            g_scr.at[sl], store_h)
    lru_done = 0

    n_items = len(items)
    prepare(0)
    col_max = {i: scores(i) for i in range(SCORE_SLOTS)}
    probs(0, col_max.pop(0))
    for i in range(n_items):
        pp, within = divmod(i, per_pair)
        if within == PREPARE_AT and pp + 1 < n_pairs:
            prepare(pp + 1)
        if i + SCORE_SLOTS < n_items:
            col_max[i + SCORE_SLOTS] = scores(i + SCORE_SLOTS)
        if i + 1 < n_items:
            probs(i + 1, col_max.pop(i + 1))
        weighted(i)
        if within == per_pair - 1:
            lanes = slice(pp * LANES, (pp + 1) * LANES)
            o_ref[0, :, lanes] = out_scr[lanes, :].T.astype(o_ref.dtype)
        lru_target = (i + 1) * len(lru_units) // n_items
        while lru_done < lru_target:
            lru_units[lru_done]()
            lru_done += 1


def _alibi_slopes():
    return np.exp2(-8.0 * (np.arange(N_HEADS, dtype=np.float32) + 1.0) / N_HEADS).astype(np.float32)


def _key_side_table(s_len):
    n_blk = s_len // MOBA_BLOCK
    head = np.arange(N_HEADS)[:, None, None]
    pos = np.arange(s_len)[None, :, None]
    lane = np.arange(LANES)[None, None, :]
    sub = lane % HEAD_DIM
    foreign = lane // HEAD_DIM != head % HEADS_PER_PAIR
    onehot = (sub % n_blk == pos // MOBA_BLOCK).astype(np.float32)
    off = ((_alibi_slopes() * np.float32(LOG2E))[:, None, None]
           * (pos % MOBA_BLOCK).astype(np.float32)).astype(np.float32)
    ext = np.where(sub < BIAS_TERMS * n_blk, onehot, 0.0).astype(np.float32)
    rest = off
    for t in range(BIAS_TERMS):
        part = (rest.view(np.uint32) & np.uint32(0xFFFF0000)).view(np.float32) if t + 1 < BIAS_TERMS else rest
        ext = np.where(sub == BIAS_TERMS * n_blk + t, part, ext)
        rest = rest - part
    table = np.where(foreign, ext, 0.0).astype(np.float32)
    return jnp.asarray(table.astype(BF16))


def _mixer_call(slopes, kext, q_t, k, v_t, xl, conv_w, conv_b, w_gate, a_b, x_b, lam):
    b, s, _ = k.shape
    pairs = ATTN_PAIRS_PER_STEP
    width = pairs * LANES
    heads = pairs * HEADS_PER_PAIR
    row = lambda a: a.reshape(1, D_LRU)
    spec = pl.BlockSpec((1, s, width), lambda j, i, sl: (i, 0, j))
    spec_t = pl.BlockSpec((1, width, s), lambda j, i, sl: (i, j, 0))
    vec = pl.BlockSpec((1, width), lambda j, i, sl: (0, j))
    slab_specs = [pl.BlockSpec((1, s, LANES), lambda j, i, sl, n=n: (i, 0, j * pairs + n))
                  for n in range(pairs)]
    sub_len = LRU_SEGMENT * SUBLANES
    return pl.pallas_call(
        _mixer_kernel,
        grid_spec=pltpu.PrefetchScalarGridSpec(
            num_scalar_prefetch=1,
            grid=(D_ATTN // width, b),
            in_specs=[spec_t, spec, spec_t,
                      pl.BlockSpec((heads, s, LANES), lambda j, i, sl: (j, 0, 0),
                                   pipeline_mode=pl.Buffered(1))]
            + slab_specs
            + [pl.BlockSpec((CONV_WIDTH, width), lambda j, i, sl: (0, j)), vec,
               pl.BlockSpec((pairs, LANES, 2 * LANES), lambda j, i, sl: (j, 0, 0)), vec, vec, vec],
            out_specs=[spec, spec],
            scratch_shapes=[
                pltpu.VMEM((heads, s, LANES), BF16),
                pltpu.VMEM((heads, LANES, s), BF16),
                pltpu.VMEM((heads, HEAD_DIM + 2 * SUBLANES, s), BF16),
                pltpu.VMEM((SCORE_SLOTS, s, MOBA_BLOCK), F32),
                pltpu.VMEM((2, s, MOBA_BLOCK), BF16),
                pltpu.VMEM((width, s), F32),
                pltpu.VMEM((pairs, SUBLANES + sub_len, LANES), F32),
                pltpu.VMEM((pairs, s, LANES), F32),
                pltpu.VMEM((pairs, 2, LRU_CHUNK, LANES), F32),
                pltpu.VMEM((pairs, 2, LRU_CHUNK, 2 * LANES), F32),
            ],
        ),
        out_shape=[jax.ShapeDtypeStruct((b, s, D_ATTN), BF16), jax.ShapeDtypeStruct((b, s, D_LRU), BF16)],
        compiler_params=pltpu.CompilerParams(
            dimension_semantics=("arbitrary", "arbitrary"), vmem_limit_bytes=VMEM_LIMIT_BYTES),
        name="mixer",
    )(slopes, q_t, k, v_t, kext, *([xl] * pairs), conv_w, row(conv_b), w_gate, row(a_b), row(x_b),
      row(lam))


def _lru_units(x_seq, cw, cb, w_gate, a_b, x_b, lam, head_scr, h_scr, xc_scr, g_scr, store_h):
    s_len = x_seq.shape[0]
    seg = LRU_SEGMENT
    sub_len = seg * SUBLANES
    n_sub = LRU_CHUNK // sub_len
    n_chunks = s_len // LRU_CHUNK

    neg_lam = -lam
    softplus = jnp.maximum(neg_lam, 0.0) + jnp.log1p(jnp.exp(-jnp.abs(neg_lam)))
    half_rate = (0.5 * RG_C) * softplus
    rate_log2 = -LOG2E * half_rate
    ab_half = 0.5 * a_b
    xb_half = 0.5 * x_b
    cw_half = 0.5 * cw
    cb_half = 0.5 * cb
    sub = lax.broadcasted_iota(jnp.int32, (SUBLANES, LANES), 0)
    steps = (1, 2, 4)
    in_vreg = [sub >= step for step in steps]
    state = {"h": jnp.zeros((SUBLANES, LANES), F32)}

    def strided(ref, start):
        return ref[pl.ds(start, SUBLANES, stride=seg), :]

    def init():
        head_scr[0:SUBLANES, :] = jnp.zeros((SUBLANES, LANES), F32)
        head_scr[SUBLANES:, :] = x_seq[0:sub_len, :]

    def conv_and_gates(ci):
        xc_parts = []
        for c in range(n_sub):
            if ci == 0 and c == 0:
                src, t0 = head_scr, SUBLANES
            else:
                src, t0 = x_seq, ci * LRU_CHUNK
            cache = {}
            for g in range(seg):
                acc = cb_half
                for j in range(CONV_WIDTH):
                    off = c * sub_len + g - (CONV_WIDTH - 1) + j
                    if off not in cache:
                        cache[off] = strided(src, t0 + off)
                    acc = acc + cw_half[j:j + 1, :] * cache[off]
                xc_parts.append(acc)
        hx = jnp.concatenate(xc_parts, axis=0)
        xc_scr[ci % 2] = hx
        g_scr[ci % 2] = jnp.dot(hx.astype(BF16), w_gate[...], preferred_element_type=F32)

    def recurrence(ci, c):
        t0 = ci * LRU_CHUNK
        rows = slice(c * sub_len, (c + 1) * sub_len)
        h_in = state["h"]
        hx = xc_scr[ci % 2, rows, :]
        t_r = jnp.tanh(g_scr[ci % 2, rows, :LANES] + ab_half)
        t_i = jnp.tanh(g_scr[ci % 2, rows, LANES:] + xb_half)
        two_r = t_r + 1.0
        neg_log_a = half_rate * two_r
        a = jnp.exp2(rate_log2 * two_r)
        y = jnp.tanh(neg_log_a) * (a * a + 1.0)
        root = jnp.where(y == 0.0, 0.0, y * lax.rsqrt(y))
        bterm = root * ((t_i + 1.0) * hx)

        comp = []
        for g in range(seg):
            av, bv = a[g * SUBLANES:(g + 1) * SUBLANES], bterm[g * SUBLANES:(g + 1) * SUBLANES]
            if g > 0:
                bv = av * comp[-1][1] + bv
                av = av * comp[-1][0]
            comp.append((av, bv))
        pa, pb = comp[-1]
        for step, valid in zip(steps, in_vreg):
            a_sh = jnp.where(valid, pltpu.roll(pa, step, axis=0), 1.0)
            b_sh = jnp.where(valid, pltpu.roll(pb, step, axis=0), 0.0)
            pb = pa * b_sh + pb
            pa = pa * a_sh
        ea = jnp.where(in_vreg[0], pltpu.roll(pa, 1, axis=0), 1.0)
        eb = jnp.where(in_vreg[0], pltpu.roll(pb, 1, axis=0), 0.0)
        seg_in = ea * h_in + eb
        for g in range(seg):
            h_g = comp[g][0] * seg_in + comp[g][1]
            h_scr[pl.ds(t0 + c * sub_len + g, SUBLANES, stride=seg), :] = h_g
        h_last = pa * h_in + pb
        state["h"] = jnp.broadcast_to(h_last[SUBLANES - 1:SUBLANES, :], (SUBLANES, LANES))

    def finish():
        store_h(h_scr[...].astype(BF16))

    units = [init, lambda: conv_and_gates(0)]
    for ci in range(n_chunks):
        for c in range(n_sub):
            units.append(lambda ci=ci, c=c: recurrence(ci, c))
            if c == n_sub // 2 - 1 and ci + 1 < n_chunks:
                units.append(lambda ci=ci: conv_and_gates(ci + 1))
    units.append(finish)
    return units


def _gated_norm(y_ref, z_ref, nw_ref):
    y = y_ref[0].astype(F32)
    ms = jnp.mean(y * y, axis=-1, keepdims=True)
    return ((y * lax.rsqrt(ms + EPS) * nw_ref[...]) * _silu(z_ref[0].astype(F32))).astype(BF16)


def _outproj_math(ya_ref, za_ref, yl_ref, zl_ref, x_ref, mod_ref, anw_ref, lnw_ref, w_ref):
    y = jnp.dot(_gated_norm(ya_ref, za_ref, anw_ref), w_ref[0:D_ATTN, :], preferred_element_type=F32)
    y = y + jnp.dot(_gated_norm(yl_ref, zl_ref, lnw_ref), w_ref[D_ATTN:, :],
                    preferred_element_type=F32)
    return x_ref[0] + mod_ref[0, 2:3, :] * y


def _outproj_kernel(ya_ref, za_ref, yl_ref, zl_ref, x_ref, mod_ref, anw_ref, lnw_ref, wf_ref, fnw_ref,
                    o_ref, w_ref):
    _cast_weight_once(wf_ref, w_ref)
    out = _outproj_math(ya_ref, za_ref, yl_ref, zl_ref, x_ref, mod_ref, anw_ref, lnw_ref, w_ref)
    ms = jnp.mean(out * out, axis=-1, keepdims=True)
    o_ref[0] = out * lax.rsqrt(ms + EPS) * fnw_ref[...]


def _outin_kernel(ya_ref, za_ref, yl_ref, zl_ref, x_ref, mod_ref, anw_ref, lnw_ref, wof_ref,
                  modn_ref, nwn_ref, wif_ref,
                  xo_ref, q_ref, k_ref, v_ref, zao_ref, xlo_ref, zlo_ref, wo_ref, wi_ref):
    _cast_weight_once(wof_ref, wo_ref)
    _cast_weight_once(wif_ref, wi_ref)
    out = _outproj_math(ya_ref, za_ref, yl_ref, zl_ref, x_ref, mod_ref, anw_ref, lnw_ref, wo_ref)
    xo_ref[0] = out
    _inproj_math(out, modn_ref, nwn_ref, wi_ref, (q_ref, k_ref, v_ref, zao_ref, xlo_ref, zlo_ref))


def _proj_specs(d, layer, tm=ROW_TILE):
    return dict(
        half=pl.BlockSpec((1, tm, D_ATTN), lambda i, j: (i, j, 0)),
        full=pl.BlockSpec((1, tm, d), lambda i, j: (i, j, 0)),
        mod=pl.BlockSpec((1, 3, d), lambda i, j: (i, 0, 0)),
        nvec=pl.BlockSpec((1, D_ATTN), lambda i, j: (0, 0)),
        dvec=pl.BlockSpec((1, d), lambda i, j: (0, 0)),
        w_out=pl.BlockSpec((1, d, d), lambda i, j: (layer, 0, 0), pipeline_mode=pl.Buffered(1)),
        w_in=pl.BlockSpec((1, d, D_IN), lambda i, j: (layer + 1, 0, 0), pipeline_mode=pl.Buffered(1)),
    )


def _outproj_call(ya, za, yl, zl, x, mod3, anw, lnw, w_out, layer, fnw):
    b, s, d = x.shape
    sp = _proj_specs(d, layer, OUT_ROW_TILE)
    return pl.pallas_call(
        _outproj_kernel,
        grid=(b, s // OUT_ROW_TILE),
        in_specs=[sp["half"]] * 4 + [sp["full"], sp["mod"], sp["nvec"], sp["nvec"], sp["w_out"],
                                     sp["dvec"]],
        out_specs=sp["full"],
        out_shape=jax.ShapeDtypeStruct((b, s, d), F32),
        scratch_shapes=[pltpu.VMEM((d, d), BF16)],
        compiler_params=pltpu.CompilerParams(
            dimension_semantics=("arbitrary", "arbitrary"), vmem_limit_bytes=VMEM_LIMIT_BYTES),
        name="outproj",
    )(ya, za, yl, zl, x, mod3, anw.reshape(1, D_ATTN), lnw.reshape(1, D_LRU), w_out,
      fnw.reshape(1, d))


def _outin_call(ya, za, yl, zl, x, mod3, anw, lnw, w_out, layer, mod3_next, norm_w_next, w_in):
    b, s, d = x.shape
    sp = _proj_specs(d, layer)
    proj_specs, proj_shapes = _inproj_outputs(b, s)
    return pl.pallas_call(
        _outin_kernel,
        grid=(b, s // ROW_TILE),
        in_specs=[sp["half"]] * 4 + [sp["full"], sp["mod"], sp["nvec"], sp["nvec"], sp["w_out"],
                                     sp["mod"], sp["dvec"], sp["w_in"]],
        out_specs=[sp["full"]] + proj_specs,
        out_shape=[jax.ShapeDtypeStruct((b, s, d), F32)] + proj_shapes,
        scratch_shapes=[pltpu.VMEM((d, d), BF16), pltpu.VMEM((d, D_IN), BF16)],
        compiler_params=pltpu.CompilerParams(
            dimension_semantics=("arbitrary", "arbitrary"), vmem_limit_bytes=PROJ_VMEM_LIMIT_BYTES),
        name="outproj_inproj",
    )(ya, za, yl, zl, x, mod3, anw.reshape(1, D_ATTN), lnw.reshape(1, D_LRU), w_out,
      mod3_next, norm_w_next.reshape(1, d), w_in)


def _gate_weights(a_w, x_w):
    per_slab = LANES // LRU_BLOCK
    slabs = []
    for j in range(D_LRU // LANES):
        blocks = range(j * per_slab, (j + 1) * per_slab)
        diag = lambda w: jax.scipy.linalg.block_diag(*[w[g] for g in blocks])
        slabs.append(jnp.concatenate([diag(a_w), diag(x_w)], axis=1))
    return jnp.stack(slabs).astype(BF16)


def kernel(x, c, norm_w, ada_w, ada_b, w_in, conv_w, conv_b, rg_a_w, rg_a_b, rg_x_w, rg_x_b,
           rg_lambda, attn_out_norm_w, lru_out_norm_w, w_out, final_norm_w):
    n_layers = ada_w.shape[0]
    b = x.shape[0]
    mod = _mod_call(c, ada_w, ada_b)
    slopes = jnp.asarray(_alibi_slopes())
    kext = _key_side_table(x.shape[1])
    mod3 = [mod[l].reshape(b, 3, D_MODEL) for l in range(n_layers)]
    q, k, v, za, xl, zl = _inproj_call(x, mod3[0], norm_w[0], w_in, 0)
    for l in range(n_layers):
        ya, yl = _mixer_call(slopes, kext, q, k, v, xl, conv_w[l], conv_b[l],
                             _gate_weights(rg_a_w[l], rg_x_w[l]), rg_a_b[l], rg_x_b[l], rg_lambda[l])
        if l + 1 < n_layers:
            x, q, k, v, za, xl, zl = _outin_call(
                ya, za, yl, zl, x, mod3[l], attn_out_norm_w[l], lru_out_norm_w[l], w_out, l,
                mod3[l + 1], norm_w[l + 1], w_in)
        else:
            x = _outproj_call(ya, za, yl, zl, x, mod3[l], attn_out_norm_w[l], lru_out_norm_w[l],
                              w_out, l, final_norm_w)
    return x
```

```python
import math

import jax
import jax.numpy as jnp
import numpy as np
from jax import lax
from jax.experimental import pallas as pl
from jax.experimental.pallas import tpu as pltpu

D_MODEL = 1024
D_ATTN = 512
D_LRU = 512
HEAD_DIM = 64
N_HEADS = D_ATTN // HEAD_DIM
N_LRU_BLOCKS = 8
LRU_BLOCK = D_LRU // N_LRU_BLOCKS
CONV_WIDTH = 4
RG_C = 8.0
MOBA_BLOCK = 256
MOBA_TOPK = 3
D_IN = 4 * D_ATTN + 2 * D_LRU
EPS = 1e-6

F32 = jnp.float32
BF16 = jnp.bfloat16
NEG_BIG = -1e30
LOG2E = math.log2(math.e)

LANES = 128
SUBLANES = 8
VMEM_LIMIT_BYTES = 48 * 1024 * 1024
PROJ_VMEM_LIMIT_BYTES = 56 * 1024 * 1024

ROW_TILE = 512
IN_ROW_TILE = 1024
OUT_ROW_TILE = 1024
WEIGHT_CAST_ROWS = 64
LRU_CHUNK = 256
LRU_SEGMENT = 4
HEADS_PER_PAIR = LANES // HEAD_DIM
BIAS_TERMS = 3
FEATURE_MAJOR_OUTPUTS = (0, 2)
ATTN_PAIRS_PER_STEP = 2
SCORE_SLOTS = 3
PREPARE_AT = 6


def _sigmoid(v):
    return 0.5 * jnp.tanh(0.5 * v) + 0.5


def _silu(v):
    return v * _sigmoid(v)


def _split_bf16(v):
    parts = []
    for _ in range(BIAS_TERMS):
        part = v.astype(BF16).astype(F32)
        parts.append(part)
        v = v - part
    return parts


def _mod_kernel(c_ref, w_ref, b_ref, o_ref):
    s = _silu(c_ref[...]).astype(BF16)
    w = w_ref[0].astype(BF16)
    o_ref[0] = jnp.dot(s, w, preferred_element_type=F32) + b_ref[0]


def _mod_call(c, ada_w, ada_b):
    n_layers, d, d3 = ada_w.shape
    b = c.shape[0]
    tn = 1024
    return pl.pallas_call(
        _mod_kernel,
        grid=(n_layers, d3 // tn),
        in_specs=[
            pl.BlockSpec((b, d), lambda l, j: (0, 0)),
            pl.BlockSpec((1, d, tn), lambda l, j: (l, 0, j)),
            pl.BlockSpec((1, 1, tn), lambda l, j: (l, 0, j)),
        ],
        out_specs=pl.BlockSpec((1, b, tn), lambda l, j: (l, 0, j)),
        out_shape=jax.ShapeDtypeStruct((n_layers, b, d3), F32),
        compiler_params=pltpu.CompilerParams(
            dimension_semantics=("arbitrary", "arbitrary"), vmem_limit_bytes=VMEM_LIMIT_BYTES),
        name="adaln_mod",
    )(c, ada_w, ada_b.reshape(n_layers, 1, d3))


def _cast_weight_once(w_ref, wbf_scr):
    rows = w_ref.shape[1]

    @pl.when((pl.program_id(0) == 0) & (pl.program_id(1) == 0))
    def _():
        def body(i, carry):
            r0 = pl.multiple_of(i * WEIGHT_CAST_ROWS, WEIGHT_CAST_ROWS)
            wbf_scr[pl.ds(r0, WEIGHT_CAST_ROWS), :] = w_ref[0, pl.ds(r0, WEIGHT_CAST_ROWS), :].astype(BF16)
            return carry
        lax.fori_loop(0, rows // WEIGHT_CAST_ROWS, body, 0)


def _inproj_math(x, mod_ref, nw_ref, w_ref, out_refs):
    ms = jnp.mean(x * x, axis=-1, keepdims=True)
    y = x * lax.rsqrt(ms + EPS) * nw_ref[...]
    shift = mod_ref[0, 0:1, :]
    scale = mod_ref[0, 1:2, :]
    h = (y * (1.0 + scale) + shift).astype(BF16)
    for j, ref in enumerate(out_refs):
        r = jnp.dot(h, w_ref[:, j * D_ATTN:(j + 1) * D_ATTN], preferred_element_type=F32)
        if j == 0:
            r = r * (HEAD_DIM ** -0.5 * LOG2E)
        if j in FEATURE_MAJOR_OUTPUTS:
            r = r.T
        ref[0] = r.astype(ref.dtype)


def _inproj_kernel(x_ref, mod_ref, nw_ref, wf_ref, q_ref, k_ref, v_ref, za_ref, xl_ref, zl_ref,
                   w_ref):
    _cast_weight_once(wf_ref, w_ref)
    _inproj_math(x_ref[0], mod_ref, nw_ref, w_ref, (q_ref, k_ref, v_ref, za_ref, xl_ref, zl_ref))


def _inproj_outputs(b, s, tm=ROW_TILE):
    token_major = pl.BlockSpec((1, tm, D_ATTN), lambda i, j: (i, j, 0))
    feature_major = pl.BlockSpec((1, D_ATTN, tm), lambda i, j: (i, 0, j))
    dtypes = (BF16, BF16, BF16, BF16, F32, BF16)
    specs, shapes = [], []
    for j, dt in enumerate(dtypes):
        fm = j in FEATURE_MAJOR_OUTPUTS
        specs.append(feature_major if fm else token_major)
        shapes.append(jax.ShapeDtypeStruct((b, D_ATTN, s) if fm else (b, s, D_ATTN), dt))
    return specs, shapes


def _inproj_call(x, mod3, norm_w, w_in, layer):
    b, s, d = x.shape
    tm = IN_ROW_TILE
    out_specs, out_shape = _inproj_outputs(b, s, tm)
    return pl.pallas_call(
        _inproj_kernel,
        grid=(b, s // tm),
        in_specs=[
            pl.BlockSpec((1, tm, d), lambda i, j: (i, j, 0)),
            pl.BlockSpec((1, 3, d), lambda i, j: (i, 0, 0)),
            pl.BlockSpec((1, d), lambda i, j: (0, 0)),
            pl.BlockSpec((1, d, D_IN), lambda i, j: (layer, 0, 0), pipeline_mode=pl.Buffered(1)),
        ],
        out_specs=out_specs,
        out_shape=out_shape,
        scratch_shapes=[pltpu.VMEM((d, D_IN), BF16)],
        compiler_params=pltpu.CompilerParams(
            dimension_semantics=("arbitrary", "arbitrary"), vmem_limit_bytes=VMEM_LIMIT_BYTES),
        name="inproj",
    )(x, mod3, norm_w.reshape(1, d), w_in)


def _mixer_kernel(slopes_ref, q_ref, k_ref, v_ref, kext_ref, *rest):
    n_slabs = k_ref.shape[2] // LANES
    xl_refs = rest[:n_slabs]
    (cw_ref, cb_ref, wg_ref, ab_ref, xb_ref, lam_ref, o_ref, ol_ref,
     kaug_scr, qaug_scr, vaug_scr, s_scr, p_scr, out_scr,
     head_scr, h_scr, xc_scr, g_scr) = rest[n_slabs:]
    s_len = k_ref.shape[1]
    n_blk = s_len // MOBA_BLOCK
    n_pairs = k_ref.shape[2] // LANES
    n_heads = n_pairs * HEADS_PER_PAIR
    first_head = pl.program_id(0) * n_heads

    prow = lax.broadcasted_iota(jnp.int32, (n_blk, s_len), 0)
    pcol = lax.broadcasted_iota(jnp.int32, (n_blk, s_len), 1)
    block_mean = jnp.where(pcol // MOBA_BLOCK == prow, 1.0 / MOBA_BLOCK, 0.0).astype(BF16)
    klane = lax.broadcasted_iota(jnp.int32, (s_len, LANES), 1)
    mlane = lax.broadcasted_iota(jnp.int32, (n_blk, LANES), 1)
    n_iota = prow
    q_blk = pcol // MOBA_BLOCK
    key_off = lax.broadcasted_iota(jnp.int32, (MOBA_BLOCK, MOBA_BLOCK), 0)
    qry_off = lax.broadcasted_iota(jnp.int32, (MOBA_BLOCK, MOBA_BLOCK), 1)
    causal = key_off <= qry_off

    def prepare(pp):
        lanes = slice(pp * LANES, (pp + 1) * LANES)
        kp = k_ref[0, :, lanes]
        q_t = q_ref[0, lanes, :]
        v_t = v_ref[0, lanes, :]
        kmean = jnp.dot(block_mean, kp, preferred_element_type=F32)
        terms = []
        for hd in range(HEADS_PER_PAIR):
            rest = jnp.where(mlane // HEAD_DIM == hd, kmean, 0.0)
            for _ in range(3):
                part = rest.astype(BF16).astype(F32)
                terms.append(part)
                rest = rest - part
        terms = jnp.concatenate(terms, axis=0).astype(BF16)
        half = s_len // 2
        gates = jnp.concatenate(
            [jnp.dot(terms, q_t[:, :half], preferred_element_type=F32),
             jnp.dot(terms, q_t[:, half:], preferred_element_type=F32)], axis=1)
        for hd in range(HEADS_PER_PAIR):
            h = pp * HEADS_PER_PAIR + hd
            kaug_scr[h] = jnp.where(klane // HEAD_DIM == hd, kp, kext_ref[h])
            slope2 = slopes_ref[first_head + h] * LOG2E
            g_hi, g_mid, g_lo = (gates[(3 * hd + t) * n_blk:(3 * hd + t + 1) * n_blk]
                                 for t in range(3))
            g_t = g_hi + g_mid + g_lo
            rank = jnp.zeros((n_blk, s_len), jnp.int32)
            for m in range(n_blk - 1):
                gm = g_t[m:m + 1, :]
                beats = (gm > g_t) | ((gm == g_t) & (m < n_iota))
                rank = rank + jnp.where(beats & (m < q_blk), 1, 0)
            sel = ((n_iota < q_blk) & (rank < MOBA_TOPK)) | (n_iota == q_blk)
            bias = _split_bf16(slope2 * ((n_iota - q_blk) * MOBA_BLOCK).astype(F32))
            selb = [jnp.where(sel, part, NEG_BIG if t == 0 else 0.0)
                    for t, part in enumerate(bias)]
            flag = jnp.where(n_iota < BIAS_TERMS, 1.0, 0.0)
            pad = jnp.zeros((HEAD_DIM - (BIAS_TERMS + 1) * n_blk, s_len), F32)
            ext = jnp.concatenate(selb + [flag, pad], axis=0)
            own = slice(hd * HEAD_DIM, (hd + 1) * HEAD_DIM)
            other = slice((1 - hd) * HEAD_DIM, (2 - hd) * HEAD_DIM)
            qaug_scr[h, own, :] = q_t[own]
            qaug_scr[h, other, :] = ext.astype(BF16)
            vaug_scr[h, 0:HEAD_DIM, :] = v_t[own]
            vaug_scr[h, HEAD_DIM:, :] = jnp.ones((2 * SUBLANES, s_len), BF16)

    items = [(h, qb) for h in range(n_heads)
             for qb in (range(n_blk) if h % 2 == 0 else reversed(range(n_blk)))]
    per_pair = HEADS_PER_PAIR * n_blk

    def scores(i):
        h, qb = items[i]
        slot = i % SCORE_SLOTS
        q0, kk = qb * MOBA_BLOCK, (qb + 1) * MOBA_BLOCK
        qa = qaug_scr[h, :, q0:kk]
        m = None
        for r0 in range(0, kk, MOBA_BLOCK):
            s = jnp.dot(kaug_scr[h, r0:r0 + MOBA_BLOCK, :], qa, preferred_element_type=F32)
            if r0 == q0:
                s = jnp.where(causal, s, NEG_BIG)
            s_scr[slot, r0:r0 + MOBA_BLOCK, :] = s
            mh = jnp.max(s.reshape(MOBA_BLOCK // SUBLANES, SUBLANES, MOBA_BLOCK), axis=0)
            m = mh if m is None else jnp.maximum(m, mh)
        return jnp.max(m, axis=0, keepdims=True)

    def probs(i, m):
        _, qb = items[i]
        kk = (qb + 1) * MOBA_BLOCK
        p_scr[i % 2, 0:kk, :] = jnp.exp2(s_scr[i % SCORE_SLOTS, 0:kk, :] - m).astype(BF16)

    def weighted(i):
        h, qb = items[i]
        slot = i % 2
        q0, kk = qb * MOBA_BLOCK, (qb + 1) * MOBA_BLOCK
        o = None
        for r0, r1 in ((0, kk // 2), (kk // 2, kk)):
            part = jnp.dot(vaug_scr[h, :, r0:r1], p_scr[slot, r0:r1, :], preferred_element_type=F32)
            o = part if o is None else o + part
        out_scr[h * HEAD_DIM:(h + 1) * HEAD_DIM, q0:kk] = o[0:HEAD_DIM] / o[HEAD_DIM:HEAD_DIM + 1]

    lru_units = []
    for sl in range(n_slabs):
        lanes = slice(sl * LANES, (sl + 1) * LANES)

        def store_h(value, lanes=lanes):
            ol_ref[0, :, lanes] = value

        lru_units += _lru_units(
            xl_refs[sl].at[0], cw_ref[:, lanes], cb_ref[:, lanes], wg_ref.at[sl], ab_ref[:, lanes],
            xb_ref[:, lanes], lam_ref[:, lanes], head_scr.at[sl], h_scr.at[sl], xc_scr.at[sl],
            g_scr.at[sl], store_h)
    lru_done = 0

    n_items = len(items)
    prepare(0)
    col_max = {i: scores(i) for i in range(SCORE_SLOTS)}
    probs(0, col_max.pop(0))
    for i in range(n_items):
        pp, within = divmod(i, per_pair)
        if within == PREPARE_AT and pp + 1 < n_pairs:
            prepare(pp + 1)
        if i + SCORE_SLOTS < n_items:
            col_max[i + SCORE_SLOTS] = scores(i + SCORE_SLOTS)
        if i + 1 < n_items:
            probs(i + 1, col_max.pop(i + 1))
        weighted(i)
        if within == per_pair - 1:
            lanes = slice(pp * LANES, (pp + 1) * LANES)
            o_ref[0, :, lanes] = out_scr[lanes, :].T.astype(o_ref.dtype)
        lru_target = (i + 1) * len(lru_units) // n_items
        while lru_done < lru_target:
            lru_units[lru_done]()
            lru_done += 1


def _alibi_slopes():
    return np.exp2(-8.0 * (np.arange(N_HEADS, dtype=np.float32) + 1.0) / N_HEADS).astype(np.float32)


def _key_side_table(s_len):
    n_blk = s_len // MOBA_BLOCK
    head = np.arange(N_HEADS)[:, None, None]
    pos = np.arange(s_len)[None, :, None]
    lane = np.arange(LANES)[None, None, :]
    sub = lane % HEAD_DIM
    foreign = lane // HEAD_DIM != head % HEADS_PER_PAIR
    onehot = (sub % n_blk == pos // MOBA_BLOCK).astype(np.float32)
    off = ((_alibi_slopes() * np.float32(LOG2E))[:, None, None]
           * (pos % MOBA_BLOCK).astype(np.float32)).astype(np.float32)
    ext = np.where(sub < BIAS_TERMS * n_blk, onehot, 0.0).astype(np.float32)
    rest = off
    for t in range(BIAS_TERMS):
        part = (rest.view(np.uint32) & np.uint32(0xFFFF0000)).view(np.float32) if t + 1 < BIAS_TERMS else rest
        ext = np.where(sub == BIAS_TERMS * n_blk + t, part, ext)
        rest = rest - part
    table = np.where(foreign, ext, 0.0).astype(np.float32)
    return jnp.asarray(table.astype(BF16))


def _mixer_call(slopes, kext, q_t, k, v_t, xl, conv_w, conv_b, w_gate, a_b, x_b, lam):
    b, s, _ = k.shape
    pairs = ATTN_PAIRS_PER_STEP
    width = pairs * LANES
    heads = pairs * HEADS_PER_PAIR
    row = lambda a: a.reshape(1, D_LRU)
    spec = pl.BlockSpec((1, s, width), lambda j, i, sl: (i, 0, j))
    spec_t = pl.BlockSpec((1, width, s), lambda j, i, sl: (i, j, 0))
    vec = pl.BlockSpec((1, width), lambda j, i, sl: (0, j))
    slab_specs = [pl.BlockSpec((1, s, LANES), lambda j, i, sl, n=n: (i, 0, j * pairs + n))
                  for n in range(pairs)]
    sub_len = LRU_SEGMENT * SUBLANES
    return pl.pallas_call(
        _mixer_kernel,
        grid_spec=pltpu.PrefetchScalarGridSpec(
            num_scalar_prefetch=1,
            grid=(D_ATTN // width, b),
            in_specs=[spec_t, spec, spec_t,
                      pl.BlockSpec((heads, s, LANES), lambda j, i, sl: (j, 0, 0),
                                   pipeline_mode=pl.Buffered(1))]
            + slab_specs
            + [pl.BlockSpec((CONV_WIDTH, width), lambda j, i, sl: (0, j)), vec,
               pl.BlockSpec((pairs, LANES, 2 * LANES), lambda j, i, sl: (j, 0, 0)), vec, vec, vec],
            out_specs=[spec, spec],
            scratch_shapes=[
                pltpu.VMEM((heads, s, LANES), BF16),
                pltpu.VMEM((heads, LANES, s), BF16),
                pltpu.VMEM((heads, HEAD_DIM + 2 * SUBLANES, s), BF16),
                pltpu.VMEM((SCORE_SLOTS, s, MOBA_BLOCK), F32),
                pltpu.VMEM((2, s, MOBA_BLOCK), BF16),
                pltpu.VMEM((width, s), F32),
                pltpu.VMEM((pairs, SUBLANES + sub_len, LANES), F32),
                pltpu.VMEM((pairs, s, LANES), F32),
                pltpu.VMEM((pairs, 2, LRU_CHUNK, LANES), F32),
                pltpu.VMEM((pairs, 2, LRU_CHUNK, 2 * LANES), F32),
            ],
        ),
        out_shape=[jax.ShapeDtypeStruct((b, s, D_ATTN), BF16), jax.ShapeDtypeStruct((b, s, D_LRU), BF16)],
        compiler_params=pltpu.CompilerParams(
            dimension_semantics=("arbitrary", "arbitrary"), vmem_limit_bytes=VMEM_LIMIT_BYTES),
        name="mixer",
    )(slopes, q_t, k, v_t, kext, *([xl] * pairs), conv_w, row(conv_b), w_gate, row(a_b), row(x_b),
      row(lam))


def _lru_units(x_seq, cw, cb, w_gate, a_b, x_b, lam, head_scr, h_scr, xc_scr, g_scr, store_h):
    s_len = x_seq.shape[0]
    seg = LRU_SEGMENT
    sub_len = seg * SUBLANES
    n_sub = LRU_CHUNK // sub_len
    n_chunks = s_len // LRU_CHUNK

    neg_lam = -lam
    softplus = jnp.maximum(neg_lam, 0.0) + jnp.log1p(jnp.exp(-jnp.abs(neg_lam)))
    half_rate = (0.5 * RG_C) * softplus
    rate_log2 = -LOG2E * half_rate
    ab_half = 0.5 * a_b
    xb_half = 0.5 * x_b
    cw_half = 0.5 * cw
    cb_half = 0.5 * cb
    sub = lax.broadcasted_iota(jnp.int32, (SUBLANES, LANES), 0)
    steps = (1, 2, 4)
    in_vreg = [sub >= step for step in steps]
    state = {"h": jnp.zeros((SUBLANES, LANES), F32)}

    def strided(ref, start):
        return ref[pl.ds(start, SUBLANES, stride=seg), :]

    def init():
        head_scr[0:SUBLANES, :] = jnp.zeros((SUBLANES, LANES), F32)
        head_scr[SUBLANES:, :] = x_seq[0:sub_len, :]

    def conv_and_gates(ci):
        xc_parts = []
        for c in range(n_sub):
            if ci == 0 and c == 0:
                src, t0 = head_scr, SUBLANES
            else:
                src, t0 = x_seq, ci * LRU_CHUNK
            cache = {}
            for g in range(seg):
                acc = cb_half
                for j in range(CONV_WIDTH):
                    off = c * sub_len + g - (CONV_WIDTH - 1) + j
                    if off not in cache:
                        cache[off] = strided(src, t0 + off)
                    acc = acc + cw_half[j:j + 1, :] * cache[off]
                xc_parts.append(acc)
        hx = jnp.concatenate(xc_parts, axis=0)
        xc_scr[ci % 2] = hx
        g_scr[ci % 2] = jnp.dot(hx.astype(BF16), w_gate[...], preferred_element_type=F32)

    def recurrence(ci, c):
        t0 = ci * LRU_CHUNK
        rows = slice(c * sub_len, (c + 1) * sub_len)
        h_in = state["h"]
        hx = xc_scr[ci % 2, rows, :]
        t_r = jnp.tanh(g_scr[ci % 2, rows, :LANES] + ab_half)
        t_i = jnp.tanh(g_scr[ci % 2, rows, LANES:] + xb_half)
        two_r = t_r + 1.0
        neg_log_a = half_rate * two_r
        a = jnp.exp2(rate_log2 * two_r)
        y = jnp.tanh(neg_log_a) * (a * a + 1.0)
        root = jnp.where(y == 0.0, 0.0, y * lax.rsqrt(y))
        bterm = root * ((t_i + 1.0) * hx)

        comp = []
        for g in range(seg):
            av, bv = a[g * SUBLANES:(g + 1) * SUBLANES], bterm[g * SUBLANES:(g + 1) * SUBLANES]
            if g > 0:
                bv = av * comp[-1][1] + bv
                av = av * comp[-1][0]
            comp.append((av, bv))
        pa, pb = comp[-1]
        for step, valid in zip(steps, in_vreg):
            a_sh = jnp.where(valid, pltpu.roll(pa, step, axis=0), 1.0)
            b_sh = jnp.where(valid, pltpu.roll(pb, step, axis=0), 0.0)
            pb = pa * b_sh + pb
            pa = pa * a_sh
        ea = jnp.where(in_vreg[0], pltpu.roll(pa, 1, axis=0), 1.0)
        eb = jnp.where(in_vreg[0], pltpu.roll(pb, 1, axis=0), 0.0)
        seg_in = ea * h_in + eb
        for g in range(seg):
            h_g = comp[g][0] * seg_in + comp[g][1]
            h_scr[pl.ds(t0 + c * sub_len + g, SUBLANES, stride=seg), :] = h_g
        h_last = pa * h_in + pb
        state["h"] = jnp.broadcast_to(h_last[SUBLANES - 1:SUBLANES, :], (SUBLANES, LANES))

    def finish():
        store_h(h_scr[...].astype(BF16))

    units = [init, lambda: conv_and_gates(0)]
    for ci in range(n_chunks):
        for c in range(n_sub):
            units.append(lambda ci=ci, c=c: recurrence(ci, c))
            if c == n_sub // 2 - 1 and ci + 1 < n_chunks:
                units.append(lambda ci=ci: conv_and_gates(ci + 1))
    units.append(finish)
    return units


def _gated_norm(y_ref, z_ref, nw_ref):
    y = y_ref[0].astype(F32)
    ms = jnp.mean(y * y, axis=-1, keepdims=True)
    return ((y * lax.rsqrt(ms + EPS) * nw_ref[...]) * _silu(z_ref[0].astype(F32))).astype(BF16)


def _outproj_math(ya_ref, za_ref, yl_ref, zl_ref, x_ref, mod_ref, anw_ref, lnw_ref, w_ref):
    y = jnp.dot(_gated_norm(ya_ref, za_ref, anw_ref), w_ref[0:D_ATTN, :], preferred_element_type=F32)
    y = y + jnp.dot(_gated_norm(yl_ref, zl_ref, lnw_ref), w_ref[D_ATTN:, :],
                    preferred_element_type=F32)
    return x_ref[0] + mod_ref[0, 2:3, :] * y


def _outproj_kernel(ya_ref, za_ref, yl_ref, zl_ref, x_ref, mod_ref, anw_ref, lnw_ref, wf_ref, fnw_ref,
                    o_ref, w_ref):
    _cast_weight_once(wf_ref, w_ref)
    out = _outproj_math(ya_ref, za_ref, yl_ref, zl_ref, x_ref, mod_ref, anw_ref, lnw_ref, w_ref)
    ms = jnp.mean(out * out, axis=-1, keepdims=True)
    o_ref[0] = out * lax.rsqrt(ms + EPS) * fnw_ref[...]


def _outin_kernel(ya_ref, za_ref, yl_ref, zl_ref, x_ref, mod_ref, anw_ref, lnw_ref, wof_ref,
                  modn_ref, nwn_ref, wif_ref,
                  xo_ref, q_ref, k_ref, v_ref, zao_ref, xlo_ref, zlo_ref, wo_ref, wi_ref):
    _cast_weight_once(wof_ref, wo_ref)
    _cast_weight_once(wif_ref, wi_ref)
    out = _outproj_math(ya_ref, za_ref, yl_ref, zl_ref, x_ref, mod_ref, anw_ref, lnw_ref, wo_ref)
    xo_ref[0] = out
    _inproj_math(out, modn_ref, nwn_ref, wi_ref, (q_ref, k_ref, v_ref, zao_ref, xlo_ref, zlo_ref))


def _proj_specs(d, layer, tm=ROW_TILE):
    return dict(
        half=pl.BlockSpec((1, tm, D_ATTN), lambda i, j: (i, j, 0)),
        full=pl.BlockSpec((1, tm, d), lambda i, j: (i, j, 0)),
        mod=pl.BlockSpec((1, 3, d), lambda i, j: (i, 0, 0)),
        nvec=pl.BlockSpec((1, D_ATTN), lambda i, j: (0, 0)),
        dvec=pl.BlockSpec((1, d), lambda i, j: (0, 0)),
        w_out=pl.BlockSpec((1, d, d), lambda i, j: (layer, 0, 0), pipeline_mode=pl.Buffered(1)),
        w_in=pl.BlockSpec((1, d, D_IN), lambda i, j: (layer + 1, 0, 0), pipeline_mode=pl.Buffered(1)),
    )


def _outproj_call(ya, za, yl, zl, x, mod3, anw, lnw, w_out, layer, fnw):
    b, s, d = x.shape
    sp = _proj_specs(d, layer, OUT_ROW_TILE)
    return pl.pallas_call(
        _outproj_kernel,
        grid=(b, s // OUT_ROW_TILE),
        in_specs=[sp["half"]] * 4 + [sp["full"], sp["mod"], sp["nvec"], sp["nvec"], sp["w_out"],
                                     sp["dvec"]],
        out_specs=sp["full"],
        out_shape=jax.ShapeDtypeStruct((b, s, d), F32),
        scratch_shapes=[pltpu.VMEM((d, d), BF16)],
        compiler_params=pltpu.CompilerParams(
            dimension_semantics=("arbitrary", "arbitrary"), vmem_limit_bytes=VMEM_LIMIT_BYTES),
        name="outproj",
    )(ya, za, yl, zl, x, mod3, anw.reshape(1, D_ATTN), lnw.reshape(1, D_LRU), w_out,
      fnw.reshape(1, d))


def _outin_call(ya, za, yl, zl, x, mod3, anw, lnw, w_out, layer, mod3_next, norm_w_next, w_in):
    b, s, d = x.shape
    sp = _proj_specs(d, layer)
    proj_specs, proj_shapes = _inproj_outputs(b, s)
    return pl.pallas_call(
        _outin_kernel,
        grid=(b, s // ROW_TILE),
        in_specs=[sp["half"]] * 4 + [sp["full"], sp["mod"], sp["nvec"], sp["nvec"], sp["w_out"],
                                     sp["mod"], sp["dvec"], sp["w_in"]],
        out_specs=[sp["full"]] + proj_specs,
        out_shape=[jax.ShapeDtypeStruct((b, s, d), F32)] + proj_shapes,
        scratch_shapes=[pltpu.VMEM((d, d), BF16), pltpu.VMEM((d, D_IN), BF16)],
        compiler_params=pltpu.CompilerParams(
            dimension_semantics=("arbitrary", "arbitrary"), vmem_limit_bytes=PROJ_VMEM_LIMIT_BYTES),
        name="outproj_inproj",
    )(ya, za, yl, zl, x, mod3, anw.reshape(1, D_ATTN), lnw.reshape(1, D_LRU), w_out,
      mod3_next, norm_w_next.reshape(1, d), w_in)


def _gate_weights(a_w, x_w):
    per_slab = LANES // LRU_BLOCK
    slabs = []
    for j in range(D_LRU // LANES):
        blocks = range(j * per_slab, (j + 1) * per_slab)
        diag = lambda w: jax.scipy.linalg.block_diag(*[w[g] for g in blocks])
        slabs.append(jnp.concatenate([diag(a_w), diag(x_w)], axis=1))
    return jnp.stack(slabs).astype(BF16)


def kernel(x, c, norm_w, ada_w, ada_b, w_in, conv_w, conv_b, rg_a_w, rg_a_b, rg_x_w, rg_x_b,
           rg_lambda, attn_out_norm_w, lru_out_norm_w, w_out, final_norm_w):
    n_layers = ada_w.shape[0]
    b = x.shape[0]
    mod = _mod_call(c, ada_w, ada_b)
    slopes = jnp.asarray(_alibi_slopes())
    kext = _key_side_table(x.shape[1])
    mod3 = [mod[l].reshape(b, 3, D_MODEL) for l in range(n_layers)]
    q, k, v, za, xl, zl = _inproj_call(x, mod3[0], norm_w[0], w_in, 0)
    for l in range(n_layers):
        ya, yl = _mixer_call(slopes, kext, q, k, v, xl, conv_w[l], conv_b[l],
                             _gate_weights(rg_a_w[l], rg_x_w[l]), rg_a_b[l], rg_x_b[l], rg_lambda[l])
        if l + 1 < n_layers:
            x, q, k, v, za, xl, zl = _outin_call(
                ya, za, yl, zl, x, mod3[l], attn_out_norm_w[l], lru_out_norm_w[l], w_out, l,
                mod3[l + 1], norm_w[l + 1], w_in)
        else:
            x = _outproj_call(ya, za, yl, zl, x, mod3[l], attn_out_norm_w[l], lru_out_norm_w[l],
                              w_out, l, final_norm_w)
    return x
```

```python
import math

import jax
import jax.numpy as jnp
import numpy as np
from jax import lax
from jax.experimental import pallas as pl
from jax.experimental.pallas import tpu as pltpu

D_MODEL = 1024
D_ATTN = 512
D_LRU = 512
HEAD_DIM = 64
N_HEADS = D_ATTN // HEAD_DIM
N_LRU_BLOCKS = 8
LRU_BLOCK = D_LRU // N_LRU_BLOCKS
CONV_WIDTH = 4
RG_C = 8.0
MOBA_BLOCK = 256
MOBA_TOPK = 3
D_IN = 4 * D_ATTN + 2 * D_LRU
EPS = 1e-6

F32 = jnp.float32
BF16 = jnp.bfloat16
NEG_BIG = -1e30
LOG2E = math.log2(math.e)

LANES = 128
SUBLANES = 8
VMEM_LIMIT_BYTES = 48 * 1024 * 1024
PROJ_VMEM_LIMIT_BYTES = 56 * 1024 * 1024

ROW_TILE = 512
IN_ROW_TILE = 1024
OUT_ROW_TILE = 1024
WEIGHT_CAST_ROWS = 64
LRU_CHUNK = 256
LRU_SEGMENT = 4
HEADS_PER_PAIR = LANES // HEAD_DIM
BIAS_TERMS = 3
FEATURE_MAJOR_OUTPUTS = (0, 2)
ATTN_PAIRS_PER_STEP = 2
SCORE_SLOTS = 3
PREPARE_AT = 6


def _sigmoid(v):
    return 0.5 * jnp.tanh(0.5 * v) + 0.5


def _silu(v):
    return v * _sigmoid(v)


def _split_bf16(v):
    parts = []
    for _ in range(BIAS_TERMS):
        part = v.astype(BF16).astype(F32)
        parts.append(part)
        v = v - part
    return parts


def _mod_kernel(c_ref, w_ref, b_ref, o_ref):
    s = _silu(c_ref[...]).astype(BF16)
    w = w_ref[0].astype(BF16)
    o_ref[0] = jnp.dot(s, w, preferred_element_type=F32) + b_ref[0]


def _mod_call(c, ada_w, ada_b):
    n_layers, d, d3 = ada_w.shape
    b = c.shape[0]
    tn = 1024
    return pl.pallas_call(
        _mod_kernel,
        grid=(n_layers, d3 // tn),
        in_specs=[
            pl.BlockSpec((b, d), lambda l, j: (0, 0)),
            pl.BlockSpec((1, d, tn), lambda l, j: (l, 0, j)),
            pl.BlockSpec((1, 1, tn), lambda l, j: (l, 0, j)),
        ],
        out_specs=pl.BlockSpec((1, b, tn), lambda l, j: (l, 0, j)),
        out_shape=jax.ShapeDtypeStruct((n_layers, b, d3), F32),
        compiler_params=pltpu.CompilerParams(
            dimension_semantics=("arbitrary", "arbitrary"), vmem_limit_bytes=VMEM_LIMIT_BYTES),
        name="adaln_mod",
    )(c, ada_w, ada_b.reshape(n_layers, 1, d3))


def _cast_weight_once(w_ref, wbf_scr):
    rows = w_ref.shape[1]

    @pl.when((pl.program_id(0) == 0) & (pl.program_id(1) == 0))
    def _():
        def body(i, carry):
            r0 = pl.multiple_of(i * WEIGHT_CAST_ROWS, WEIGHT_CAST_ROWS)
            wbf_scr[pl.ds(r0, WEIGHT_CAST_ROWS), :] = w_ref[0, pl.ds(r0, WEIGHT_CAST_ROWS), :].astype(BF16)
            return carry
        lax.fori_loop(0, rows // WEIGHT_CAST_ROWS, body, 0)


def _inproj_math(x, mod_ref, nw_ref, w_ref, out_refs):
    ms = jnp.mean(x * x, axis=-1, keepdims=True)
    y = x * lax.rsqrt(ms + EPS) * nw_ref[...]
    shift = mod_ref[0, 0:1, :]
    scale = mod_ref[0, 1:2, :]
    h = (y * (1.0 + scale) + shift).astype(BF16)
    for j, ref in enumerate(out_refs):
        r = jnp.dot(h, w_ref[:, j * D_ATTN:(j + 1) * D_ATTN], preferred_element_type=F32)
        if j == 0:
            r = r * (HEAD_DIM ** -0.5 * LOG2E)
        if j in FEATURE_MAJOR_OUTPUTS:
            r = r.T
        ref[0] = r.astype(ref.dtype)


def _inproj_kernel(x_ref, mod_ref, nw_ref, wf_ref, q_ref, k_ref, v_ref, za_ref, xl_ref, zl_ref,
                   w_ref):
    _cast_weight_once(wf_ref, w_ref)
    _inproj_math(x_ref[0], mod_ref, nw_ref, w_ref, (q_ref, k_ref, v_ref, za_ref, xl_ref, zl_ref))


def _inproj_outputs(b, s, tm=ROW_TILE):
    token_major = pl.BlockSpec((1, tm, D_ATTN), lambda i, j: (i, j, 0))
    feature_major = pl.BlockSpec((1, D_ATTN, tm), lambda i, j: (i, 0, j))
    dtypes = (BF16, BF16, BF16, BF16, F32, BF16)
    specs, shapes = [], []
    for j, dt in enumerate(dtypes):
        fm = j in FEATURE_MAJOR_OUTPUTS
        specs.append(feature_major if fm else token_major)
        shapes.append(jax.ShapeDtypeStruct((b, D_ATTN, s) if fm else (b, s, D_ATTN), dt))
    return specs, shapes


def _inproj_call(x, mod3, norm_w, w_in, layer):
    b, s, d = x.shape
    tm = IN_ROW_TILE
    out_specs, out_shape = _inproj_outputs(b, s, tm)
    return pl.pallas_call(
        _inproj_kernel,
        grid=(b, s // tm),
        in_specs=[
            pl.BlockSpec((1, tm, d), lambda i, j: (i, j, 0)),
            pl.BlockSpec((1, 3, d), lambda i, j: (i, 0, 0)),
            pl.BlockSpec((1, d), lambda i, j: (0, 0)),
            pl.BlockSpec((1, d, D_IN), lambda i, j: (layer, 0, 0), pipeline_mode=pl.Buffered(1)),
        ],
        out_specs=out_specs,
        out_shape=out_shape,
        scratch_shapes=[pltpu.VMEM((d, D_IN), BF16)],
        compiler_params=pltpu.CompilerParams(
            dimension_semantics=("arbitrary", "arbitrary"), vmem_limit_bytes=VMEM_LIMIT_BYTES),
        name="inproj",
    )(x, mod3, norm_w.reshape(1, d), w_in)


def _mixer_kernel(slopes_ref, q_ref, k_ref, v_ref, kext_ref, *rest):
    n_slabs = k_ref.shape[2] // LANES
    xl_refs = rest[:n_slabs]
    (cw_ref, cb_ref, wg_ref, ab_ref, xb_ref, lam_ref, o_ref, ol_ref,
     kaug_scr, qaug_scr, vaug_scr, s_scr, p_scr, out_scr,
     head_scr, h_scr, xc_scr, g_scr) = rest[n_slabs:]
    s_len = k_ref.shape[1]
    n_blk = s_len // MOBA_BLOCK
    n_pairs = k_ref.shape[2] // LANES
    n_heads = n_pairs * HEADS_PER_PAIR
    first_head = pl.program_id(0) * n_heads

    prow = lax.broadcasted_iota(jnp.int32, (n_blk, s_len), 0)
    pcol = lax.broadcasted_iota(jnp.int32, (n_blk, s_len), 1)
    block_mean = jnp.where(pcol // MOBA_BLOCK == prow, 1.0 / MOBA_BLOCK, 0.0).astype(BF16)
    klane = lax.broadcasted_iota(jnp.int32, (s_len, LANES), 1)
    mlane = lax.broadcasted_iota(jnp.int32, (n_blk, LANES), 1)
    n_iota = prow
    q_blk = pcol // MOBA_BLOCK
    key_off = lax.broadcasted_iota(jnp.int32, (MOBA_BLOCK, MOBA_BLOCK), 0)
    qry_off = lax.broadcasted_iota(jnp.int32, (MOBA_BLOCK, MOBA_BLOCK), 1)
    causal = key_off <= qry_off

    def prepare(pp):
        lanes = slice(pp * LANES, (pp + 1) * LANES)
        kp = k_ref[0, :, lanes]
        q_t = q_ref[0, lanes, :]
        v_t = v_ref[0, lanes, :]
        kmean = jnp.dot(block_mean, kp, preferred_element_type=F32)
        terms = []
        for hd in range(HEADS_PER_PAIR):
            rest = jnp.where(mlane // HEAD_DIM == hd, kmean, 0.0)
            for _ in range(3):
                part = rest.astype(BF16).astype(F32)
                terms.append(part)
                rest = rest - part
        terms = jnp.concatenate(terms, axis=0).astype(BF16)
        half = s_len // 2
        gates = jnp.concatenate(
            [jnp.dot(terms, q_t[:, :half], preferred_element_type=F32),
             jnp.dot(terms, q_t[:, half:], preferred_element_type=F32)], axis=1)
        for hd in range(HEADS_PER_PAIR):
            h = pp * HEADS_PER_PAIR + hd
            kaug_scr[h] = jnp.where(klane // HEAD_DIM == hd, kp, kext_ref[h])
            slope2 = slopes_ref[first_head + h] * LOG2E
            g_hi, g_mid, g_lo = (gates[(3 * hd + t) * n_blk:(3 * hd + t + 1) * n_blk]
                                 for t in range(3))
            g_t = g_hi + g_mid + g_lo
            rank = jnp.zeros((n_blk, s_len), jnp.int32)
            for m in range(n_blk):
                gm = g_t[m:m + 1, :]
                beats = (gm > g_t) | ((gm == g_t) & (m < n_iota))
                rank = rank + jnp.where(beats & (m < q_blk), 1, 0)
            sel = ((n_iota < q_blk) & (rank < MOBA_TOPK)) | (n_iota == q_blk)
            bias = _split_bf16(slope2 * ((n_iota - q_blk) * MOBA_BLOCK).astype(F32))
            selb = [jnp.where(sel, part, NEG_BIG if t == 0 else 0.0)
                    for t, part in enumerate(bias)]
            flag = jnp.where(n_iota < BIAS_TERMS, 1.0, 0.0)
            pad = jnp.zeros((HEAD_DIM - (BIAS_TERMS + 1) * n_blk, s_len), F32)
            ext = jnp.concatenate(selb + [flag, pad], axis=0)
            own = slice(hd * HEAD_DIM, (hd + 1) * HEAD_DIM)
            other = slice((1 - hd) * HEAD_DIM, (2 - hd) * HEAD_DIM)
            qaug_scr[h, own, :] = q_t[own]
            qaug_scr[h, other, :] = ext.astype(BF16)
            vaug_scr[h, 0:HEAD_DIM, :] = v_t[own]
            vaug_scr[h, HEAD_DIM:, :] = jnp.ones((2 * SUBLANES, s_len), BF16)

    items = [(h, qb) for h in range(n_heads)
             for qb in (range(n_blk) if h % 2 == 0 else reversed(range(n_blk)))]
    per_pair = HEADS_PER_PAIR * n_blk

    def scores(i):
        h, qb = items[i]
        slot = i % SCORE_SLOTS
        q0, kk = qb * MOBA_BLOCK, (qb + 1) * MOBA_BLOCK
        qa = qaug_scr[h, :, q0:kk]
        m = None
        for r0 in range(0, kk, MOBA_BLOCK):
            s = jnp.dot(kaug_scr[h, r0:r0 + MOBA_BLOCK, :], qa, preferred_element_type=F32)
            if r0 == q0:
                s = jnp.where(causal, s, NEG_BIG)
            s_scr[slot, r0:r0 + MOBA_BLOCK, :] = s
            mh = jnp.max(s.reshape(MOBA_BLOCK // SUBLANES, SUBLANES, MOBA_BLOCK), axis=0)
            m = mh if m is None else jnp.maximum(m, mh)
        return jnp.max(m, axis=0, keepdims=True)

    def probs(i, m):
        _, qb = items[i]
        kk = (qb + 1) * MOBA_BLOCK
        p_scr[i % 2, 0:kk, :] = jnp.exp2(s_scr[i % SCORE_SLOTS, 0:kk, :] - m).astype(BF16)

    def weighted(i):
        h, qb = items[i]
        slot = i % 2
        q0, kk = qb * MOBA_BLOCK, (qb + 1) * MOBA_BLOCK
        o = None
        for r0, r1 in ((0, kk // 2), (kk // 2, kk)):
            part = jnp.dot(vaug_scr[h, :, r0:r1], p_scr[slot, r0:r1, :], preferred_element_type=F32)
            o = part if o is None else o + part
        out_scr[h * HEAD_DIM:(h + 1) * HEAD_DIM, q0:kk] = o[0:HEAD_DIM] / o[HEAD_DIM:HEAD_DIM + 1]

    lru_units = []
    for sl in range(n_slabs):
        lanes = slice(sl * LANES, (sl + 1) * LANES)

        def store_h(value, lanes=lanes):
            ol_ref[0, :, lanes] = value

        lru_units += _lru_units(
            xl_refs[sl].at[0], cw_ref[:, lanes], cb_ref[:, lanes], wg_ref.at[sl], ab_ref[:, lanes],
            xb_ref[:, lanes], lam_ref[:, lanes], head_scr.at[sl], h_scr.at[sl], xc_scr.at[sl],
            g_scr.at[sl], store_h)
    lru_done = 0

    n_items = len(items)
    prepare(0)
    col_max = {i: scores(i) for i in range(SCORE_SLOTS)}
    probs(0, col_max.pop(0))
    for i in range(n_items):
        pp, within = divmod(i, per_pair)
        if within == PREPARE_AT and pp + 1 < n_pairs:
            prepare(pp + 1)
        if i + SCORE_SLOTS < n_items:
            col_max[i + SCORE_SLOTS] = scores(i + SCORE_SLOTS)
        if i + 1 < n_items:
            probs(i + 1, col_max.pop(i + 1))
        weighted(i)
        if within == per_pair - 1:
            lanes = slice(pp * LANES, (pp + 1) * LANES)
            o_ref[0, :, lanes] = out_scr[lanes, :].T.astype(o_ref.dtype)
        lru_target = (i + 1) * len(lru_units) // n_items
        while lru_done < lru_target:
            lru_units[lru_done]()
            lru_done += 1


def _alibi_slopes():
    return np.exp2(-8.0 * (np.arange(N_HEADS, dtype=np.float32) + 1.0) / N_HEADS).astype(np.float32)


def _key_side_table(s_len):
    n_blk = s_len // MOBA_BLOCK
    head = np.arange(N_HEADS)[:, None, None]
    pos = np.arange(s_len)[None, :, None]
    lane = np.arange(LANES)[None, None, :]
    sub = lane % HEAD_DIM
    foreign = lane // HEAD_DIM != head % HEADS_PER_PAIR
    onehot = (sub % n_blk == pos // MOBA_BLOCK).astype(np.float32)
    off = ((_alibi_slopes() * np.float32(LOG2E))[:, None, None]
           * (pos % MOBA_BLOCK).astype(np.float32)).astype(np.float32)
    ext = np.where(sub < BIAS_TERMS * n_blk, onehot, 0.0).astype(np.float32)
    rest = off
    for t in range(BIAS_TERMS):
        part = (rest.view(np.uint32) & np.uint32(0xFFFF0000)).view(np.float32) if t + 1 < BIAS_TERMS else rest
        ext = np.where(sub == BIAS_TERMS * n_blk + t, part, ext)
        rest = rest - part
    table = np.where(foreign, ext, 0.0).astype(np.float32)
    return jnp.asarray(table.astype(BF16))


def _mixer_call(slopes, kext, q_t, k, v_t, xl, conv_w, conv_b, w_gate, a_b, x_b, lam):
    b, s, _ = k.shape
    pairs = ATTN_PAIRS_PER_STEP
    width = pairs * LANES
    heads = pairs * HEADS_PER_PAIR
    row = lambda a: a.reshape(1, D_LRU)
    spec = pl.BlockSpec((1, s, width), lambda j, i, sl: (i, 0, j))
    spec_t = pl.BlockSpec((1, width, s), lambda j, i, sl: (i, j, 0))
    vec = pl.BlockSpec((1, width), lambda j, i, sl: (0, j))
    slab_specs = [pl.BlockSpec((1, s, LANES), lambda j, i, sl, n=n: (i, 0, j * pairs + n))
                  for n in range(pairs)]
    sub_len = LRU_SEGMENT * SUBLANES
    return pl.pallas_call(
        _mixer_kernel,
        grid_spec=pltpu.PrefetchScalarGridSpec(
            num_scalar_prefetch=1,
            grid=(D_ATTN // width, b),
            in_specs=[spec_t, spec, spec_t,
                      pl.BlockSpec((heads, s, LANES), lambda j, i, sl: (j, 0, 0),
                                   pipeline_mode=pl.Buffered(1))]
            + slab_specs
            + [pl.BlockSpec((CONV_WIDTH, width), lambda j, i, sl: (0, j)), vec,
               pl.BlockSpec((pairs, LANES, 2 * LANES), lambda j, i, sl: (j, 0, 0)), vec, vec, vec],
            out_specs=[spec, spec],
            scratch_shapes=[
                pltpu.VMEM((heads, s, LANES), BF16),
                pltpu.VMEM((heads, LANES, s), BF16),
                pltpu.VMEM((heads, HEAD_DIM + 2 * SUBLANES, s), BF16),
                pltpu.VMEM((SCORE_SLOTS, s, MOBA_BLOCK), F32),
                pltpu.VMEM((2, s, MOBA_BLOCK), BF16),
                pltpu.VMEM((width, s), F32),
                pltpu.VMEM((pairs, SUBLANES + sub_len, LANES), F32),
                pltpu.VMEM((pairs, s, LANES), F32),
                pltpu.VMEM((pairs, 2, LRU_CHUNK, LANES), F32),
                pltpu.VMEM((pairs, 2, LRU_CHUNK, 2 * LANES), F32),
            ],
        ),
        out_shape=[jax.ShapeDtypeStruct((b, s, D_ATTN), BF16), jax.ShapeDtypeStruct((b, s, D_LRU), BF16)],
        compiler_params=pltpu.CompilerParams(
            dimension_semantics=("arbitrary", "arbitrary"), vmem_limit_bytes=VMEM_LIMIT_BYTES),
        name="mixer",
    )(slopes, q_t, k, v_t, kext, *([xl] * pairs), conv_w, row(conv_b), w_gate, row(a_b), row(x_b),
      row(lam))


def _lru_units(x_seq, cw, cb, w_gate, a_b, x_b, lam, head_scr, h_scr, xc_scr, g_scr, store_h):
    s_len = x_seq.shape[0]
    seg = LRU_SEGMENT
    sub_len = seg * SUBLANES
    n_sub = LRU_CHUNK // sub_len
    n_chunks = s_len // LRU_CHUNK

    neg_lam = -lam
    softplus = jnp.maximum(neg_lam, 0.0) + jnp.log1p(jnp.exp(-jnp.abs(neg_lam)))
    half_rate = (0.5 * RG_C) * softplus
    rate_log2 = -LOG2E * half_rate
    ab_half = 0.5 * a_b
    xb_half = 0.5 * x_b
    cw_half = 0.5 * cw
    cb_half = 0.5 * cb
    sub = lax.broadcasted_iota(jnp.int32, (SUBLANES, LANES), 0)
    steps = (1, 2, 4)
    in_vreg = [sub >= step for step in steps]
    state = {"h": jnp.zeros((SUBLANES, LANES), F32)}

    def strided(ref, start):
        return ref[pl.ds(start, SUBLANES, stride=seg), :]

    def init():
        head_scr[0:SUBLANES, :] = jnp.zeros((SUBLANES, LANES), F32)
        head_scr[SUBLANES:, :] = x_seq[0:sub_len, :]

    def conv_and_gates(ci):
        xc_parts = []
        for c in range(n_sub):
            if ci == 0 and c == 0:
                src, t0 = head_scr, SUBLANES
            else:
                src, t0 = x_seq, ci * LRU_CHUNK
            cache = {}
            for g in range(seg):
                acc = cb_half
                for j in range(CONV_WIDTH):
                    off = c * sub_len + g - (CONV_WIDTH - 1) + j
                    if off not in cache:
                        cache[off] = strided(src, t0 + off)
                    acc = acc + cw_half[j:j + 1, :] * cache[off]
                xc_parts.append(acc)
        hx = jnp.concatenate(xc_parts, axis=0)
        xc_scr[ci % 2] = hx
        g_scr[ci % 2] = jnp.dot(hx.astype(BF16), w_gate[...], preferred_element_type=F32)

    def recurrence(ci, c):
        t0 = ci * LRU_CHUNK
        rows = slice(c * sub_len, (c + 1) * sub_len)
        h_in = state["h"]
        hx = xc_scr[ci % 2, rows, :]
        t_r = jnp.tanh(g_scr[ci % 2, rows, :LANES] + ab_half)
        t_i = jnp.tanh(g_scr[ci % 2, rows, LANES:] + xb_half)
        two_r = t_r + 1.0
        neg_log_a = half_rate * two_r
        a = jnp.exp2(rate_log2 * two_r)
        y = jnp.tanh(neg_log_a) * (a * a + 1.0)
        root = jnp.where(y == 0.0, 0.0, y * lax.rsqrt(y))
        bterm = root * ((t_i + 1.0) * hx)

        comp = []
        for g in range(seg):
            av, bv = a[g * SUBLANES:(g + 1) * SUBLANES], bterm[g * SUBLANES:(g + 1) * SUBLANES]
            if g > 0:
                bv = av * comp[-1][1] + bv
                av = av * comp[-1][0]
            comp.append((av, bv))
        pa, pb = comp[-1]
        for step, valid in zip(steps, in_vreg):
            a_sh = jnp.where(valid, pltpu.roll(pa, step, axis=0), 1.0)
            b_sh = jnp.where(valid, pltpu.roll(pb, step, axis=0), 0.0)
            pb = pa * b_sh + pb
            pa = pa * a_sh
        ea = jnp.where(in_vreg[0], pltpu.roll(pa, 1, axis=0), 1.0)
        eb = jnp.where(in_vreg[0], pltpu.roll(pb, 1, axis=0), 0.0)
        seg_in = ea * h_in + eb
        for g in range(seg):
            h_g = comp[g][0] * seg_in + comp[g][1]
            h_scr[pl.ds(t0 + c * sub_len + g, SUBLANES, stride=seg), :] = h_g
        h_last = pa * h_in + pb
        state["h"] = jnp.broadcast_to(h_last[SUBLANES - 1:SUBLANES, :], (SUBLANES, LANES))

    def finish():
        store_h(h_scr[...].astype(BF16))

    units = [init, lambda: conv_and_gates(0)]
    for ci in range(n_chunks):
        for c in range(n_sub):
            units.append(lambda ci=ci, c=c: recurrence(ci, c))
            if c == n_sub // 2 - 1 and ci + 1 < n_chunks:
                units.append(lambda ci=ci: conv_and_gates(ci + 1))
    units.append(finish)
    return units


def _gated_norm(y_ref, z_ref, nw_ref):
    y = y_ref[0].astype(F32)
    ms = jnp.mean(y * y, axis=-1, keepdims=True)
    return ((y * lax.rsqrt(ms + EPS) * nw_ref[...]) * _silu(z_ref[0].astype(F32))).astype(BF16)


def _outproj_math(ya_ref, za_ref, yl_ref, zl_ref, x_ref, mod_ref, anw_ref, lnw_ref, w_ref):
    y = jnp.dot(_gated_norm(ya_ref, za_ref, anw_ref), w_ref[0:D_ATTN, :], preferred_element_type=F32)
    y = y + jnp.dot(_gated_norm(yl_ref, zl_ref, lnw_ref), w_ref[D_ATTN:, :],
                    preferred_element_type=F32)
    return x_ref[0] + mod_ref[0, 2:3, :] * y


def _outproj_kernel(ya_ref, za_ref, yl_ref, zl_ref, x_ref, mod_ref, anw_ref, lnw_ref, wf_ref, fnw_ref,
                    o_ref, w_ref):
    _cast_weight_once(wf_ref, w_ref)
    out = _outproj_math(ya_ref, za_ref, yl_ref, zl_ref, x_ref, mod_ref, anw_ref, lnw_ref, w_ref)
    ms = jnp.mean(out * out, axis=-1, keepdims=True)
    o_ref[0] = out * lax.rsqrt(ms + EPS) * fnw_ref[...]


def _outin_kernel(ya_ref, za_ref, yl_ref, zl_ref, x_ref, mod_ref, anw_ref, lnw_ref, wof_ref,
                  modn_ref, nwn_ref, wif_ref,
                  xo_ref, q_ref, k_ref, v_ref, zao_ref, xlo_ref, zlo_ref, wo_ref, wi_ref):
    _cast_weight_once(wof_ref, wo_ref)
    _cast_weight_once(wif_ref, wi_ref)
    out = _outproj_math(ya_ref, za_ref, yl_ref, zl_ref, x_ref, mod_ref, anw_ref, lnw_ref, wo_ref)
    xo_ref[0] = out
    _inproj_math(out, modn_ref, nwn_ref, wi_ref, (q_ref, k_ref, v_ref, zao_ref, xlo_ref, zlo_ref))


def _proj_specs(d, layer, tm=ROW_TILE):
    return dict(
        half=pl.BlockSpec((1, tm, D_ATTN), lambda i, j: (i, j, 0)),
        full=pl.BlockSpec((1, tm, d), lambda i, j: (i, j, 0)),
        mod=pl.BlockSpec((1, 3, d), lambda i, j: (i, 0, 0)),
        nvec=pl.BlockSpec((1, D_ATTN), lambda i, j: (0, 0)),
        dvec=pl.BlockSpec((1, d), lambda i, j: (0, 0)),
        w_out=pl.BlockSpec((1, d, d), lambda i, j: (layer, 0, 0), pipeline_mode=pl.Buffered(1)),
        w_in=pl.BlockSpec((1, d, D_IN), lambda i, j: (layer + 1, 0, 0), pipeline_mode=pl.Buffered(1)),
    )


def _outproj_call(ya, za, yl, zl, x, mod3, anw, lnw, w_out, layer, fnw):
    b, s, d = x.shape
    sp = _proj_specs(d, layer, OUT_ROW_TILE)
    return pl.pallas_call(
        _outproj_kernel,
        grid=(b, s // OUT_ROW_TILE),
        in_specs=[sp["half"]] * 4 + [sp["full"], sp["mod"], sp["nvec"], sp["nvec"], sp["w_out"],
                                     sp["dvec"]],
        out_specs=sp["full"],
        out_shape=jax.ShapeDtypeStruct((b, s, d), F32),
        scratch_shapes=[pltpu.VMEM((d, d), BF16)],
        compiler_params=pltpu.CompilerParams(
            dimension_semantics=("arbitrary", "arbitrary"), vmem_limit_bytes=VMEM_LIMIT_BYTES),
        name="outproj",
    )(ya, za, yl, zl, x, mod3, anw.reshape(1, D_ATTN), lnw.reshape(1, D_LRU), w_out,
      fnw.reshape(1, d))


def _outin_call(ya, za, yl, zl, x, mod3, anw, lnw, w_out, layer, mod3_next, norm_w_next, w_in):
    b, s, d = x.shape
    sp = _proj_specs(d, layer)
    proj_specs, proj_shapes = _inproj_outputs(b, s)
    return pl.pallas_call(
        _outin_kernel,
        grid=(b, s // ROW_TILE),
        in_specs=[sp["half"]] * 4 + [sp["full"], sp["mod"], sp["nvec"], sp["nvec"], sp["w_out"],
                                     sp["mod"], sp["dvec"], sp["w_in"]],
        out_specs=[sp["full"]] + proj_specs,
        out_shape=[jax.ShapeDtypeStruct((b, s, d), F32)] + proj_shapes,
        scratch_shapes=[pltpu.VMEM((d, d), BF16), pltpu.VMEM((d, D_IN), BF16)],
        compiler_params=pltpu.CompilerParams(
            dimension_semantics=("arbitrary", "arbitrary"), vmem_limit_bytes=PROJ_VMEM_LIMIT_BYTES),
        name="outproj_inproj",
    )(ya, za, yl, zl, x, mod3, anw.reshape(1, D_ATTN), lnw.reshape(1, D_LRU), w_out,
      mod3_next, norm_w_next.reshape(1, d), w_in)


def _gate_weights(a_w, x_w):
    per_slab = LANES // LRU_BLOCK
    slabs = []
    for j in range(D_LRU // LANES):
        blocks = range(j * per_slab, (j + 1) * per_slab)
        diag = lambda w: jax.scipy.linalg.block_diag(*[w[g] for g in blocks])
        slabs.append(jnp.concatenate([diag(a_w), diag(x_w)], axis=1))
    return jnp.stack(slabs).astype(BF16)


def kernel(x, c, norm_w, ada_w, ada_b, w_in, conv_w, conv_b, rg_a_w, rg_a_b, rg_x_w, rg_x_b,
           rg_lambda, attn_out_norm_w, lru_out_norm_w, w_out, final_norm_w):
    n_layers = ada_w.shape[0]
    b = x.shape[0]
    mod = _mod_call(c, ada_w, ada_b)
    slopes = jnp.asarray(_alibi_slopes())
    kext = _key_side_table(x.shape[1])
    mod3 = [mod[l].reshape(b, 3, D_MODEL) for l in range(n_layers)]
    q, k, v, za, xl, zl = _inproj_call(x, mod3[0], norm_w[0], w_in, 0)
    for l in range(n_layers):
        ya, yl = _mixer_call(slopes, kext, q, k, v, xl, conv_w[l], conv_b[l],
                             _gate_weights(rg_a_w[l], rg_x_w[l]), rg_a_b[l], rg_x_b[l], rg_lambda[l])
        if l + 1 < n_layers:
            x, q, k, v, za, xl, zl = _outin_call(
                ya, za, yl, zl, x, mod3[l], attn_out_norm_w[l], lru_out_norm_w[l], w_out, l,
                mod3[l + 1], norm_w[l + 1], w_in)
        else:
            x = _outproj_call(ya, za, yl, zl, x, mod3[l], attn_out_norm_w[l], lru_out_norm_w[l],
                              w_out, l, final_norm_w)
    return x
```

```python
import math

import jax
import jax.numpy as jnp
import numpy as np
from jax import lax
from jax.experimental import pallas as pl
from jax.experimental.pallas import tpu as pltpu

D_MODEL = 1024
D_ATTN = 512
D_LRU = 512
HEAD_DIM = 64
N_HEADS = D_ATTN // HEAD_DIM
N_LRU_BLOCKS = 8
LRU_BLOCK = D_LRU // N_LRU_BLOCKS
CONV_WIDTH = 4
RG_C = 8.0
MOBA_BLOCK = 256
MOBA_TOPK = 3
D_IN = 4 * D_ATTN + 2 * D_LRU
EPS = 1e-6

F32 = jnp.float32
BF16 = jnp.bfloat16
NEG_BIG = -1e30
LOG2E = math.log2(math.e)

LANES = 128
SUBLANES = 8
VMEM_LIMIT_BYTES = 48 * 1024 * 1024
PROJ_VMEM_LIMIT_BYTES = 56 * 1024 * 1024

ROW_TILE = 512
IN_ROW_TILE = 1024
OUT_ROW_TILE = 1024
OUT_SUB_ROWS = 256
WEIGHT_CAST_ROWS = 64
LRU_CHUNK = 256
LRU_SEGMENT = 4
HEADS_PER_PAIR = LANES // HEAD_DIM
BIAS_TERMS = 3
FEATURE_MAJOR_OUTPUTS = (0, 2)
ATTN_PAIRS_PER_STEP = 2
SCORE_SLOTS = 3
PREPARE_AT = 6


def _sigmoid(v):
    return 0.5 * jnp.tanh(0.5 * v) + 0.5


def _silu(v):
    return v * _sigmoid(v)


def _split_bf16(v):
    parts = []
    for _ in range(BIAS_TERMS):
        part = v.astype(BF16).astype(F32)
        parts.append(part)
        v = v - part
    return parts


def _mod_kernel(c_ref, w_ref, b_ref, o_ref):
    s = _silu(c_ref[...]).astype(BF16)
    w = w_ref[0].astype(BF16)
    o_ref[0] = jnp.dot(s, w, preferred_element_type=F32) + b_ref[0]


def _mod_call(c, ada_w, ada_b):
    n_layers, d, d3 = ada_w.shape
    b = c.shape[0]
    tn = 1024
    return pl.pallas_call(
        _mod_kernel,
        grid=(n_layers, d3 // tn),
        in_specs=[
            pl.BlockSpec((b, d), lambda l, j: (0, 0)),
            pl.BlockSpec((1, d, tn), lambda l, j: (l, 0, j)),
            pl.BlockSpec((1, 1, tn), lambda l, j: (l, 0, j)),
        ],
        out_specs=pl.BlockSpec((1, b, tn), lambda l, j: (l, 0, j)),
        out_shape=jax.ShapeDtypeStruct((n_layers, b, d3), F32),
        compiler_params=pltpu.CompilerParams(
            dimension_semantics=("arbitrary", "arbitrary"), vmem_limit_bytes=VMEM_LIMIT_BYTES),
        name="adaln_mod",
    )(c, ada_w, ada_b.reshape(n_layers, 1, d3))


def _cast_weight_once(w_ref, wbf_scr):
    rows = w_ref.shape[1]

    @pl.when((pl.program_id(0) == 0) & (pl.program_id(1) == 0))
    def _():
        def body(i, carry):
            r0 = pl.multiple_of(i * WEIGHT_CAST_ROWS, WEIGHT_CAST_ROWS)
            wbf_scr[pl.ds(r0, WEIGHT_CAST_ROWS), :] = w_ref[0, pl.ds(r0, WEIGHT_CAST_ROWS), :].astype(BF16)
            return carry
        lax.fori_loop(0, rows // WEIGHT_CAST_ROWS, body, 0)


def _inproj_math(x, mod_ref, nw_ref, w_ref, out_refs):
    ms = jnp.mean(x * x, axis=-1, keepdims=True)
    y = x * lax.rsqrt(ms + EPS) * nw_ref[...]
    shift = mod_ref[0, 0:1, :]
    scale = mod_ref[0, 1:2, :]
    h = (y * (1.0 + scale) + shift).astype(BF16)
    for j, ref in enumerate(out_refs):
        r = jnp.dot(h, w_ref[:, j * D_ATTN:(j + 1) * D_ATTN], preferred_element_type=F32)
        if j == 0:
            r = r * (HEAD_DIM ** -0.5 * LOG2E)
        if j in FEATURE_MAJOR_OUTPUTS:
            r = r.T
        ref[0] = r.astype(ref.dtype)


def _inproj_kernel(x_ref, mod_ref, nw_ref, wf_ref, q_ref, k_ref, v_ref, za_ref, xl_ref, zl_ref,
                   w_ref):
    _cast_weight_once(wf_ref, w_ref)
    _inproj_math(x_ref[0], mod_ref, nw_ref, w_ref, (q_ref, k_ref, v_ref, za_ref, xl_ref, zl_ref))


def _inproj_outputs(b, s, tm=ROW_TILE):
    token_major = pl.BlockSpec((1, tm, D_ATTN), lambda i, j: (i, j, 0))
    feature_major = pl.BlockSpec((1, D_ATTN, tm), lambda i, j: (i, 0, j))
    dtypes = (BF16, BF16, BF16, BF16, F32, BF16)
    specs, shapes = [], []
    for j, dt in enumerate(dtypes):
        fm = j in FEATURE_MAJOR_OUTPUTS
        specs.append(feature_major if fm else token_major)
        shapes.append(jax.ShapeDtypeStruct((b, D_ATTN, s) if fm else (b, s, D_ATTN), dt))
    return specs, shapes


def _inproj_call(x, mod3, norm_w, w_in, layer):
    b, s, d = x.shape
    tm = IN_ROW_TILE
    out_specs, out_shape = _inproj_outputs(b, s, tm)
    return pl.pallas_call(
        _inproj_kernel,
        grid=(b, s // tm),
        in_specs=[
            pl.BlockSpec((1, tm, d), lambda i, j: (i, j, 0)),
            pl.BlockSpec((1, 3, d), lambda i, j: (i, 0, 0)),
            pl.BlockSpec((1, d), lambda i, j: (0, 0)),
            pl.BlockSpec((1, d, D_IN), lambda i, j: (layer, 0, 0), pipeline_mode=pl.Buffered(1)),
        ],
        out_specs=out_specs,
        out_shape=out_shape,
        scratch_shapes=[pltpu.VMEM((d, D_IN), BF16)],
        compiler_params=pltpu.CompilerParams(
            dimension_semantics=("arbitrary", "arbitrary"), vmem_limit_bytes=VMEM_LIMIT_BYTES),
        name="inproj",
    )(x, mod3, norm_w.reshape(1, d), w_in)


def _mixer_kernel(slopes_ref, q_ref, k_ref, v_ref, kext_ref, *rest):
    n_slabs = k_ref.shape[2] // LANES
    xl_refs = rest[:n_slabs]
    (cw_ref, cb_ref, wg_ref, ab_ref, xb_ref, lam_ref, o_ref, ol_ref,
     kaug_scr, qaug_scr, vaug_scr, s_scr, p_scr, out_scr,
     head_scr, h_scr, xc_scr, g_scr) = rest[n_slabs:]
    s_len = k_ref.shape[1]
    n_blk = s_len // MOBA_BLOCK
    n_pairs = k_ref.shape[2] // LANES
    n_heads = n_pairs * HEADS_PER_PAIR
    first_head = pl.program_id(0) * n_heads

    prow = lax.broadcasted_iota(jnp.int32, (n_blk, s_len), 0)
    pcol = lax.broadcasted_iota(jnp.int32, (n_blk, s_len), 1)
    block_mean = jnp.where(pcol // MOBA_BLOCK == prow, 1.0 / MOBA_BLOCK, 0.0).astype(BF16)
    klane = lax.broadcasted_iota(jnp.int32, (s_len, LANES), 1)
    mlane = lax.broadcasted_iota(jnp.int32, (n_blk, LANES), 1)
    n_iota = prow
    q_blk = pcol // MOBA_BLOCK
    key_off = lax.broadcasted_iota(jnp.int32, (MOBA_BLOCK, MOBA_BLOCK), 0)
    qry_off = lax.broadcasted_iota(jnp.int32, (MOBA_BLOCK, MOBA_BLOCK), 1)
    causal = key_off <= qry_off

    def prepare(pp):
        lanes = slice(pp * LANES, (pp + 1) * LANES)
        kp = k_ref[0, :, lanes]
        q_t = q_ref[0, lanes, :]
        v_t = v_ref[0, lanes, :]
        kmean = jnp.dot(block_mean, kp, preferred_element_type=F32)
        terms = []
        for hd in range(HEADS_PER_PAIR):
            rest = jnp.where(mlane // HEAD_DIM == hd, kmean, 0.0)
            for _ in range(3):
                part = rest.astype(BF16).astype(F32)
                terms.append(part)
                rest = rest - part
        terms = jnp.concatenate(terms, axis=0).astype(BF16)
        half = s_len // 2
        gates = jnp.concatenate(
            [jnp.dot(terms, q_t[:, :half], preferred_element_type=F32),
             jnp.dot(terms, q_t[:, half:], preferred_element_type=F32)], axis=1)
        for hd in range(HEADS_PER_PAIR):
            h = pp * HEADS_PER_PAIR + hd
            kaug_scr[h] = jnp.where(klane // HEAD_DIM == hd, kp, kext_ref[h])
            slope2 = slopes_ref[first_head + h] * LOG2E
            g_hi, g_mid, g_lo = (gates[(3 * hd + t) * n_blk:(3 * hd + t + 1) * n_blk]
                                 for t in range(3))
            g_t = g_hi + g_mid + g_lo
            rank = jnp.zeros((n_blk, s_len), jnp.int32)
            for m in range(n_blk):
                gm = g_t[m:m + 1, :]
                beats = (gm > g_t) | ((gm == g_t) & (m < n_iota))
                rank = rank + jnp.where(beats & (m < q_blk), 1, 0)
            sel = ((n_iota < q_blk) & (rank < MOBA_TOPK)) | (n_iota == q_blk)
            bias = _split_bf16(slope2 * ((n_iota - q_blk) * MOBA_BLOCK).astype(F32))
            selb = [jnp.where(sel, part, NEG_BIG if t == 0 else 0.0)
                    for t, part in enumerate(bias)]
            flag = jnp.where(n_iota < BIAS_TERMS, 1.0, 0.0)
            pad = jnp.zeros((HEAD_DIM - (BIAS_TERMS + 1) * n_blk, s_len), F32)
            ext = jnp.concatenate(selb + [flag, pad], axis=0)
            own = slice(hd * HEAD_DIM, (hd + 1) * HEAD_DIM)
            other = slice((1 - hd) * HEAD_DIM, (2 - hd) * HEAD_DIM)
            qaug_scr[h, own, :] = q_t[own]
            qaug_scr[h, other, :] = ext.astype(BF16)
            vaug_scr[h, 0:HEAD_DIM, :] = v_t[own]
            vaug_scr[h, HEAD_DIM:, :] = jnp.ones((2 * SUBLANES, s_len), BF16)

    items = [(h, qb) for h in range(n_heads)
             for qb in (range(n_blk) if h % 2 == 0 else reversed(range(n_blk)))]
    per_pair = HEADS_PER_PAIR * n_blk

    def scores(i):
        h, qb = items[i]
        slot = i % SCORE_SLOTS
        q0, kk = qb * MOBA_BLOCK, (qb + 1) * MOBA_BLOCK
        qa = qaug_scr[h, :, q0:kk]
        m = None
        for r0 in range(0, kk, MOBA_BLOCK):
            s = jnp.dot(kaug_scr[h, r0:r0 + MOBA_BLOCK, :], qa, preferred_element_type=F32)
            if r0 == q0:
                s = jnp.where(causal, s, NEG_BIG)
            s_scr[slot, r0:r0 + MOBA_BLOCK, :] = s
            mh = jnp.max(s.reshape(MOBA_BLOCK // SUBLANES, SUBLANES, MOBA_BLOCK), axis=0)
            m = mh if m is None else jnp.maximum(m, mh)
        return jnp.max(m, axis=0, keepdims=True)

    def probs(i, m):
        _, qb = items[i]
        kk = (qb + 1) * MOBA_BLOCK
        p_scr[i % 2, 0:kk, :] = jnp.exp2(s_scr[i % SCORE_SLOTS, 0:kk, :] - m).astype(BF16)

    def weighted(i):
        h, qb = items[i]
        slot = i % 2
        q0, kk = qb * MOBA_BLOCK, (qb + 1) * MOBA_BLOCK
        o = None
        for r0, r1 in ((0, kk // 2), (kk // 2, kk)):
            part = jnp.dot(vaug_scr[h, :, r0:r1], p_scr[slot, r0:r1, :], preferred_element_type=F32)
            o = part if o is None else o + part
        out_scr[h * HEAD_DIM:(h + 1) * HEAD_DIM, q0:kk] = o[0:HEAD_DIM] / o[HEAD_DIM:HEAD_DIM + 1]

    lru_units = []
    for sl in range(n_slabs):
        lanes = slice(sl * LANES, (sl + 1) * LANES)

        def store_h(value, lanes=lanes):
            ol_ref[0, :, lanes] = value

        lru_units += _lru_units(
            xl_refs[sl].at[0], cw_ref[:, lanes], cb_ref[:, lanes], wg_ref.at[sl], ab_ref[:, lanes],
            xb_ref[:, lanes], lam_ref[:, lanes], head_scr.at[sl], h_scr.at[sl], xc_scr.at[sl],
            g_scr.at[sl], store_h)
    lru_done = 0

    n_items = len(items)
    prepare(0)
    col_max = {i: scores(i) for i in range(SCORE_SLOTS)}
    probs(0, col_max.pop(0))
    for i in range(n_items):
        pp, within = divmod(i, per_pair)
        if within == PREPARE_AT and pp + 1 < n_pairs:
            prepare(pp + 1)
        if i + SCORE_SLOTS < n_items:
            col_max[i + SCORE_SLOTS] = scores(i + SCORE_SLOTS)
        if i + 1 < n_items:
            probs(i + 1, col_max.pop(i + 1))
        weighted(i)
        if within == per_pair - 1:
            lanes = slice(pp * LANES, (pp + 1) * LANES)
            o_ref[0, :, lanes] = out_scr[lanes, :].T.astype(o_ref.dtype)
        lru_target = (i + 1) * len(lru_units) // n_items
        while lru_done < lru_target:
            lru_units[lru_done]()
            lru_done += 1


def _alibi_slopes():
    return np.exp2(-8.0 * (np.arange(N_HEADS, dtype=np.float32) + 1.0) / N_HEADS).astype(np.float32)


def _key_side_table(s_len):
    n_blk = s_len // MOBA_BLOCK
    head = np.arange(N_HEADS)[:, None, None]
    pos = np.arange(s_len)[None, :, None]
    lane = np.arange(LANES)[None, None, :]
    sub = lane % HEAD_DIM
    foreign = lane // HEAD_DIM != head % HEADS_PER_PAIR
    onehot = (sub % n_blk == pos // MOBA_BLOCK).astype(np.float32)
    off = ((_alibi_slopes() * np.float32(LOG2E))[:, None, None]
           * (pos % MOBA_BLOCK).astype(np.float32)).astype(np.float32)
    ext = np.where(sub < BIAS_TERMS * n_blk, onehot, 0.0).astype(np.float32)
    rest = off
    for t in range(BIAS_TERMS):
        part = (rest.view(np.uint32) & np.uint32(0xFFFF0000)).view(np.float32) if t + 1 < BIAS_TERMS else rest
        ext = np.where(sub == BIAS_TERMS * n_blk + t, part, ext)
        rest = rest - part
    table = np.where(foreign, ext, 0.0).astype(np.float32)
    return jnp.asarray(table.astype(BF16))


def _mixer_call(slopes, kext, q_t, k, v_t, xl, conv_w, conv_b, w_gate, a_b, x_b, lam):
    b, s, _ = k.shape
    pairs = ATTN_PAIRS_PER_STEP
    width = pairs * LANES
    heads = pairs * HEADS_PER_PAIR
    row = lambda a: a.reshape(1, D_LRU)
    spec = pl.BlockSpec((1, s, width), lambda j, i, sl: (i, 0, j))
    spec_t = pl.BlockSpec((1, width, s), lambda j, i, sl: (i, j, 0))
    vec = pl.BlockSpec((1, width), lambda j, i, sl: (0, j))
    slab_specs = [pl.BlockSpec((1, s, LANES), lambda j, i, sl, n=n: (i, 0, j * pairs + n))
                  for n in range(pairs)]
    sub_len = LRU_SEGMENT * SUBLANES
    return pl.pallas_call(
        _mixer_kernel,
        grid_spec=pltpu.PrefetchScalarGridSpec(
            num_scalar_prefetch=1,
            grid=(D_ATTN // width, b),
            in_specs=[spec_t, spec, spec_t,
                      pl.BlockSpec((heads, s, LANES), lambda j, i, sl: (j, 0, 0),
                                   pipeline_mode=pl.Buffered(1))]
            + slab_specs
            + [pl.BlockSpec((CONV_WIDTH, width), lambda j, i, sl: (0, j)), vec,
               pl.BlockSpec((pairs, LANES, 2 * LANES), lambda j, i, sl: (j, 0, 0)), vec, vec, vec],
            out_specs=[spec, spec],
            scratch_shapes=[
                pltpu.VMEM((heads, s, LANES), BF16),
                pltpu.VMEM((heads, LANES, s), BF16),
                pltpu.VMEM((heads, HEAD_DIM + 2 * SUBLANES, s), BF16),
                pltpu.VMEM((SCORE_SLOTS, s, MOBA_BLOCK), F32),
                pltpu.VMEM((2, s, MOBA_BLOCK), BF16),
                pltpu.VMEM((width, s), F32),
                pltpu.VMEM((pairs, SUBLANES + sub_len, LANES), F32),
                pltpu.VMEM((pairs, s, LANES), F32),
                pltpu.VMEM((pairs, 2, LRU_CHUNK, LANES), F32),
                pltpu.VMEM((pairs, 2, LRU_CHUNK, 2 * LANES), F32),
            ],
        ),
        out_shape=[jax.ShapeDtypeStruct((b, s, D_ATTN), BF16), jax.ShapeDtypeStruct((b, s, D_LRU), BF16)],
        compiler_params=pltpu.CompilerParams(
            dimension_semantics=("arbitrary", "arbitrary"), vmem_limit_bytes=VMEM_LIMIT_BYTES),
        name="mixer",
    )(slopes, q_t, k, v_t, kext, *([xl] * pairs), conv_w, row(conv_b), w_gate, row(a_b), row(x_b),
      row(lam))


def _lru_units(x_seq, cw, cb, w_gate, a_b, x_b, lam, head_scr, h_scr, xc_scr, g_scr, store_h):
    s_len = x_seq.shape[0]
    seg = LRU_SEGMENT
    sub_len = seg * SUBLANES
    n_sub = LRU_CHUNK // sub_len
    n_chunks = s_len // LRU_CHUNK

    neg_lam = -lam
    softplus = jnp.maximum(neg_lam, 0.0) + jnp.log1p(jnp.exp(-jnp.abs(neg_lam)))
    half_rate = (0.5 * RG_C) * softplus
    rate_log2 = -LOG2E * half_rate
    ab_half = 0.5 * a_b
    xb_half = 0.5 * x_b
    cw_half = 0.5 * cw
    cb_half = 0.5 * cb
    sub = lax.broadcasted_iota(jnp.int32, (SUBLANES, LANES), 0)
    steps = (1, 2, 4)
    in_vreg = [sub >= step for step in steps]
    state = {"h": jnp.zeros((SUBLANES, LANES), F32)}

    def strided(ref, start):
        return ref[pl.ds(start, SUBLANES, stride=seg), :]

    def init():
        head_scr[0:SUBLANES, :] = jnp.zeros((SUBLANES, LANES), F32)
        head_scr[SUBLANES:, :] = x_seq[0:sub_len, :]

    def conv_and_gates(ci):
        xc_parts = []
        for c in range(n_sub):
            if ci == 0 and c == 0:
                src, t0 = head_scr, SUBLANES
            else:
                src, t0 = x_seq, ci * LRU_CHUNK
            cache = {}
            for g in range(seg):
                acc = cb_half
                for j in range(CONV_WIDTH):
                    off = c * sub_len + g - (CONV_WIDTH - 1) + j
                    if off not in cache:
                        cache[off] = strided(src, t0 + off)
                    acc = acc + cw_half[j:j + 1, :] * cache[off]
                xc_parts.append(acc)
        hx = jnp.concatenate(xc_parts, axis=0)
        xc_scr[ci % 2] = hx
        g_scr[ci % 2] = jnp.dot(hx.astype(BF16), w_gate[...], preferred_element_type=F32)

    def recurrence(ci, c):
        t0 = ci * LRU_CHUNK
        rows = slice(c * sub_len, (c + 1) * sub_len)
        h_in = state["h"]
        hx = xc_scr[ci % 2, rows, :]
        t_r = jnp.tanh(g_scr[ci % 2, rows, :LANES] + ab_half)
        t_i = jnp.tanh(g_scr[ci % 2, rows, LANES:] + xb_half)
        two_r = t_r + 1.0
        neg_log_a = half_rate * two_r
        a = jnp.exp2(rate_log2 * two_r)
        y = jnp.tanh(neg_log_a) * (a * a + 1.0)
        root = jnp.where(y == 0.0, 0.0, y * lax.rsqrt(y))
        bterm = root * ((t_i + 1.0) * hx)

        comp = []
        for g in range(seg):
            av, bv = a[g * SUBLANES:(g + 1) * SUBLANES], bterm[g * SUBLANES:(g + 1) * SUBLANES]
            if g > 0:
                bv = av * comp[-1][1] + bv
                av = av * comp[-1][0]
            comp.append((av, bv))
        pa, pb = comp[-1]
        for step, valid in zip(steps, in_vreg):
            a_sh = jnp.where(valid, pltpu.roll(pa, step, axis=0), 1.0)
            b_sh = jnp.where(valid, pltpu.roll(pb, step, axis=0), 0.0)
            pb = pa * b_sh + pb
            pa = pa * a_sh
        ea = jnp.where(in_vreg[0], pltpu.roll(pa, 1, axis=0), 1.0)
        eb = jnp.where(in_vreg[0], pltpu.roll(pb, 1, axis=0), 0.0)
        seg_in = ea * h_in + eb
        for g in range(seg):
            h_g = comp[g][0] * seg_in + comp[g][1]
            h_scr[pl.ds(t0 + c * sub_len + g, SUBLANES, stride=seg), :] = h_g
        h_last = pa * h_in + pb
        state["h"] = jnp.broadcast_to(h_last[SUBLANES - 1:SUBLANES, :], (SUBLANES, LANES))

    def finish():
        store_h(h_scr[...].astype(BF16))

    units = [init, lambda: conv_and_gates(0)]
    for ci in range(n_chunks):
        for c in range(n_sub):
            units.append(lambda ci=ci, c=c: recurrence(ci, c))
            if c == n_sub // 2 - 1 and ci + 1 < n_chunks:
                units.append(lambda ci=ci: conv_and_gates(ci + 1))
    units.append(finish)
    return units


def _gated_norm(y_ref, z_ref, nw_ref, rows):
    y = y_ref[0, rows, :].astype(F32)
    ms = jnp.mean(y * y, axis=-1, keepdims=True)
    return ((y * lax.rsqrt(ms + EPS) * nw_ref[...]) * _silu(z_ref[0, rows, :].astype(F32))).astype(BF16)


def _outproj_math(ya_ref, za_ref, yl_ref, zl_ref, x_ref, mod_ref, anw_ref, lnw_ref, w_ref,
                  rows=slice(None)):
    y = jnp.dot(_gated_norm(ya_ref, za_ref, anw_ref, rows), w_ref[0:D_ATTN, :],
                preferred_element_type=F32)
    y = y + jnp.dot(_gated_norm(yl_ref, zl_ref, lnw_ref, rows), w_ref[D_ATTN:, :],
                    preferred_element_type=F32)
    return x_ref[0, rows, :] + mod_ref[0, 2:3, :] * y


def _outproj_kernel(ya_ref, za_ref, yl_ref, zl_ref, x_ref, mod_ref, anw_ref, lnw_ref, wf_ref, fnw_ref,
                    o_ref, w_ref):
    _cast_weight_once(wf_ref, w_ref)
    for r0 in range(0, x_ref.shape[1], OUT_SUB_ROWS):
        rows = slice(r0, r0 + OUT_SUB_ROWS)
        out = _outproj_math(ya_ref, za_ref, yl_ref, zl_ref, x_ref, mod_ref, anw_ref, lnw_ref, w_ref,
                            rows)
        ms = jnp.mean(out * out, axis=-1, keepdims=True)
        o_ref[0, rows, :] = out * lax.rsqrt(ms + EPS) * fnw_ref[...]


def _outin_kernel(ya_ref, za_ref, yl_ref, zl_ref, x_ref, mod_ref, anw_ref, lnw_ref, wof_ref,
                  modn_ref, nwn_ref, wif_ref,
                  xo_ref, q_ref, k_ref, v_ref, zao_ref, xlo_ref, zlo_ref, wo_ref, wi_ref):
    _cast_weight_once(wof_ref, wo_ref)
    _cast_weight_once(wif_ref, wi_ref)
    out = _outproj_math(ya_ref, za_ref, yl_ref, zl_ref, x_ref, mod_ref, anw_ref, lnw_ref, wo_ref)
    xo_ref[0] = out
    _inproj_math(out, modn_ref, nwn_ref, wi_ref, (q_ref, k_ref, v_ref, zao_ref, xlo_ref, zlo_ref))


def _proj_specs(d, layer, tm=ROW_TILE):
    return dict(
        half=pl.BlockSpec((1, tm, D_ATTN), lambda i, j: (i, j, 0)),
        full=pl.BlockSpec((1, tm, d), lambda i, j: (i, j, 0)),
        mod=pl.BlockSpec((1, 3, d), lambda i, j: (i, 0, 0)),
        nvec=pl.BlockSpec((1, D_ATTN), lambda i, j: (0, 0)),
        dvec=pl.BlockSpec((1, d), lambda i, j: (0, 0)),
        w_out=pl.BlockSpec((1, d, d), lambda i, j: (layer, 0, 0), pipeline_mode=pl.Buffered(1)),
        w_in=pl.BlockSpec((1, d, D_IN), lambda i, j: (layer + 1, 0, 0), pipeline_mode=pl.Buffered(1)),
    )


def _outproj_call(ya, za, yl, zl, x, mod3, anw, lnw, w_out, layer, fnw):
    b, s, d = x.shape
    sp = _proj_specs(d, layer, OUT_ROW_TILE)
    return pl.pallas_call(
        _outproj_kernel,
        grid=(b, s // OUT_ROW_TILE),
        in_specs=[sp["half"]] * 4 + [sp["full"], sp["mod"], sp["nvec"], sp["nvec"], sp["w_out"],
                                     sp["dvec"]],
        out_specs=sp["full"],
        out_shape=jax.ShapeDtypeStruct((b, s, d), F32),
        scratch_shapes=[pltpu.VMEM((d, d), BF16)],
        compiler_params=pltpu.CompilerParams(
            dimension_semantics=("arbitrary", "arbitrary"), vmem_limit_bytes=VMEM_LIMIT_BYTES),
        name="outproj",
    )(ya, za, yl, zl, x, mod3, anw.reshape(1, D_ATTN), lnw.reshape(1, D_LRU), w_out,
      fnw.reshape(1, d))


def _outin_call(ya, za, yl, zl, x, mod3, anw, lnw, w_out, layer, mod3_next, norm_w_next, w_in):
    b, s, d = x.shape
    sp = _proj_specs(d, layer)
    proj_specs, proj_shapes = _inproj_outputs(b, s)
    return pl.pallas_call(
        _outin_kernel,
        grid=(b, s // ROW_TILE),
        in_specs=[sp["half"]] * 4 + [sp["full"], sp["mod"], sp["nvec"], sp["nvec"], sp["w_out"],
                                     sp["mod"], sp["dvec"], sp["w_in"]],
        out_specs=[sp["full"]] + proj_specs,
        out_shape=[jax.ShapeDtypeStruct((b, s, d), F32)] + proj_shapes,
        scratch_shapes=[pltpu.VMEM((d, d), BF16), pltpu.VMEM((d, D_IN), BF16)],
        compiler_params=pltpu.CompilerParams(
            dimension_semantics=("arbitrary", "arbitrary"), vmem_limit_bytes=PROJ_VMEM_LIMIT_BYTES),
        name="outproj_inproj",
    )(ya, za, yl, zl, x, mod3, anw.reshape(1, D_ATTN), lnw.reshape(1, D_LRU), w_out,
      mod3_next, norm_w_next.reshape(1, d), w_in)


def _gate_weights(a_w, x_w):
    per_slab = LANES // LRU_BLOCK
    slabs = []
    for j in range(D_LRU // LANES):
        blocks = range(j * per_slab, (j + 1) * per_slab)
        diag = lambda w: jax.scipy.linalg.block_diag(*[w[g] for g in blocks])
        slabs.append(jnp.concatenate([diag(a_w), diag(x_w)], axis=1))
    return jnp.stack(slabs).astype(BF16)


def kernel(x, c, norm_w, ada_w, ada_b, w_in, conv_w, conv_b, rg_a_w, rg_a_b, rg_x_w, rg_x_b,
           rg_lambda, attn_out_norm_w, lru_out_norm_w, w_out, final_norm_w):
    n_layers = ada_w.shape[0]
    b = x.shape[0]
    mod = _mod_call(c, ada_w, ada_b)
    slopes = jnp.asarray(_alibi_slopes())
    kext = _key_side_table(x.shape[1])
    mod3 = [mod[l].reshape(b, 3, D_MODEL) for l in range(n_layers)]
    q, k, v, za, xl, zl = _inproj_call(x, mod3[0], norm_w[0], w_in, 0)
    for l in range(n_layers):
        ya, yl = _mixer_call(slopes, kext, q, k, v, xl, conv_w[l], conv_b[l],
                             _gate_weights(rg_a_w[l], rg_x_w[l]), rg_a_b[l], rg_x_b[l], rg_lambda[l])
        if l + 1 < n_layers:
            x, q, k, v, za, xl, zl = _outin_call(
                ya, za, yl, zl, x, mod3[l], attn_out_norm_w[l], lru_out_norm_w[l], w_out, l,
                mod3[l + 1], norm_w[l + 1], w_in)
        else:
            x = _outproj_call(ya, za, yl, zl, x, mod3[l], attn_out_norm_w[l], lru_out_norm_w[l],
                              w_out, l, final_norm_w)
    return x
```

```python
import functools
import math

import jax
import jax.numpy as jnp
import numpy as np
from jax import lax
from jax.experimental import pallas as pl
from jax.experimental.pallas import tpu as pltpu

D_MODEL = 1024
D_ATTN = 512
D_LRU = 512
HEAD_DIM = 64
N_HEADS = D_ATTN // HEAD_DIM
N_LRU_BLOCKS = 8
LRU_BLOCK = D_LRU // N_LRU_BLOCKS
CONV_WIDTH = 4
RG_C = 8.0
MOBA_BLOCK = 256
MOBA_TOPK = 3
D_IN = 4 * D_ATTN + 2 * D_LRU
EPS = 1e-6

F32 = jnp.float32
BF16 = jnp.bfloat16
NEG_BIG = -1e30
LOG2E = math.log2(math.e)

LANES = 128
SUBLANES = 8
VMEM_LIMIT_BYTES = 48 * 1024 * 1024
PROJ_VMEM_LIMIT_BYTES = 56 * 1024 * 1024

ROW_TILE = 512
IN_ROW_TILE = 1024
OUT_ROW_TILE = 1024
WEIGHT_CAST_ROWS = 64
LRU_CHUNK = 256
LRU_SEGMENT = 4
HEADS_PER_PAIR = LANES // HEAD_DIM
BIAS_TERMS = 3
FEATURE_MAJOR_OUTPUTS = (0, 2)
ATTN_PAIRS_PER_STEP = 2
SCORE_SLOTS = 3
PREPARE_AT = 6


def _sigmoid(v):
    return 0.5 * jnp.tanh(0.5 * v) + 0.5


def _silu(v):
    return v * _sigmoid(v)


def _split_bf16(v):
    parts = []
    for _ in range(BIAS_TERMS):
        part = v.astype(BF16).astype(F32)
        parts.append(part)
        v = v - part
    return parts


def _mod_kernel(c_ref, w_ref, b_ref, o_ref):
    s = _silu(c_ref[...]).astype(BF16)
    w = w_ref[0].astype(BF16)
    o_ref[0] = jnp.dot(s, w, preferred_element_type=F32) + b_ref[0]


def _mod_call(c, ada_w, ada_b):
    n_layers, d, d3 = ada_w.shape
    b = c.shape[0]
    tn = 1024
    return pl.pallas_call(
        _mod_kernel,
        grid=(n_layers, d3 // tn),
        in_specs=[
            pl.BlockSpec((b, d), lambda l, j: (0, 0)),
            pl.BlockSpec((1, d, tn), lambda l, j: (l, 0, j)),
            pl.BlockSpec((1, 1, tn), lambda l, j: (l, 0, j)),
        ],
        out_specs=pl.BlockSpec((1, b, tn), lambda l, j: (l, 0, j)),
        out_shape=jax.ShapeDtypeStruct((n_layers, b, d3), F32),
        compiler_params=pltpu.CompilerParams(
            dimension_semantics=("arbitrary", "arbitrary"), vmem_limit_bytes=VMEM_LIMIT_BYTES),
        name="adaln_mod",
    )(c, ada_w, ada_b.reshape(n_layers, 1, d3))


def _cast_weight_once(w_ref, wbf_scr):
    rows = w_ref.shape[1]

    @pl.when((pl.program_id(0) == 0) & (pl.program_id(1) == 0))
    def _():
        def body(i, carry):
            r0 = pl.multiple_of(i * WEIGHT_CAST_ROWS, WEIGHT_CAST_ROWS)
            wbf_scr[pl.ds(r0, WEIGHT_CAST_ROWS), :] = w_ref[0, pl.ds(r0, WEIGHT_CAST_ROWS), :].astype(BF16)
            return carry
        lax.fori_loop(0, rows // WEIGHT_CAST_ROWS, body, 0)


def _inproj_math(x, mod_ref, nw_ref, w_ref, out_refs):
    ms = jnp.mean(x * x, axis=-1, keepdims=True)
    y = x * lax.rsqrt(ms + EPS) * nw_ref[...]
    shift = mod_ref[0, 0:1, :]
    scale = mod_ref[0, 1:2, :]
    h = (y * (1.0 + scale) + shift).astype(BF16)
    for j, ref in enumerate(out_refs):
        r = jnp.dot(h, w_ref[:, j * D_ATTN:(j + 1) * D_ATTN], preferred_element_type=F32)
        if j == 0:
            r = r * (HEAD_DIM ** -0.5 * LOG2E)
        if j in FEATURE_MAJOR_OUTPUTS:
            r = r.T
        ref[0] = r.astype(ref.dtype)


def _inproj_kernel(x_ref, mod_ref, nw_ref, wf_ref, q_ref, k_ref, v_ref, za_ref, xl_ref, zl_ref,
                   w_ref):
    _cast_weight_once(wf_ref, w_ref)
    _inproj_math(x_ref[0], mod_ref, nw_ref, w_ref, (q_ref, k_ref, v_ref, za_ref, xl_ref, zl_ref))


def _inproj_outputs(b, s, tm=ROW_TILE):
    token_major = pl.BlockSpec((1, tm, D_ATTN), lambda i, j: (i, j, 0))
    feature_major = pl.BlockSpec((1, D_ATTN, tm), lambda i, j: (i, 0, j))
    dtypes = (BF16, BF16, BF16, BF16, F32, BF16)
    specs, shapes = [], []
    for j, dt in enumerate(dtypes):
        fm = j in FEATURE_MAJOR_OUTPUTS
        specs.append(feature_major if fm else token_major)
        shapes.append(jax.ShapeDtypeStruct((b, D_ATTN, s) if fm else (b, s, D_ATTN), dt))
    return specs, shapes


def _inproj_call(x, mod3, norm_w, w_in, layer):
    b, s, d = x.shape
    tm = IN_ROW_TILE
    out_specs, out_shape = _inproj_outputs(b, s, tm)
    return pl.pallas_call(
        _inproj_kernel,
        grid=(b, s // tm),
        in_specs=[
            pl.BlockSpec((1, tm, d), lambda i, j: (i, j, 0)),
            pl.BlockSpec((1, 3, d), lambda i, j: (i, 0, 0)),
            pl.BlockSpec((1, d), lambda i, j: (0, 0)),
            pl.BlockSpec((1, d, D_IN), lambda i, j: (layer, 0, 0), pipeline_mode=pl.Buffered(1)),
        ],
        out_specs=out_specs,
        out_shape=out_shape,
        scratch_shapes=[pltpu.VMEM((d, D_IN), BF16)],
        compiler_params=pltpu.CompilerParams(
            dimension_semantics=("arbitrary", "arbitrary"), vmem_limit_bytes=VMEM_LIMIT_BYTES),
        name="inproj",
    )(x, mod3, norm_w.reshape(1, d), w_in)


def _mixer_kernel(slopes_ref, q_ref, k_ref, v_ref, kext_ref, *rest, n_weights):
    n_slabs = k_ref.shape[2] // LANES
    xl_refs = rest[:n_slabs]
    rest = rest[n_slabs:]
    cw_ref, cb_ref, wg_ref, ab_ref, xb_ref, lam_ref = rest[:6]
    wf_refs = rest[6:6 + n_weights]
    o_ref, ol_ref = rest[6 + n_weights:8 + n_weights]
    wbf_refs = rest[8 + n_weights:8 + 2 * n_weights]
    (kaug_scr, qaug_scr, vaug_scr, s_scr, p_scr, out_scr,
     head_scr, h_scr, xc_scr, g_scr) = rest[8 + 2 * n_weights:]
    for wf_ref, wbf_ref in zip(wf_refs, wbf_refs):
        wbf_ref[...] = wf_ref[0].astype(BF16)
    s_len = k_ref.shape[1]
    n_blk = s_len // MOBA_BLOCK
    n_pairs = k_ref.shape[2] // LANES
    n_heads = n_pairs * HEADS_PER_PAIR
    first_head = pl.program_id(0) * n_heads

    prow = lax.broadcasted_iota(jnp.int32, (n_blk, s_len), 0)
    pcol = lax.broadcasted_iota(jnp.int32, (n_blk, s_len), 1)
    block_mean = jnp.where(pcol // MOBA_BLOCK == prow, 1.0 / MOBA_BLOCK, 0.0).astype(BF16)
    klane = lax.broadcasted_iota(jnp.int32, (s_len, LANES), 1)
    mlane = lax.broadcasted_iota(jnp.int32, (n_blk, LANES), 1)
    n_iota = prow
    q_blk = pcol // MOBA_BLOCK
    key_off = lax.broadcasted_iota(jnp.int32, (MOBA_BLOCK, MOBA_BLOCK), 0)
    qry_off = lax.broadcasted_iota(jnp.int32, (MOBA_BLOCK, MOBA_BLOCK), 1)
    causal = key_off <= qry_off

    def prepare(pp):
        lanes = slice(pp * LANES, (pp + 1) * LANES)
        kp = k_ref[0, :, lanes]
        q_t = q_ref[0, lanes, :]
        v_t = v_ref[0, lanes, :]
        kmean = jnp.dot(block_mean, kp, preferred_element_type=F32)
        terms = []
        for hd in range(HEADS_PER_PAIR):
            rest = jnp.where(mlane // HEAD_DIM == hd, kmean, 0.0)
            for _ in range(3):
                part = rest.astype(BF16).astype(F32)
                terms.append(part)
                rest = rest - part
        terms = jnp.concatenate(terms, axis=0).astype(BF16)
        half = s_len // 2
        gates = jnp.concatenate(
            [jnp.dot(terms, q_t[:, :half], preferred_element_type=F32),
             jnp.dot(terms, q_t[:, half:], preferred_element_type=F32)], axis=1)
        for hd in range(HEADS_PER_PAIR):
            h = pp * HEADS_PER_PAIR + hd
            kaug_scr[h] = jnp.where(klane // HEAD_DIM == hd, kp, kext_ref[h])
            slope2 = slopes_ref[first_head + h] * LOG2E
            g_hi, g_mid, g_lo = (gates[(3 * hd + t) * n_blk:(3 * hd + t + 1) * n_blk]
                                 for t in range(3))
            g_t = g_hi + g_mid + g_lo
            rank = jnp.zeros((n_blk, s_len), jnp.int32)
            for m in range(n_blk):
                gm = g_t[m:m + 1, :]
                beats = (gm > g_t) | ((gm == g_t) & (m < n_iota))
                rank = rank + jnp.where(beats & (m < q_blk), 1, 0)
            sel = ((n_iota < q_blk) & (rank < MOBA_TOPK)) | (n_iota == q_blk)
            bias = _split_bf16(slope2 * ((n_iota - q_blk) * MOBA_BLOCK).astype(F32))
            selb = [jnp.where(sel, part, NEG_BIG if t == 0 else 0.0)
                    for t, part in enumerate(bias)]
            flag = jnp.where(n_iota < BIAS_TERMS, 1.0, 0.0)
            pad = jnp.zeros((HEAD_DIM - (BIAS_TERMS + 1) * n_blk, s_len), F32)
            ext = jnp.concatenate(selb + [flag, pad], axis=0)
            own = slice(hd * HEAD_DIM, (hd + 1) * HEAD_DIM)
            other = slice((1 - hd) * HEAD_DIM, (2 - hd) * HEAD_DIM)
            qaug_scr[h, own, :] = q_t[own]
            qaug_scr[h, other, :] = ext.astype(BF16)
            vaug_scr[h, 0:HEAD_DIM, :] = v_t[own]
            vaug_scr[h, HEAD_DIM:, :] = jnp.ones((2 * SUBLANES, s_len), BF16)

    items = [(h, qb) for h in range(n_heads)
             for qb in (range(n_blk) if h % 2 == 0 else reversed(range(n_blk)))]
    per_pair = HEADS_PER_PAIR * n_blk

    def scores(i):
        h, qb = items[i]
        slot = i % SCORE_SLOTS
        q0, kk = qb * MOBA_BLOCK, (qb + 1) * MOBA_BLOCK
        qa = qaug_scr[h, :, q0:kk]
        m = None
        for r0 in range(0, kk, MOBA_BLOCK):
            s = jnp.dot(kaug_scr[h, r0:r0 + MOBA_BLOCK, :], qa, preferred_element_type=F32)
            if r0 == q0:
                s = jnp.where(causal, s, NEG_BIG)
            s_scr[slot, r0:r0 + MOBA_BLOCK, :] = s
            mh = jnp.max(s.reshape(MOBA_BLOCK // SUBLANES, SUBLANES, MOBA_BLOCK), axis=0)
            m = mh if m is None else jnp.maximum(m, mh)
        return jnp.max(m, axis=0, keepdims=True)

    def probs(i, m):
        _, qb = items[i]
        kk = (qb + 1) * MOBA_BLOCK
        p_scr[i % 2, 0:kk, :] = jnp.exp2(s_scr[i % SCORE_SLOTS, 0:kk, :] - m).astype(BF16)

    def weighted(i):
        h, qb = items[i]
        slot = i % 2
        q0, kk = qb * MOBA_BLOCK, (qb + 1) * MOBA_BLOCK
        o = None
        for r0, r1 in ((0, kk // 2), (kk // 2, kk)):
            part = jnp.dot(vaug_scr[h, :, r0:r1], p_scr[slot, r0:r1, :], preferred_element_type=F32)
            o = part if o is None else o + part
        out_scr[h * HEAD_DIM:(h + 1) * HEAD_DIM, q0:kk] = o[0:HEAD_DIM] / o[HEAD_DIM:HEAD_DIM + 1]

    lru_units = []
    for sl in range(n_slabs):
        lanes = slice(sl * LANES, (sl + 1) * LANES)

        def store_h(value, lanes=lanes):
            ol_ref[0, :, lanes] = value

        lru_units += _lru_units(
            xl_refs[sl].at[0], cw_ref[:, lanes], cb_ref[:, lanes], wg_ref.at[sl], ab_ref[:, lanes],
            xb_ref[:, lanes], lam_ref[:, lanes], head_scr.at[sl], h_scr.at[sl], xc_scr.at[sl],
            g_scr.at[sl], store_h)
    lru_done = 0

    n_items = len(items)
    prepare(0)
    col_max = {i: scores(i) for i in range(SCORE_SLOTS)}
    probs(0, col_max.pop(0))
    for i in range(n_items):
        pp, within = divmod(i, per_pair)
        if within == PREPARE_AT and pp + 1 < n_pairs:
            prepare(pp + 1)
        if i + SCORE_SLOTS < n_items:
            col_max[i + SCORE_SLOTS] = scores(i + SCORE_SLOTS)
        if i + 1 < n_items:
            probs(i + 1, col_max.pop(i + 1))
        weighted(i)
        if within == per_pair - 1:
            lanes = slice(pp * LANES, (pp + 1) * LANES)
            o_ref[0, :, lanes] = out_scr[lanes, :].T.astype(o_ref.dtype)
        lru_target = (i + 1) * len(lru_units) // n_items
        while lru_done < lru_target:
            lru_units[lru_done]()
            lru_done += 1


def _alibi_slopes():
    return np.exp2(-8.0 * (np.arange(N_HEADS, dtype=np.float32) + 1.0) / N_HEADS).astype(np.float32)


def _key_side_table(s_len):
    n_blk = s_len // MOBA_BLOCK
    head = np.arange(N_HEADS)[:, None, None]
    pos = np.arange(s_len)[None, :, None]
    lane = np.arange(LANES)[None, None, :]
    sub = lane % HEAD_DIM
    foreign = lane // HEAD_DIM != head % HEADS_PER_PAIR
    onehot = (sub % n_blk == pos // MOBA_BLOCK).astype(np.float32)
    off = ((_alibi_slopes() * np.float32(LOG2E))[:, None, None]
           * (pos % MOBA_BLOCK).astype(np.float32)).astype(np.float32)
    ext = np.where(sub < BIAS_TERMS * n_blk, onehot, 0.0).astype(np.float32)
    rest = off
    for t in range(BIAS_TERMS):
        part = (rest.view(np.uint32) & np.uint32(0xFFFF0000)).view(np.float32) if t + 1 < BIAS_TERMS else rest
        ext = np.where(sub == BIAS_TERMS * n_blk + t, part, ext)
        rest = rest - part
    table = np.where(foreign, ext, 0.0).astype(np.float32)
    return jnp.asarray(table.astype(BF16))


def _mixer_call(slopes, kext, q_t, k, v_t, xl, conv_w, conv_b, w_gate, a_b, x_b, lam, weights):
    b, s, _ = k.shape
    pairs = ATTN_PAIRS_PER_STEP
    width = pairs * LANES
    heads = pairs * HEADS_PER_PAIR
    row = lambda a: a.reshape(1, D_LRU)
    spec = pl.BlockSpec((1, s, width), lambda j, i, sl: (i, 0, j))
    spec_t = pl.BlockSpec((1, width, s), lambda j, i, sl: (i, j, 0))
    vec = pl.BlockSpec((1, width), lambda j, i, sl: (0, j))
    slab_specs = [pl.BlockSpec((1, s, LANES), lambda j, i, sl, n=n: (i, 0, j * pairs + n))
                  for n in range(pairs)]
    sub_len = LRU_SEGMENT * SUBLANES
    steps = (D_ATTN // width) * b
    w_specs, wbf_specs, wbf_shapes = [], [], []
    for w, layer in weights:
        rows, cols = w.shape[1] // steps, w.shape[2]
        w_specs.append(pl.BlockSpec((1, rows, cols), lambda j, i, sl, layer=layer: (layer, j * b + i, 0)))
        wbf_specs.append(pl.BlockSpec((rows, cols), lambda j, i, sl: (j * b + i, 0)))
        wbf_shapes.append(jax.ShapeDtypeStruct(w.shape[1:], BF16))
    return pl.pallas_call(
        functools.partial(_mixer_kernel, n_weights=len(weights)),
        grid_spec=pltpu.PrefetchScalarGridSpec(
            num_scalar_prefetch=1,
            grid=(D_ATTN // width, b),
            in_specs=[spec_t, spec, spec_t,
                      pl.BlockSpec((heads, s, LANES), lambda j, i, sl: (j, 0, 0),
                                   pipeline_mode=pl.Buffered(1))]
            + slab_specs
            + [pl.BlockSpec((CONV_WIDTH, width), lambda j, i, sl: (0, j)), vec,
               pl.BlockSpec((pairs, LANES, 2 * LANES), lambda j, i, sl: (j, 0, 0)), vec, vec, vec]
            + w_specs,
            out_specs=[spec, spec] + wbf_specs,
            scratch_shapes=[
                pltpu.VMEM((heads, s, LANES), BF16),
                pltpu.VMEM((heads, LANES, s), BF16),
                pltpu.VMEM((heads, HEAD_DIM + 2 * SUBLANES, s), BF16),
                pltpu.VMEM((SCORE_SLOTS, s, MOBA_BLOCK), F32),
                pltpu.VMEM((2, s, MOBA_BLOCK), BF16),
                pltpu.VMEM((width, s), F32),
                pltpu.VMEM((pairs, SUBLANES + sub_len, LANES), F32),
                pltpu.VMEM((pairs, s, LANES), F32),
                pltpu.VMEM((pairs, 2, LRU_CHUNK, LANES), F32),
                pltpu.VMEM((pairs, 2, LRU_CHUNK, 2 * LANES), F32),
            ],
        ),
        out_shape=[jax.ShapeDtypeStruct((b, s, D_ATTN), BF16), jax.ShapeDtypeStruct((b, s, D_LRU), BF16)]
        + wbf_shapes,
        compiler_params=pltpu.CompilerParams(
            dimension_semantics=("arbitrary", "arbitrary"), vmem_limit_bytes=VMEM_LIMIT_BYTES),
        name="mixer",
    )(slopes, q_t, k, v_t, kext, *([xl] * pairs), conv_w, row(conv_b), w_gate, row(a_b), row(x_b),
      row(lam), *[w for w, _ in weights])


def _lru_units(x_seq, cw, cb, w_gate, a_b, x_b, lam, head_scr, h_scr, xc_scr, g_scr, store_h):
    s_len = x_seq.shape[0]
    seg = LRU_SEGMENT
    sub_len = seg * SUBLANES
    n_sub = LRU_CHUNK // sub_len
    n_chunks = s_len // LRU_CHUNK

    neg_lam = -lam
    softplus = jnp.maximum(neg_lam, 0.0) + jnp.log1p(jnp.exp(-jnp.abs(neg_lam)))
    half_rate = (0.5 * RG_C) * softplus
    rate_log2 = -LOG2E * half_rate
    ab_half = 0.5 * a_b
    xb_half = 0.5 * x_b
    cw_half = 0.5 * cw
    cb_half = 0.5 * cb
    sub = lax.broadcasted_iota(jnp.int32, (SUBLANES, LANES), 0)
    steps = (1, 2, 4)
    in_vreg = [sub >= step for step in steps]
    state = {"h": jnp.zeros((SUBLANES, LANES), F32)}

    def strided(ref, start):
        return ref[pl.ds(start, SUBLANES, stride=seg), :]

    def init():
        head_scr[0:SUBLANES, :] = jnp.zeros((SUBLANES, LANES), F32)
        head_scr[SUBLANES:, :] = x_seq[0:sub_len, :]

    def conv_and_gates(ci):
        xc_parts = []
        for c in range(n_sub):
            if ci == 0 and c == 0:
                src, t0 = head_scr, SUBLANES
            else:
                src, t0 = x_seq, ci * LRU_CHUNK
            cache = {}
            for g in range(seg):
                acc = cb_half
                for j in range(CONV_WIDTH):
                    off = c * sub_len + g - (CONV_WIDTH - 1) + j
                    if off not in cache:
                        cache[off] = strided(src, t0 + off)
                    acc = acc + cw_half[j:j + 1, :] * cache[off]
                xc_parts.append(acc)
        hx = jnp.concatenate(xc_parts, axis=0)
        xc_scr[ci % 2] = hx
        g_scr[ci % 2] = jnp.dot(hx.astype(BF16), w_gate[...], preferred_element_type=F32)

    def recurrence(ci, c):
        t0 = ci * LRU_CHUNK
        rows = slice(c * sub_len, (c + 1) * sub_len)
        h_in = state["h"]
        hx = xc_scr[ci % 2, rows, :]
        t_r = jnp.tanh(g_scr[ci % 2, rows, :LANES] + ab_half)
        t_i = jnp.tanh(g_scr[ci % 2, rows, LANES:] + xb_half)
        two_r = t_r + 1.0
        neg_log_a = half_rate * two_r
        a = jnp.exp2(rate_log2 * two_r)
        y = jnp.tanh(neg_log_a) * (a * a + 1.0)
        root = jnp.where(y == 0.0, 0.0, y * lax.rsqrt(y))
        bterm = root * ((t_i + 1.0) * hx)

        comp = []
        for g in range(seg):
            av, bv = a[g * SUBLANES:(g + 1) * SUBLANES], bterm[g * SUBLANES:(g + 1) * SUBLANES]
            if g > 0:
                bv = av * comp[-1][1] + bv
                av = av * comp[-1][0]
            comp.append((av, bv))
        pa, pb = comp[-1]
        for step, valid in zip(steps, in_vreg):
            a_sh = jnp.where(valid, pltpu.roll(pa, step, axis=0), 1.0)
            b_sh = jnp.where(valid, pltpu.roll(pb, step, axis=0), 0.0)
            pb = pa * b_sh + pb
            pa = pa * a_sh
        ea = jnp.where(in_vreg[0], pltpu.roll(pa, 1, axis=0), 1.0)
        eb = jnp.where(in_vreg[0], pltpu.roll(pb, 1, axis=0), 0.0)
        seg_in = ea * h_in + eb
        for g in range(seg):
            h_g = comp[g][0] * seg_in + comp[g][1]
            h_scr[pl.ds(t0 + c * sub_len + g, SUBLANES, stride=seg), :] = h_g
        h_last = pa * h_in + pb
        state["h"] = jnp.broadcast_to(h_last[SUBLANES - 1:SUBLANES, :], (SUBLANES, LANES))

    def finish():
        store_h(h_scr[...].astype(BF16))

    units = [init, lambda: conv_and_gates(0)]
    for ci in range(n_chunks):
        for c in range(n_sub):
            units.append(lambda ci=ci, c=c: recurrence(ci, c))
            if c == n_sub // 2 - 1 and ci + 1 < n_chunks:
                units.append(lambda ci=ci: conv_and_gates(ci + 1))
    units.append(finish)
    return units


def _gated_norm(y_ref, z_ref, nw_ref):
    y = y_ref[0].astype(F32)
    ms = jnp.mean(y * y, axis=-1, keepdims=True)
    return ((y * lax.rsqrt(ms + EPS) * nw_ref[...]) * _silu(z_ref[0].astype(F32))).astype(BF16)


def _outproj_math(ya_ref, za_ref, yl_ref, zl_ref, x_ref, mod_ref, anw_ref, lnw_ref, w_ref):
    y = jnp.dot(_gated_norm(ya_ref, za_ref, anw_ref), w_ref[0:D_ATTN, :], preferred_element_type=F32)
    y = y + jnp.dot(_gated_norm(yl_ref, zl_ref, lnw_ref), w_ref[D_ATTN:, :],
                    preferred_element_type=F32)
    return x_ref[0] + mod_ref[0, 2:3, :] * y


def _outproj_kernel(ya_ref, za_ref, yl_ref, zl_ref, x_ref, mod_ref, anw_ref, lnw_ref, w_ref, fnw_ref,
                    o_ref):
    out = _outproj_math(ya_ref, za_ref, yl_ref, zl_ref, x_ref, mod_ref, anw_ref, lnw_ref, w_ref)
    ms = jnp.mean(out * out, axis=-1, keepdims=True)
    o_ref[0] = out * lax.rsqrt(ms + EPS) * fnw_ref[...]


def _outin_kernel(ya_ref, za_ref, yl_ref, zl_ref, x_ref, mod_ref, anw_ref, lnw_ref, wo_ref,
                  modn_ref, nwn_ref, wi_ref,
                  xo_ref, q_ref, k_ref, v_ref, zao_ref, xlo_ref, zlo_ref):
    out = _outproj_math(ya_ref, za_ref, yl_ref, zl_ref, x_ref, mod_ref, anw_ref, lnw_ref, wo_ref)
    xo_ref[0] = out
    _inproj_math(out, modn_ref, nwn_ref, wi_ref, (q_ref, k_ref, v_ref, zao_ref, xlo_ref, zlo_ref))


def _proj_specs(d, tm=ROW_TILE):
    return dict(
        half=pl.BlockSpec((1, tm, D_ATTN), lambda i, j: (i, j, 0)),
        full=pl.BlockSpec((1, tm, d), lambda i, j: (i, j, 0)),
        mod=pl.BlockSpec((1, 3, d), lambda i, j: (i, 0, 0)),
        nvec=pl.BlockSpec((1, D_ATTN), lambda i, j: (0, 0)),
        dvec=pl.BlockSpec((1, d), lambda i, j: (0, 0)),
        w_out=pl.BlockSpec((d, d), lambda i, j: (0, 0), pipeline_mode=pl.Buffered(1)),
        w_in=pl.BlockSpec((d, D_IN), lambda i, j: (0, 0), pipeline_mode=pl.Buffered(1)),
    )


def _outproj_call(ya, za, yl, zl, x, mod3, anw, lnw, w_out, fnw):
    b, s, d = x.shape
    sp = _proj_specs(d, OUT_ROW_TILE)
    return pl.pallas_call(
        _outproj_kernel,
        grid=(b, s // OUT_ROW_TILE),
        in_specs=[sp["half"]] * 4 + [sp["full"], sp["mod"], sp["nvec"], sp["nvec"], sp["w_out"],
                                     sp["dvec"]],
        out_specs=sp["full"],
        out_shape=jax.ShapeDtypeStruct((b, s, d), F32),
        compiler_params=pltpu.CompilerParams(
            dimension_semantics=("arbitrary", "arbitrary"), vmem_limit_bytes=VMEM_LIMIT_BYTES),
        name="outproj",
    )(ya, za, yl, zl, x, mod3, anw.reshape(1, D_ATTN), lnw.reshape(1, D_LRU), w_out,
      fnw.reshape(1, d))


def _outin_call(ya, za, yl, zl, x, mod3, anw, lnw, w_out, mod3_next, norm_w_next, w_in):
    b, s, d = x.shape
    sp = _proj_specs(d)
    proj_specs, proj_shapes = _inproj_outputs(b, s)
    return pl.pallas_call(
        _outin_kernel,
        grid=(b, s // ROW_TILE),
        in_specs=[sp["half"]] * 4 + [sp["full"], sp["mod"], sp["nvec"], sp["nvec"], sp["w_out"],
                                     sp["mod"], sp["dvec"], sp["w_in"]],
        out_specs=[sp["full"]] + proj_specs,
        out_shape=[jax.ShapeDtypeStruct((b, s, d), F32)] + proj_shapes,
        compiler_params=pltpu.CompilerParams(
            dimension_semantics=("arbitrary", "arbitrary"), vmem_limit_bytes=PROJ_VMEM_LIMIT_BYTES),
        name="outproj_inproj",
    )(ya, za, yl, zl, x, mod3, anw.reshape(1, D_ATTN), lnw.reshape(1, D_LRU), w_out,
      mod3_next, norm_w_next.reshape(1, d), w_in)


def _gate_weights(a_w, x_w):
    per_slab = LANES // LRU_BLOCK
    slabs = []
    for j in range(D_LRU // LANES):
        blocks = range(j * per_slab, (j + 1) * per_slab)
        diag = lambda w: jax.scipy.linalg.block_diag(*[w[g] for g in blocks])
        slabs.append(jnp.concatenate([diag(a_w), diag(x_w)], axis=1))
    return jnp.stack(slabs).astype(BF16)


def kernel(x, c, norm_w, ada_w, ada_b, w_in, conv_w, conv_b, rg_a_w, rg_a_b, rg_x_w, rg_x_b,
           rg_lambda, attn_out_norm_w, lru_out_norm_w, w_out, final_norm_w):
    n_layers = ada_w.shape[0]
    b = x.shape[0]
    mod = _mod_call(c, ada_w, ada_b)
    slopes = jnp.asarray(_alibi_slopes())
    kext = _key_side_table(x.shape[1])
    mod3 = [mod[l].reshape(b, 3, D_MODEL) for l in range(n_layers)]
    q, k, v, za, xl, zl = _inproj_call(x, mod3[0], norm_w[0], w_in, 0)
    for l in range(n_layers):
        last = l + 1 == n_layers
        ya, yl, *wbf = _mixer_call(slopes, kext, q, k, v, xl, conv_w[l], conv_b[l],
                                   _gate_weights(rg_a_w[l], rg_x_w[l]), rg_a_b[l], rg_x_b[l],
                                   rg_lambda[l], [(w_out, l)] if last else [(w_out, l), (w_in, l + 1)])
        if last:
            x = _outproj_call(ya, za, yl, zl, x, mod3[l], attn_out_norm_w[l], lru_out_norm_w[l],
                              wbf[0], final_norm_w)
        else:
            x, q, k, v, za, xl, zl = _outin_call(
                ya, za, yl, zl, x, mod3[l], attn_out_norm_w[l], lru_out_norm_w[l], wbf[0],
                mod3[l + 1], norm_w[l + 1], wbf[1])
    return x
```

```python
import math

import jax
import jax.numpy as jnp
import numpy as np
from jax import lax
from jax.experimental import pallas as pl
from jax.experimental.pallas import tpu as pltpu

D_MODEL = 1024
D_ATTN = 512
D_LRU = 512
HEAD_DIM = 64
N_HEADS = D_ATTN // HEAD_DIM
N_LRU_BLOCKS = 8
LRU_BLOCK = D_LRU // N_LRU_BLOCKS
CONV_WIDTH = 4
RG_C = 8.0
MOBA_BLOCK = 256
MOBA_TOPK = 3
D_IN = 4 * D_ATTN + 2 * D_LRU
EPS = 1e-6

F32 = jnp.float32
BF16 = jnp.bfloat16
NEG_BIG = -1e30
LOG2E = math.log2(math.e)

LANES = 128
SUBLANES = 8
VMEM_LIMIT_BYTES = 48 * 1024 * 1024
PROJ_VMEM_LIMIT_BYTES = 56 * 1024 * 1024

ROW_TILE = 512
IN_ROW_TILE = 1024
OUT_ROW_TILE = 1024
WEIGHT_CAST_ROWS = 64
LRU_CHUNK = 256
LRU_SEGMENT = 4
HEADS_PER_PAIR = LANES // HEAD_DIM
BIAS_TERMS = 3
FEATURE_MAJOR_OUTPUTS = (0, 2)
ATTN_PAIRS_PER_STEP = 2
SCORE_SLOTS = 3
PREPARE_AT = 6


def _sigmoid(v):
    return 0.5 * jnp.tanh(0.5 * v) + 0.5


def _silu(v):
    return v * _sigmoid(v)


def _split_bf16(v):
    parts = []
    for _ in range(BIAS_TERMS):
        part = v.astype(BF16).astype(F32)
        parts.append(part)
        v = v - part
    return parts


def _mod_kernel(c_ref, w_ref, b_ref, o_ref):
    s = _silu(c_ref[...]).astype(BF16)
    w = w_ref[0].astype(BF16)
    o_ref[0] = jnp.dot(s, w, preferred_element_type=F32) + b_ref[0]


def _mod_call(c, ada_w, ada_b, n_layers):
    _, d, d3 = ada_w.shape
    b = c.shape[0]
    tn = 1024
    return pl.pallas_call(
        _mod_kernel,
        grid=(n_layers, d3 // tn),
        in_specs=[
            pl.BlockSpec((b, d), lambda l, j: (0, 0)),
            pl.BlockSpec((1, d, tn), lambda l, j: (l, 0, j)),
            pl.BlockSpec((1, 1, tn), lambda l, j: (l, 0, j)),
        ],
        out_specs=pl.BlockSpec((1, b, tn), lambda l, j: (l, 0, j)),
        out_shape=jax.ShapeDtypeStruct((n_layers, b, d3), F32),
        compiler_params=pltpu.CompilerParams(
            dimension_semantics=("arbitrary", "arbitrary"), vmem_limit_bytes=VMEM_LIMIT_BYTES),
        name="adaln_mod",
    )(c, ada_w, ada_b.reshape(-1, 1, d3))


def _cast_weight_once(w_ref, wbf_scr):
    rows = w_ref.shape[1]

    @pl.when((pl.program_id(0) == 0) & (pl.program_id(1) == 0))
    def _():
        def body(i, carry):
            r0 = pl.multiple_of(i * WEIGHT_CAST_ROWS, WEIGHT_CAST_ROWS)
            wbf_scr[pl.ds(r0, WEIGHT_CAST_ROWS), :] = w_ref[0, pl.ds(r0, WEIGHT_CAST_ROWS), :].astype(BF16)
            return carry
        lax.fori_loop(0, rows // WEIGHT_CAST_ROWS, body, 0)


def _inproj_math(x, mod_ref, nw_ref, w_ref, out_refs):
    ms = jnp.mean(x * x, axis=-1, keepdims=True)
    y = x * lax.rsqrt(ms + EPS) * nw_ref[...]
    shift = mod_ref[0, 0:1, :]
    scale = mod_ref[0, 1:2, :]
    h = (y * (1.0 + scale) + shift).astype(BF16)
    for j, ref in enumerate(out_refs):
        r = jnp.dot(h, w_ref[:, j * D_ATTN:(j + 1) * D_ATTN], preferred_element_type=F32)
        if j == 0:
            r = r * (HEAD_DIM ** -0.5 * LOG2E)
        if j in FEATURE_MAJOR_OUTPUTS:
            r = r.T
        ref[0] = r.astype(ref.dtype)


def _inproj_kernel(x_ref, mod_ref, nw_ref, wf_ref, *rest):
    n_mods = (len(rest) - 7) // 3
    w_ref = rest[-1]
    if n_mods:
        c_ref, rest = rest[0], rest[1:]
        aw_refs, ab_refs = rest[:n_mods], rest[n_mods:2 * n_mods]
        out_refs, mo_refs = rest[2 * n_mods:2 * n_mods + 6], rest[2 * n_mods + 6:3 * n_mods + 6]
        first = (pl.program_id(0) == 0) & (pl.program_id(1) == 0)
        sc = _silu(c_ref[0]).astype(BF16)
        for aw_ref, ab_ref, mo_ref in zip(aw_refs, ab_refs, mo_refs):
            @pl.when(first)
            def _(ab_ref=ab_ref, mo_ref=mo_ref):
                mo_ref[...] = jnp.broadcast_to(ab_ref[0], mo_ref.shape)
            mo_ref[...] += jnp.dot(sc, aw_ref[0].astype(BF16), preferred_element_type=F32)
    else:
        out_refs = rest[:6]
    _cast_weight_once(wf_ref, w_ref)
    _inproj_math(x_ref[0], mod_ref, nw_ref, w_ref, out_refs)


def _inproj_outputs(b, s, tm=ROW_TILE):
    token_major = pl.BlockSpec((1, tm, D_ATTN), lambda i, j: (i, j, 0))
    feature_major = pl.BlockSpec((1, D_ATTN, tm), lambda i, j: (i, 0, j))
    dtypes = (BF16, BF16, BF16, BF16, F32, BF16)
    specs, shapes = [], []
    for j, dt in enumerate(dtypes):
        fm = j in FEATURE_MAJOR_OUTPUTS
        specs.append(feature_major if fm else token_major)
        shapes.append(jax.ShapeDtypeStruct((b, D_ATTN, s) if fm else (b, s, D_ATTN), dt))
    return specs, shapes


def _inproj_call(x, mod3, norm_w, w_in, layer, c, ada_w, ada_b, mod_layers):
    b, s, d = x.shape
    tm = IN_ROW_TILE
    out_specs, out_shape = _inproj_outputs(b, s, tm)
    nj = s // tm
    steps = b * nj
    rows, d3 = d // steps, ada_w.shape[2]
    extra_specs, extra_args = [], []
    if mod_layers:
        extra_args.append(c.reshape(b, steps, rows).transpose(1, 0, 2))
        extra_specs.append(pl.BlockSpec((1, b, rows), lambda i, j: (i * nj + j, 0, 0)))
        for l in mod_layers:
            extra_args.append(ada_w)
            extra_specs.append(pl.BlockSpec((1, rows, d3), lambda i, j, l=l: (l, i * nj + j, 0)))
        for l in mod_layers:
            extra_args.append(ada_b.reshape(-1, 1, d3))
            extra_specs.append(pl.BlockSpec((1, 1, d3), lambda i, j, l=l: (l, 0, 0)))
    return pl.pallas_call(
        _inproj_kernel,
        grid=(b, nj),
        in_specs=[
            pl.BlockSpec((1, tm, d), lambda i, j: (i, j, 0)),
            pl.BlockSpec((1, 3, d), lambda i, j: (i, 0, 0)),
            pl.BlockSpec((1, d), lambda i, j: (0, 0)),
            pl.BlockSpec((1, d, D_IN), lambda i, j: (layer, 0, 0), pipeline_mode=pl.Buffered(1)),
        ] + extra_specs,
        out_specs=out_specs + [pl.BlockSpec((b, d3), lambda i, j: (0, 0)) for _ in mod_layers],
        out_shape=out_shape + [jax.ShapeDtypeStruct((b, d3), F32) for _ in mod_layers],
        scratch_shapes=[pltpu.VMEM((d, D_IN), BF16)],
        compiler_params=pltpu.CompilerParams(
            dimension_semantics=("arbitrary", "arbitrary"), vmem_limit_bytes=VMEM_LIMIT_BYTES),
        name="inproj",
    )(x, mod3, norm_w.reshape(1, d), w_in, *extra_args)


def _mixer_kernel(slopes_ref, q_ref, k_ref, v_ref, kext_ref, *rest):
    n_slabs = k_ref.shape[2] // LANES
    xl_refs = rest[:n_slabs]
    (cw_ref, cb_ref, wg_ref, ab_ref, xb_ref, lam_ref, o_ref, ol_ref,
     kaug_scr, qaug_scr, vaug_scr, s_scr, p_scr, out_scr,
     head_scr, h_scr, xc_scr, g_scr) = rest[n_slabs:]
    s_len = k_ref.shape[1]
    n_blk = s_len // MOBA_BLOCK
    n_pairs = k_ref.shape[2] // LANES
    n_heads = n_pairs * HEADS_PER_PAIR
    first_head = pl.program_id(0) * n_heads

    prow = lax.broadcasted_iota(jnp.int32, (n_blk, s_len), 0)
    pcol = lax.broadcasted_iota(jnp.int32, (n_blk, s_len), 1)
    block_mean = jnp.where(pcol // MOBA_BLOCK == prow, 1.0 / MOBA_BLOCK, 0.0).astype(BF16)
    klane = lax.broadcasted_iota(jnp.int32, (s_len, LANES), 1)
    mlane = lax.broadcasted_iota(jnp.int32, (n_blk, LANES), 1)
    n_iota = prow
    q_blk = pcol // MOBA_BLOCK
    key_off = lax.broadcasted_iota(jnp.int32, (MOBA_BLOCK, MOBA_BLOCK), 0)
    qry_off = lax.broadcasted_iota(jnp.int32, (MOBA_BLOCK, MOBA_BLOCK), 1)
    causal = key_off <= qry_off

    def prepare(pp):
        lanes = slice(pp * LANES, (pp + 1) * LANES)
        kp = k_ref[0, :, lanes]
        q_t = q_ref[0, lanes, :]
        v_t = v_ref[0, lanes, :]
        kmean = jnp.dot(block_mean, kp, preferred_element_type=F32)
        terms = []
        for hd in range(HEADS_PER_PAIR):
            rest = jnp.where(mlane // HEAD_DIM == hd, kmean, 0.0)
            for _ in range(3):
                part = rest.astype(BF16).astype(F32)
                terms.append(part)
                rest = rest - part
        terms = jnp.concatenate(terms, axis=0).astype(BF16)
        half = s_len // 2
        gates = jnp.concatenate(
            [jnp.dot(terms, q_t[:, :half], preferred_element_type=F32),
             jnp.dot(terms, q_t[:, half:], preferred_element_type=F32)], axis=1)
        for hd in range(HEADS_PER_PAIR):
            h = pp * HEADS_PER_PAIR + hd
            kaug_scr[h] = jnp.where(klane // HEAD_DIM == hd, kp, kext_ref[h])
            slope2 = slopes_ref[first_head + h] * LOG2E
            g_hi, g_mid, g_lo = (gates[(3 * hd + t) * n_blk:(3 * hd + t + 1) * n_blk]
                                 for t in range(3))
            g_t = g_hi + g_mid + g_lo
            rank = jnp.zeros((n_blk, s_len), jnp.int32)
            for m in range(n_blk):
                gm = g_t[m:m + 1, :]
                beats = (gm > g_t) | ((gm == g_t) & (m < n_iota))
                rank = rank + jnp.where(beats & (m < q_blk), 1, 0)
            sel = ((n_iota < q_blk) & (rank < MOBA_TOPK)) | (n_iota == q_blk)
            bias = _split_bf16(slope2 * ((n_iota - q_blk) * MOBA_BLOCK).astype(F32))
            selb = [jnp.where(sel, part, NEG_BIG if t == 0 else 0.0)
                    for t, part in enumerate(bias)]
            flag = jnp.where(n_iota < BIAS_TERMS, 1.0, 0.0)
            pad = jnp.zeros((HEAD_DIM - (BIAS_TERMS + 1) * n_blk, s_len), F32)
            ext = jnp.concatenate(selb + [flag, pad], axis=0)
            own = slice(hd * HEAD_DIM, (hd + 1) * HEAD_DIM)
            other = slice((1 - hd) * HEAD_DIM, (2 - hd) * HEAD_DIM)
            qaug_scr[h, own, :] = q_t[own]
            qaug_scr[h, other, :] = ext.astype(BF16)
            vaug_scr[h, 0:HEAD_DIM, :] = v_t[own]
            vaug_scr[h, HEAD_DIM:, :] = jnp.ones((2 * SUBLANES, s_len), BF16)

    items = [(h, qb) for h in range(n_heads)
             for qb in (range(n_blk) if h % 2 == 0 else reversed(range(n_blk)))]
    per_pair = HEADS_PER_PAIR * n_blk

    def scores(i):
        h, qb = items[i]
        slot = i % SCORE_SLOTS
        q0, kk = qb * MOBA_BLOCK, (qb + 1) * MOBA_BLOCK
        qa = qaug_scr[h, :, q0:kk]
        m = None
        for r0 in range(0, kk, MOBA_BLOCK):
            s = jnp.dot(kaug_scr[h, r0:r0 + MOBA_BLOCK, :], qa, preferred_element_type=F32)
            if r0 == q0:
                s = jnp.where(causal, s, NEG_BIG)
            s_scr[slot, r0:r0 + MOBA_BLOCK, :] = s
            mh = jnp.max(s.reshape(MOBA_BLOCK // SUBLANES, SUBLANES, MOBA_BLOCK), axis=0)
            m = mh if m is None else jnp.maximum(m, mh)
        return jnp.max(m, axis=0, keepdims=True)

    def probs(i, m):
        _, qb = items[i]
        kk = (qb + 1) * MOBA_BLOCK
        p_scr[i % 2, 0:kk, :] = jnp.exp2(s_scr[i % SCORE_SLOTS, 0:kk, :] - m).astype(BF16)

    def weighted(i):
        h, qb = items[i]
        slot = i % 2
        q0, kk = qb * MOBA_BLOCK, (qb + 1) * MOBA_BLOCK
        o = None
        for r0, r1 in ((0, kk // 2), (kk // 2, kk)):
            part = jnp.dot(vaug_scr[h, :, r0:r1], p_scr[slot, r0:r1, :], preferred_element_type=F32)
            o = part if o is None else o + part
        out_scr[h * HEAD_DIM:(h + 1) * HEAD_DIM, q0:kk] = o[0:HEAD_DIM] / o[HEAD_DIM:HEAD_DIM + 1]

    lru_units = []
    for sl in range(n_slabs):
        lanes = slice(sl * LANES, (sl + 1) * LANES)

        def store_h(value, lanes=lanes):
            ol_ref[0, :, lanes] = value

        lru_units += _lru_units(
            xl_refs[sl].at[0], cw_ref[:, lanes], cb_ref[:, lanes], wg_ref.at[sl], ab_ref[:, lanes],
            xb_ref[:, lanes], lam_ref[:, lanes], head_scr.at[sl], h_scr.at[sl], xc_scr.at[sl],
            g_scr.at[sl], store_h)
    lru_done = 0

    n_items = len(items)
    prepare(0)
    col_max = {i: scores(i) for i in range(SCORE_SLOTS)}
    probs(0, col_max.pop(0))
    for i in range(n_items):
        pp, within = divmod(i, per_pair)
        if within == PREPARE_AT and pp + 1 < n_pairs:
            prepare(pp + 1)
        if i + SCORE_SLOTS < n_items:
            col_max[i + SCORE_SLOTS] = scores(i + SCORE_SLOTS)
        if i + 1 < n_items:
            probs(i + 1, col_max.pop(i + 1))
        weighted(i)
        if within == per_pair - 1:
            lanes = slice(pp * LANES, (pp + 1) * LANES)
            o_ref[0, :, lanes] = out_scr[lanes, :].T.astype(o_ref.dtype)
        lru_target = (i + 1) * len(lru_units) // n_items
        while lru_done < lru_target:
            lru_units[lru_done]()
            lru_done += 1


def _alibi_slopes():
    return np.exp2(-8.0 * (np.arange(N_HEADS, dtype=np.float32) + 1.0) / N_HEADS).astype(np.float32)


def _key_side_table(s_len):
    n_blk = s_len // MOBA_BLOCK
    head = np.arange(N_HEADS)[:, None, None]
    pos = np.arange(s_len)[None, :, None]
    lane = np.arange(LANES)[None, None, :]
    sub = lane % HEAD_DIM
    foreign = lane // HEAD_DIM != head % HEADS_PER_PAIR
    onehot = (sub % n_blk == pos // MOBA_BLOCK).astype(np.float32)
    off = ((_alibi_slopes() * np.float32(LOG2E))[:, None, None]
           * (pos % MOBA_BLOCK).astype(np.float32)).astype(np.float32)
    ext = np.where(sub < BIAS_TERMS * n_blk, onehot, 0.0).astype(np.float32)
    rest = off
    for t in range(BIAS_TERMS):
        part = (rest.view(np.uint32) & np.uint32(0xFFFF0000)).view(np.float32) if t + 1 < BIAS_TERMS else rest
        ext = np.where(sub == BIAS_TERMS * n_blk + t, part, ext)
        rest = rest - part
    table = np.where(foreign, ext, 0.0).astype(np.float32)
    return jnp.asarray(table.astype(BF16))


def _mixer_call(slopes, kext, q_t, k, v_t, xl, conv_w, conv_b, w_gate, a_b, x_b, lam):
    b, s, _ = k.shape
    pairs = ATTN_PAIRS_PER_STEP
    width = pairs * LANES
    heads = pairs * HEADS_PER_PAIR
    row = lambda a: a.reshape(1, D_LRU)
    spec = pl.BlockSpec((1, s, width), lambda j, i, sl: (i, 0, j))
    spec_t = pl.BlockSpec((1, width, s), lambda j, i, sl: (i, j, 0))
    vec = pl.BlockSpec((1, width), lambda j, i, sl: (0, j))
    slab_specs = [pl.BlockSpec((1, s, LANES), lambda j, i, sl, n=n: (i, 0, j * pairs + n))
                  for n in range(pairs)]
    sub_len = LRU_SEGMENT * SUBLANES
    return pl.pallas_call(
        _mixer_kernel,
        grid_spec=pltpu.PrefetchScalarGridSpec(
            num_scalar_prefetch=1,
            grid=(D_ATTN // width, b),
            in_specs=[spec_t, spec, spec_t,
                      pl.BlockSpec((heads, s, LANES), lambda j, i, sl: (j, 0, 0),
                                   pipeline_mode=pl.Buffered(1))]
            + slab_specs
            + [pl.BlockSpec((CONV_WIDTH, width), lambda j, i, sl: (0, j)), vec,
               pl.BlockSpec((pairs, LANES, 2 * LANES), lambda j, i, sl: (j, 0, 0)), vec, vec, vec],
            out_specs=[spec, spec],
            scratch_shapes=[
                pltpu.VMEM((heads, s, LANES), BF16),
                pltpu.VMEM((heads, LANES, s), BF16),
                pltpu.VMEM((heads, HEAD_DIM + 2 * SUBLANES, s), BF16),
                pltpu.VMEM((SCORE_SLOTS, s, MOBA_BLOCK), F32),
                pltpu.VMEM((2, s, MOBA_BLOCK), BF16),
                pltpu.VMEM((width, s), F32),
                pltpu.VMEM((pairs, SUBLANES + sub_len, LANES), F32),
                pltpu.VMEM((pairs, s, LANES), F32),
                pltpu.VMEM((pairs, 2, LRU_CHUNK, LANES), F32),
                pltpu.VMEM((pairs, 2, LRU_CHUNK, 2 * LANES), F32),
            ],
        ),
        out_shape=[jax.ShapeDtypeStruct((b, s, D_ATTN), BF16), jax.ShapeDtypeStruct((b, s, D_LRU), BF16)],
        compiler_params=pltpu.CompilerParams(
            dimension_semantics=("arbitrary", "arbitrary"), vmem_limit_bytes=VMEM_LIMIT_BYTES),
        name="mixer",
    )(slopes, q_t, k, v_t, kext, *([xl] * pairs), conv_w, row(conv_b), w_gate, row(a_b), row(x_b),
      row(lam))


def _lru_units(x_seq, cw, cb, w_gate, a_b, x_b, lam, head_scr, h_scr, xc_scr, g_scr, store_h):
    s_len = x_seq.shape[0]
    seg = LRU_SEGMENT
    sub_len = seg * SUBLANES
    n_sub = LRU_CHUNK // sub_len
    n_chunks = s_len // LRU_CHUNK

    neg_lam = -lam
    softplus = jnp.maximum(neg_lam, 0.0) + jnp.log1p(jnp.exp(-jnp.abs(neg_lam)))
    half_rate = (0.5 * RG_C) * softplus
    rate_log2 = -LOG2E * half_rate
    ab_half = 0.5 * a_b
    xb_half = 0.5 * x_b
    cw_half = 0.5 * cw
    cb_half = 0.5 * cb
    sub = lax.broadcasted_iota(jnp.int32, (SUBLANES, LANES), 0)
    steps = (1, 2, 4)
    in_vreg = [sub >= step for step in steps]
    state = {"h": jnp.zeros((SUBLANES, LANES), F32)}

    def strided(ref, start):
        return ref[pl.ds(start, SUBLANES, stride=seg), :]

    def init():
        head_scr[0:SUBLANES, :] = jnp.zeros((SUBLANES, LANES), F32)
        head_scr[SUBLANES:, :] = x_seq[0:sub_len, :]

    def conv_and_gates(ci):
        xc_parts = []
        for c in range(n_sub):
            if ci == 0 and c == 0:
                src, t0 = head_scr, SUBLANES
            else:
                src, t0 = x_seq, ci * LRU_CHUNK
            cache = {}
            for g in range(seg):
                acc = cb_half
                for j in range(CONV_WIDTH):
                    off = c * sub_len + g - (CONV_WIDTH - 1) + j
                    if off not in cache:
                        cache[off] = strided(src, t0 + off)
                    acc = acc + cw_half[j:j + 1, :] * cache[off]
                xc_parts.append(acc)
        hx = jnp.concatenate(xc_parts, axis=0)
        xc_scr[ci % 2] = hx
        g_scr[ci % 2] = jnp.dot(hx.astype(BF16), w_gate[...], preferred_element_type=F32)

    def recurrence(ci, c):
        t0 = ci * LRU_CHUNK
        rows = slice(c * sub_len, (c + 1) * sub_len)
        h_in = state["h"]
        hx = xc_scr[ci % 2, rows, :]
        t_r = jnp.tanh(g_scr[ci % 2, rows, :LANES] + ab_half)
        t_i = jnp.tanh(g_scr[ci % 2, rows, LANES:] + xb_half)
        two_r = t_r + 1.0
        neg_log_a = half_rate * two_r
        a = jnp.exp2(rate_log2 * two_r)
        y = jnp.tanh(neg_log_a) * (a * a + 1.0)
        root = jnp.where(y == 0.0, 0.0, y * lax.rsqrt(y))
        bterm = root * ((t_i + 1.0) * hx)

        comp = []
        for g in range(seg):
            av, bv = a[g * SUBLANES:(g + 1) * SUBLANES], bterm[g * SUBLANES:(g + 1) * SUBLANES]
            if g > 0:
                bv = av * comp[-1][1] + bv
                av = av * comp[-1][0]
            comp.append((av, bv))
        pa, pb = comp[-1]
        for step, valid in zip(steps, in_vreg):
            a_sh = jnp.where(valid, pltpu.roll(pa, step, axis=0), 1.0)
            b_sh = jnp.where(valid, pltpu.roll(pb, step, axis=0), 0.0)
            pb = pa * b_sh + pb
            pa = pa * a_sh
        ea = jnp.where(in_vreg[0], pltpu.roll(pa, 1, axis=0), 1.0)
        eb = jnp.where(in_vreg[0], pltpu.roll(pb, 1, axis=0), 0.0)
        seg_in = ea * h_in + eb
        for g in range(seg):
            h_g = comp[g][0] * seg_in + comp[g][1]
            h_scr[pl.ds(t0 + c * sub_len + g, SUBLANES, stride=seg), :] = h_g
        h_last = pa * h_in + pb
        state["h"] = jnp.broadcast_to(h_last[SUBLANES - 1:SUBLANES, :], (SUBLANES, LANES))

    def finish():
        store_h(h_scr[...].astype(BF16))

    units = [init, lambda: conv_and_gates(0)]
    for ci in range(n_chunks):
        for c in range(n_sub):
            units.append(lambda ci=ci, c=c: recurrence(ci, c))
            if c == n_sub // 2 - 1 and ci + 1 < n_chunks:
                units.append(lambda ci=ci: conv_and_gates(ci + 1))
    units.append(finish)
    return units


def _gated_norm(y_ref, z_ref, nw_ref):
    y = y_ref[0].astype(F32)
    ms = jnp.mean(y * y, axis=-1, keepdims=True)
    return ((y * lax.rsqrt(ms + EPS) * nw_ref[...]) * _silu(z_ref[0].astype(F32))).astype(BF16)


def _outproj_math(ya_ref, za_ref, yl_ref, zl_ref, x_ref, mod_ref, anw_ref, lnw_ref, w_ref):
    y = jnp.dot(_gated_norm(ya_ref, za_ref, anw_ref), w_ref[0:D_ATTN, :], preferred_element_type=F32)
    y = y + jnp.dot(_gated_norm(yl_ref, zl_ref, lnw_ref), w_ref[D_ATTN:, :],
                    preferred_element_type=F32)
    return x_ref[0] + mod_ref[0, 2:3, :] * y


def _outproj_kernel(ya_ref, za_ref, yl_ref, zl_ref, x_ref, mod_ref, anw_ref, lnw_ref, wf_ref, fnw_ref,
                    o_ref, w_ref):
    _cast_weight_once(wf_ref, w_ref)
    out = _outproj_math(ya_ref, za_ref, yl_ref, zl_ref, x_ref, mod_ref, anw_ref, lnw_ref, w_ref)
    ms = jnp.mean(out * out, axis=-1, keepdims=True)
    o_ref[0] = out * lax.rsqrt(ms + EPS) * fnw_ref[...]


def _outin_kernel(ya_ref, za_ref, yl_ref, zl_ref, x_ref, mod_ref, anw_ref, lnw_ref, wof_ref,
                  modn_ref, nwn_ref, wif_ref,
                  xo_ref, q_ref, k_ref, v_ref, zao_ref, xlo_ref, zlo_ref, wo_ref, wi_ref):
    _cast_weight_once(wof_ref, wo_ref)
    _cast_weight_once(wif_ref, wi_ref)
    out = _outproj_math(ya_ref, za_ref, yl_ref, zl_ref, x_ref, mod_ref, anw_ref, lnw_ref, wo_ref)
    xo_ref[0] = out
    _inproj_math(out, modn_ref, nwn_ref, wi_ref, (q_ref, k_ref, v_ref, zao_ref, xlo_ref, zlo_ref))


def _proj_specs(d, layer, tm=ROW_TILE):
    return dict(
        half=pl.BlockSpec((1, tm, D_ATTN), lambda i, j: (i, j, 0)),
        full=pl.BlockSpec((1, tm, d), lambda i, j: (i, j, 0)),
        mod=pl.BlockSpec((1, 3, d), lambda i, j: (i, 0, 0)),
        nvec=pl.BlockSpec((1, D_ATTN), lambda i, j: (0, 0)),
        dvec=pl.BlockSpec((1, d), lambda i, j: (0, 0)),
        w_out=pl.BlockSpec((1, d, d), lambda i, j: (layer, 0, 0), pipeline_mode=pl.Buffered(1)),
        w_in=pl.BlockSpec((1, d, D_IN), lambda i, j: (layer + 1, 0, 0), pipeline_mode=pl.Buffered(1)),
    )


def _outproj_call(ya, za, yl, zl, x, mod3, anw, lnw, w_out, layer, fnw):
    b, s, d = x.shape
    sp = _proj_specs(d, layer, OUT_ROW_TILE)
    return pl.pallas_call(
        _outproj_kernel,
        grid=(b, s // OUT_ROW_TILE),
        in_specs=[sp["half"]] * 4 + [sp["full"], sp["mod"], sp["nvec"], sp["nvec"], sp["w_out"],
                                     sp["dvec"]],
        out_specs=sp["full"],
        out_shape=jax.ShapeDtypeStruct((b, s, d), F32),
        scratch_shapes=[pltpu.VMEM((d, d), BF16)],
        compiler_params=pltpu.CompilerParams(
            dimension_semantics=("arbitrary", "arbitrary"), vmem_limit_bytes=VMEM_LIMIT_BYTES),
        name="outproj",
    )(ya, za, yl, zl, x, mod3, anw.reshape(1, D_ATTN), lnw.reshape(1, D_LRU), w_out,
      fnw.reshape(1, d))


def _outin_call(ya, za, yl, zl, x, mod3, anw, lnw, w_out, layer, mod3_next, norm_w_next, w_in):
    b, s, d = x.shape
    sp = _proj_specs(d, layer)
    proj_specs, proj_shapes = _inproj_outputs(b, s)
    return pl.pallas_call(
        _outin_kernel,
        grid=(b, s // ROW_TILE),
        in_specs=[sp["half"]] * 4 + [sp["full"], sp["mod"], sp["nvec"], sp["nvec"], sp["w_out"],
                                     sp["mod"], sp["dvec"], sp["w_in"]],
        out_specs=[sp["full"]] + proj_specs,
        out_shape=[jax.ShapeDtypeStruct((b, s, d), F32)] + proj_shapes,
        scratch_shapes=[pltpu.VMEM((d, d), BF16), pltpu.VMEM((d, D_IN), BF16)],
        compiler_params=pltpu.CompilerParams(
            dimension_semantics=("arbitrary", "arbitrary"), vmem_limit_bytes=PROJ_VMEM_LIMIT_BYTES),
        name="outproj_inproj",
    )(ya, za, yl, zl, x, mod3, anw.reshape(1, D_ATTN), lnw.reshape(1, D_LRU), w_out,
      mod3_next, norm_w_next.reshape(1, d), w_in)


def _gate_weights(a_w, x_w):
    per_slab = LANES // LRU_BLOCK
    slabs = []
    for j in range(D_LRU // LANES):
        blocks = range(j * per_slab, (j + 1) * per_slab)
        diag = lambda w: jax.scipy.linalg.block_diag(*[w[g] for g in blocks])
        slabs.append(jnp.concatenate([diag(a_w), diag(x_w)], axis=1))
    return jnp.stack(slabs).astype(BF16)


def kernel(x, c, norm_w, ada_w, ada_b, w_in, conv_w, conv_b, rg_a_w, rg_a_b, rg_x_w, rg_x_b,
           rg_lambda, attn_out_norm_w, lru_out_norm_w, w_out, final_norm_w):
    n_layers = ada_w.shape[0]
    b = x.shape[0]
    mod = _mod_call(c, ada_w, ada_b, 1)
    slopes = jnp.asarray(_alibi_slopes())
    kext = _key_side_table(x.shape[1])
    mod3 = [mod[0].reshape(b, 3, D_MODEL)]
    q, k, v, za, xl, zl, *mods = _inproj_call(x, mod3[0], norm_w[0], w_in, 0, c, ada_w, ada_b,
                                              list(range(1, n_layers)))
    mod3 += [m.reshape(b, 3, D_MODEL) for m in mods]
    for l in range(n_layers):
        ya, yl = _mixer_call(slopes, kext, q, k, v, xl, conv_w[l], conv_b[l],
                             _gate_weights(rg_a_w[l], rg_x_w[l]), rg_a_b[l], rg_x_b[l], rg_lambda[l])
        if l + 1 < n_layers:
            x, q, k, v, za, xl, zl = _outin_call(
                ya, za, yl, zl, x, mod3[l], attn_out_norm_w[l], lru_out_norm_w[l], w_out, l,
                mod3[l + 1], norm_w[l + 1], w_in)
        else:
            x = _outproj_call(ya, za, yl, zl, x, mod3[l], attn_out_norm_w[l], lru_out_norm_w[l],
                              w_out, l, final_norm_w)
    return x
```

```python
import math

import jax
import jax.numpy as jnp
import numpy as np
from jax import lax
from jax.experimental import pallas as pl
from jax.experimental.pallas import tpu as pltpu

D_MODEL = 1024
D_ATTN = 512
D_LRU = 512
HEAD_DIM = 64
N_HEADS = D_ATTN // HEAD_DIM
N_LRU_BLOCKS = 8
LRU_BLOCK = D_LRU // N_LRU_BLOCKS
CONV_WIDTH = 4
RG_C = 8.0
MOBA_BLOCK = 256
MOBA_TOPK = 3
D_IN = 4 * D_ATTN + 2 * D_LRU
EPS = 1e-6

F32 = jnp.float32
BF16 = jnp.bfloat16
NEG_BIG = -1e30
LOG2E = math.log2(math.e)

LANES = 128
SUBLANES = 8
VMEM_LIMIT_BYTES = 48 * 1024 * 1024
PROJ_VMEM_LIMIT_BYTES = 56 * 1024 * 1024

ROW_TILE = 512
IN_ROW_TILE = 1024
OUT_ROW_TILE = 1024
WEIGHT_CAST_ROWS = 64
LRU_CHUNK = 256
LRU_SEGMENT = 4
HEADS_PER_PAIR = LANES // HEAD_DIM
BIAS_TERMS = 3
FEATURE_MAJOR_OUTPUTS = (0, 2)
ATTN_PAIRS_PER_STEP = 2
SCORE_SLOTS = 3
PREPARE_AT = 6


def _sigmoid(v):
    return 0.5 * jnp.tanh(0.5 * v) + 0.5


def _silu(v):
    return v * _sigmoid(v)


def _split_bf16(v):
    parts = []
    for _ in range(BIAS_TERMS):
        part = v.astype(BF16).astype(F32)
        parts.append(part)
        v = v - part
    return parts


def _mod_kernel(c_ref, w_ref, b_ref, o_ref):
    s = _silu(c_ref[...]).astype(BF16)
    w = w_ref[0].astype(BF16)
    o_ref[0] = jnp.dot(s, w, preferred_element_type=F32) + b_ref[0]


def _mod_call(c, ada_w, ada_b):
    n_layers, d, d3 = ada_w.shape
    b = c.shape[0]
    tn = 1024
    return pl.pallas_call(
        _mod_kernel,
        grid=(n_layers, d3 // tn),
        in_specs=[
            pl.BlockSpec((b, d), lambda l, j: (0, 0)),
            pl.BlockSpec((1, d, tn), lambda l, j: (l, 0, j)),
            pl.BlockSpec((1, 1, tn), lambda l, j: (l, 0, j)),
        ],
        out_specs=pl.BlockSpec((1, b, tn), lambda l, j: (l, 0, j)),
        out_shape=jax.ShapeDtypeStruct((n_layers, b, d3), F32),
        compiler_params=pltpu.CompilerParams(
            dimension_semantics=("arbitrary", "arbitrary"), vmem_limit_bytes=VMEM_LIMIT_BYTES),
        name="adaln_mod",
    )(c, ada_w, ada_b.reshape(n_layers, 1, d3))


def _cast_weight_once(w_ref, wbf_scr):
    rows = w_ref.shape[1]

    @pl.when((pl.program_id(0) == 0) & (pl.program_id(1) == 0))
    def _():
        def body(i, carry):
            r0 = pl.multiple_of(i * WEIGHT_CAST_ROWS, WEIGHT_CAST_ROWS)
            wbf_scr[pl.ds(r0, WEIGHT_CAST_ROWS), :] = w_ref[0, pl.ds(r0, WEIGHT_CAST_ROWS), :].astype(BF16)
            return carry
        lax.fori_loop(0, rows // WEIGHT_CAST_ROWS, body, 0)


def _inproj_math(x, mod_ref, nw_ref, w_ref, out_refs):
    ms = jnp.mean(x * x, axis=-1, keepdims=True)
    y = x * lax.rsqrt(ms + EPS) * nw_ref[...]
    shift = mod_ref[0, 0:1, :]
    scale = mod_ref[0, 1:2, :]
    h = (y * (1.0 + scale) + shift).astype(BF16)
    for j, ref in enumerate(out_refs):
        r = jnp.dot(h, w_ref[:, j * D_ATTN:(j + 1) * D_ATTN], preferred_element_type=F32)
        if j == 0:
            r = r * (HEAD_DIM ** -0.5 * LOG2E)
        if j in FEATURE_MAJOR_OUTPUTS:
            r = r.T
        ref[0] = r.astype(ref.dtype)


def _inproj_kernel(x_ref, mod_ref, nw_ref, wf_ref, q_ref, k_ref, v_ref, za_ref, xl_ref, zl_ref,
                   w_ref):
    _cast_weight_once(wf_ref, w_ref)
    _inproj_math(x_ref[0], mod_ref, nw_ref, w_ref, (q_ref, k_ref, v_ref, za_ref, xl_ref, zl_ref))


def _inproj_outputs(b, s, tm=ROW_TILE):
    token_major = pl.BlockSpec((1, tm, D_ATTN), lambda i, j: (i, j, 0))
    feature_major = pl.BlockSpec((1, D_ATTN, tm), lambda i, j: (i, 0, j))
    dtypes = (BF16, BF16, BF16, BF16, F32, BF16)
    specs, shapes = [], []
    for j, dt in enumerate(dtypes):
        fm = j in FEATURE_MAJOR_OUTPUTS
        specs.append(feature_major if fm else token_major)
        shapes.append(jax.ShapeDtypeStruct((b, D_ATTN, s) if fm else (b, s, D_ATTN), dt))
    return specs, shapes


def _inproj_call(x, mod3, norm_w, w_in, layer):
    b, s, d = x.shape
    tm = IN_ROW_TILE
    out_specs, out_shape = _inproj_outputs(b, s, tm)
    return pl.pallas_call(
        _inproj_kernel,
        grid=(b, s // tm),
        in_specs=[
            pl.BlockSpec((1, tm, d), lambda i, j: (i, j, 0)),
            pl.BlockSpec((1, 3, d), lambda i, j: (i, 0, 0)),
            pl.BlockSpec((1, d), lambda i, j: (0, 0)),
            pl.BlockSpec((1, d, D_IN), lambda i, j: (layer, 0, 0), pipeline_mode=pl.Buffered(1)),
        ],
        out_specs=out_specs,
        out_shape=out_shape,
        scratch_shapes=[pltpu.VMEM((d, D_IN), BF16)],
        compiler_params=pltpu.CompilerParams(
            dimension_semantics=("arbitrary", "arbitrary"), vmem_limit_bytes=VMEM_LIMIT_BYTES),
        name="inproj",
    )(x, mod3, norm_w.reshape(1, d), w_in)


def _mixer_kernel(slopes_ref, q_ref, k_ref, v_ref, kext_ref, *rest):
    n_slabs = k_ref.shape[2] // LANES
    xl_refs = rest[:n_slabs]
    (cw_ref, cb_ref, wg_ref, ab_ref, xb_ref, lam_ref, o_ref, ol_ref,
     kaug_scr, qaug_scr, vaug_scr, s_scr, p_scr, out_scr,
     head_scr, h_scr, xc_scr, g_scr) = rest[n_slabs:]
    s_len = k_ref.shape[1]
    n_blk = s_len // MOBA_BLOCK
    n_pairs = k_ref.shape[2] // LANES
    n_heads = n_pairs * HEADS_PER_PAIR
    first_head = pl.program_id(0) * n_heads

    prow = lax.broadcasted_iota(jnp.int32, (n_blk, s_len), 0)
    pcol = lax.broadcasted_iota(jnp.int32, (n_blk, s_len), 1)
    block_mean = jnp.where(pcol // MOBA_BLOCK == prow, 1.0 / MOBA_BLOCK, 0.0).astype(BF16)
    klane = lax.broadcasted_iota(jnp.int32, (s_len, LANES), 1)
    mlane = lax.broadcasted_iota(jnp.int32, (n_blk, LANES), 1)
    n_iota = prow
    q_blk = pcol // MOBA_BLOCK
    key_off = lax.broadcasted_iota(jnp.int32, (MOBA_BLOCK, MOBA_BLOCK), 0)
    qry_off = lax.broadcasted_iota(jnp.int32, (MOBA_BLOCK, MOBA_BLOCK), 1)
    causal = key_off <= qry_off

    def prepare(pp):
        lanes = slice(pp * LANES, (pp + 1) * LANES)
        kp = k_ref[0, :, lanes]
        q_t = q_ref[0, lanes, :]
        v_t = v_ref[0, lanes, :]
        kmean = jnp.dot(block_mean, kp, preferred_element_type=F32)
        terms = []
        for hd in range(HEADS_PER_PAIR):
            rest = jnp.where(mlane // HEAD_DIM == hd, kmean, 0.0)
            for _ in range(3):
                part = rest.astype(BF16).astype(F32)
                terms.append(part)
                rest = rest - part
        terms = jnp.concatenate(terms, axis=0).astype(BF16)
        half = s_len // 2
        gates = jnp.concatenate(
            [jnp.dot(terms, q_t[:, :half], preferred_element_type=F32),
             jnp.dot(terms, q_t[:, half:], preferred_element_type=F32)], axis=1)
        for hd in range(HEADS_PER_PAIR):
            h = pp * HEADS_PER_PAIR + hd
            kaug_scr[h] = jnp.where(klane // HEAD_DIM == hd, kp, kext_ref[h])
            slope2 = slopes_ref[first_head + h] * LOG2E
            g_hi, g_mid, g_lo = (gates[(3 * hd + t) * n_blk:(3 * hd + t + 1) * n_blk]
                                 for t in range(3))
            g_t = g_hi + g_mid + g_lo
            rank = jnp.zeros((n_blk, s_len), jnp.int32)
            for m in range(n_blk):
                gm = g_t[m:m + 1, :]
                beats = (gm > g_t) | ((gm == g_t) & (m < n_iota))
                rank = rank + jnp.where(beats & (m < q_blk), 1, 0)
            sel = ((n_iota < q_blk) & (rank < MOBA_TOPK)) | (n_iota == q_blk)
            bias = _split_bf16(slope2 * ((n_iota - q_blk) * MOBA_BLOCK).astype(F32))
            selb = [jnp.where(sel, part, NEG_BIG if t == 0 else 0.0)
                    for t, part in enumerate(bias)]
            flag = jnp.where(n_iota < BIAS_TERMS, 1.0, 0.0)
            pad = jnp.zeros((HEAD_DIM - (BIAS_TERMS + 1) * n_blk, s_len), F32)
            ext = jnp.concatenate(selb + [flag, pad], axis=0)
            own = slice(hd * HEAD_DIM, (hd + 1) * HEAD_DIM)
            other = slice((1 - hd) * HEAD_DIM, (2 - hd) * HEAD_DIM)
            qaug_scr[h, own, :] = q_t[own]
            qaug_scr[h, other, :] = ext.astype(BF16)
            vaug_scr[h, 0:HEAD_DIM, :] = v_t[own]
            vaug_scr[h, HEAD_DIM:, :] = jnp.ones((2 * SUBLANES, s_len), BF16)

    items = [(h, qb) for h in range(n_heads)
             for qb in (range(n_blk) if h % 2 == 0 else reversed(range(n_blk)))]
    per_pair = HEADS_PER_PAIR * n_blk

    def scores(i):
        h, qb = items[i]
        slot = i % SCORE_SLOTS
        q0, kk = qb * MOBA_BLOCK, (qb + 1) * MOBA_BLOCK
        qa = qaug_scr[h, :, q0:kk]
        m = None
        for r0 in range(0, kk, MOBA_BLOCK):
            s = jnp.dot(kaug_scr[h, r0:r0 + MOBA_BLOCK, :], qa, preferred_element_type=F32)
            if r0 == q0:
                s = jnp.where(causal, s, NEG_BIG)
            s_scr[slot, r0:r0 + MOBA_BLOCK, :] = s
            mh = jnp.max(s.reshape(MOBA_BLOCK // SUBLANES, SUBLANES, MOBA_BLOCK), axis=0)
            m = mh if m is None else jnp.maximum(m, mh)
        return jnp.max(m, axis=0, keepdims=True)

    def probs(i, m):
        _, qb = items[i]
        kk = (qb + 1) * MOBA_BLOCK
        p_scr[i % 2, 0:kk, :] = jnp.exp2(s_scr[i % SCORE_SLOTS, 0:kk, :] - m).astype(BF16)

    def weighted(i):
        h, qb = items[i]
        slot = i % 2
        q0, kk = qb * MOBA_BLOCK, (qb + 1) * MOBA_BLOCK
        o = None
        for r0, r1 in ((0, kk // 2), (kk // 2, kk)):
            part = jnp.dot(vaug_scr[h, :, r0:r1], p_scr[slot, r0:r1, :], preferred_element_type=F32)
            o = part if o is None else o + part
        inv = 1.0 / o[HEAD_DIM:HEAD_DIM + 1]
        out_scr[h * HEAD_DIM:(h + 1) * HEAD_DIM, q0:kk] = o[0:HEAD_DIM] * inv

    lru_units = []
    for sl in range(n_slabs):
        lanes = slice(sl * LANES, (sl + 1) * LANES)

        def store_h(value, lanes=lanes):
            ol_ref[0, :, lanes] = value

        lru_units += _lru_units(
            xl_refs[sl].at[0], cw_ref[:, lanes], cb_ref[:, lanes], wg_ref.at[sl], ab_ref[:, lanes],
            xb_ref[:, lanes], lam_ref[:, lanes], head_scr.at[sl], h_scr.at[sl], xc_scr.at[sl],
            g_scr.at[sl], store_h)
    lru_done = 0

    n_items = len(items)
    prepare(0)
    col_max = {i: scores(i) for i in range(SCORE_SLOTS)}
    probs(0, col_max.pop(0))
    for i in range(n_items):
        pp, within = divmod(i, per_pair)
        if within == PREPARE_AT and pp + 1 < n_pairs:
            prepare(pp + 1)
        if i + SCORE_SLOTS < n_items:
            col_max[i + SCORE_SLOTS] = scores(i + SCORE_SLOTS)
        if i + 1 < n_items:
            probs(i + 1, col_max.pop(i + 1))
        weighted(i)
        if within == per_pair - 1:
            lanes = slice(pp * LANES, (pp + 1) * LANES)
            o_ref[0, :, lanes] = out_scr[lanes, :].T.astype(o_ref.dtype)
        lru_target = (i + 1) * len(lru_units) // n_items
        while lru_done < lru_target:
            lru_units[lru_done]()
            lru_done += 1


def _alibi_slopes():
    return np.exp2(-8.0 * (np.arange(N_HEADS, dtype=np.float32) + 1.0) / N_HEADS).astype(np.float32)


def _key_side_table(s_len):
    n_blk = s_len // MOBA_BLOCK
    head = np.arange(N_HEADS)[:, None, None]
    pos = np.arange(s_len)[None, :, None]
    lane = np.arange(LANES)[None, None, :]
    sub = lane % HEAD_DIM
    foreign = lane // HEAD_DIM != head % HEADS_PER_PAIR
    onehot = (sub % n_blk == pos // MOBA_BLOCK).astype(np.float32)
    off = ((_alibi_slopes() * np.float32(LOG2E))[:, None, None]
           * (pos % MOBA_BLOCK).astype(np.float32)).astype(np.float32)
    ext = np.where(sub < BIAS_TERMS * n_blk, onehot, 0.0).astype(np.float32)
    rest = off
    for t in range(BIAS_TERMS):
        part = (rest.view(np.uint32) & np.uint32(0xFFFF0000)).view(np.float32) if t + 1 < BIAS_TERMS else rest
        ext = np.where(sub == BIAS_TERMS * n_blk + t, part, ext)
        rest = rest - part
    table = np.where(foreign, ext, 0.0).astype(np.float32)
    return jnp.asarray(table.astype(BF16))


def _mixer_call(slopes, kext, q_t, k, v_t, xl, conv_w, conv_b, w_gate, a_b, x_b, lam):
    b, s, _ = k.shape
    pairs = ATTN_PAIRS_PER_STEP
    width = pairs * LANES
    heads = pairs * HEADS_PER_PAIR
    row = lambda a: a.reshape(1, D_LRU)
    spec = pl.BlockSpec((1, s, width), lambda j, i, sl: (i, 0, j))
    spec_t = pl.BlockSpec((1, width, s), lambda j, i, sl: (i, j, 0))
    vec = pl.BlockSpec((1, width), lambda j, i, sl: (0, j))
    slab_specs = [pl.BlockSpec((1, s, LANES), lambda j, i, sl, n=n: (i, 0, j * pairs + n))
                  for n in range(pairs)]
    sub_len = LRU_SEGMENT * SUBLANES
    return pl.pallas_call(
        _mixer_kernel,
        grid_spec=pltpu.PrefetchScalarGridSpec(
            num_scalar_prefetch=1,
            grid=(D_ATTN // width, b),
            in_specs=[spec_t, spec, spec_t,
                      pl.BlockSpec((heads, s, LANES), lambda j, i, sl: (j, 0, 0),
                                   pipeline_mode=pl.Buffered(1))]
            + slab_specs
            + [pl.BlockSpec((CONV_WIDTH, width), lambda j, i, sl: (0, j)), vec,
               pl.BlockSpec((pairs, LANES, 2 * LANES), lambda j, i, sl: (j, 0, 0)), vec, vec, vec],
            out_specs=[spec, spec],
            scratch_shapes=[
                pltpu.VMEM((heads, s, LANES), BF16),
                pltpu.VMEM((heads, LANES, s), BF16),
                pltpu.VMEM((heads, HEAD_DIM + 2 * SUBLANES, s), BF16),
                pltpu.VMEM((SCORE_SLOTS, s, MOBA_BLOCK), F32),
                pltpu.VMEM((2, s, MOBA_BLOCK), BF16),
                pltpu.VMEM((width, s), F32),
                pltpu.VMEM((pairs, SUBLANES + sub_len, LANES), F32),
                pltpu.VMEM((pairs, s, LANES), F32),
                pltpu.VMEM((pairs, 2, LRU_CHUNK, LANES), F32),
                pltpu.VMEM((pairs, 2, LRU_CHUNK, 2 * LANES), F32),
            ],
        ),
        out_shape=[jax.ShapeDtypeStruct((b, s, D_ATTN), BF16), jax.ShapeDtypeStruct((b, s, D_LRU), BF16)],
        compiler_params=pltpu.CompilerParams(
            dimension_semantics=("arbitrary", "arbitrary"), vmem_limit_bytes=VMEM_LIMIT_BYTES),
        name="mixer",
    )(slopes, q_t, k, v_t, kext, *([xl] * pairs), conv_w, row(conv_b), w_gate, row(a_b), row(x_b),
      row(lam))


def _lru_units(x_seq, cw, cb, w_gate, a_b, x_b, lam, head_scr, h_scr, xc_scr, g_scr, store_h):
    s_len = x_seq.shape[0]
    seg = LRU_SEGMENT
    sub_len = seg * SUBLANES
    n_sub = LRU_CHUNK // sub_len
    n_chunks = s_len // LRU_CHUNK

    neg_lam = -lam
    softplus = jnp.maximum(neg_lam, 0.0) + jnp.log1p(jnp.exp(-jnp.abs(neg_lam)))
    half_rate = (0.5 * RG_C) * softplus
    rate_log2 = -LOG2E * half_rate
    ab_half = 0.5 * a_b
    xb_half = 0.5 * x_b
    cw_half = 0.5 * cw
    cb_half = 0.5 * cb
    sub = lax.broadcasted_iota(jnp.int32, (SUBLANES, LANES), 0)
    steps = (1, 2, 4)
    in_vreg = [sub >= step for step in steps]
    state = {"h": jnp.zeros((SUBLANES, LANES), F32)}

    def strided(ref, start):
        return ref[pl.ds(start, SUBLANES, stride=seg), :]

    def init():
        head_scr[0:SUBLANES, :] = jnp.zeros((SUBLANES, LANES), F32)
        head_scr[SUBLANES:, :] = x_seq[0:sub_len, :]

    def conv_and_gates(ci):
        xc_parts = []
        for c in range(n_sub):
            if ci == 0 and c == 0:
                src, t0 = head_scr, SUBLANES
            else:
                src, t0 = x_seq, ci * LRU_CHUNK
            cache = {}
            for g in range(seg):
                acc = cb_half
                for j in range(CONV_WIDTH):
                    off = c * sub_len + g - (CONV_WIDTH - 1) + j
                    if off not in cache:
                        cache[off] = strided(src, t0 + off)
                    acc = acc + cw_half[j:j + 1, :] * cache[off]
                xc_parts.append(acc)
        hx = jnp.concatenate(xc_parts, axis=0)
        xc_scr[ci % 2] = hx
        g_scr[ci % 2] = jnp.dot(hx.astype(BF16), w_gate[...], preferred_element_type=F32)

    def recurrence(ci, c):
        t0 = ci * LRU_CHUNK
        rows = slice(c * sub_len, (c + 1) * sub_len)
        h_in = state["h"]
        hx = xc_scr[ci % 2, rows, :]
        t_r = jnp.tanh(g_scr[ci % 2, rows, :LANES] + ab_half)
        t_i = jnp.tanh(g_scr[ci % 2, rows, LANES:] + xb_half)
        two_r = t_r + 1.0
        neg_log_a = half_rate * two_r
        a = jnp.exp2(rate_log2 * two_r)
        y = jnp.tanh(neg_log_a) * (a * a + 1.0)
        root = jnp.where(y == 0.0, 0.0, y * lax.rsqrt(y))
        bterm = root * ((t_i + 1.0) * hx)

        comp = []
        for g in range(seg):
            av, bv = a[g * SUBLANES:(g + 1) * SUBLANES], bterm[g * SUBLANES:(g + 1) * SUBLANES]
            if g > 0:
                bv = av * comp[-1][1] + bv
                av = av * comp[-1][0]
            comp.append((av, bv))
        pa, pb = comp[-1]
        for step, valid in zip(steps, in_vreg):
            a_sh = jnp.where(valid, pltpu.roll(pa, step, axis=0), 1.0)
            b_sh = jnp.where(valid, pltpu.roll(pb, step, axis=0), 0.0)
            pb = pa * b_sh + pb
            pa = pa * a_sh
        ea = jnp.where(in_vreg[0], pltpu.roll(pa, 1, axis=0), 1.0)
        eb = jnp.where(in_vreg[0], pltpu.roll(pb, 1, axis=0), 0.0)
        seg_in = ea * h_in + eb
        for g in range(seg):
            h_g = comp[g][0] * seg_in + comp[g][1]
            h_scr[pl.ds(t0 + c * sub_len + g, SUBLANES, stride=seg), :] = h_g
        h_last = pa * h_in + pb
        state["h"] = jnp.broadcast_to(h_last[SUBLANES - 1:SUBLANES, :], (SUBLANES, LANES))

    def finish():
        store_h(h_scr[...].astype(BF16))

    units = [init, lambda: conv_and_gates(0)]
    for ci in range(n_chunks):
        for c in range(n_sub):
            units.append(lambda ci=ci, c=c: recurrence(ci, c))
            if c == n_sub // 2 - 1 and ci + 1 < n_chunks:
                units.append(lambda ci=ci: conv_and_gates(ci + 1))
    units.append(finish)
    return units


def _gated_norm(y_ref, z_ref, nw_ref):
    y = y_ref[0].astype(F32)
    ms = jnp.mean(y * y, axis=-1, keepdims=True)
    return ((y * lax.rsqrt(ms + EPS) * nw_ref[...]) * _silu(z_ref[0].astype(F32))).astype(BF16)


def _outproj_math(ya_ref, za_ref, yl_ref, zl_ref, x_ref, mod_ref, anw_ref, lnw_ref, w_ref):
    y = jnp.dot(_gated_norm(ya_ref, za_ref, anw_ref), w_ref[0:D_ATTN, :], preferred_element_type=F32)
    y = y + jnp.dot(_gated_norm(yl_ref, zl_ref, lnw_ref), w_ref[D_ATTN:, :],
                    preferred_element_type=F32)
    return x_ref[0] + mod_ref[0, 2:3, :] * y


def _outproj_kernel(ya_ref, za_ref, yl_ref, zl_ref, x_ref, mod_ref, anw_ref, lnw_ref, wf_ref, fnw_ref,
                    o_ref, w_ref):
    _cast_weight_once(wf_ref, w_ref)
    out = _outproj_math(ya_ref, za_ref, yl_ref, zl_ref, x_ref, mod_ref, anw_ref, lnw_ref, w_ref)
    ms = jnp.mean(out * out, axis=-1, keepdims=True)
    o_ref[0] = out * lax.rsqrt(ms + EPS) * fnw_ref[...]


def _outin_kernel(ya_ref, za_ref, yl_ref, zl_ref, x_ref, mod_ref, anw_ref, lnw_ref, wof_ref,
                  modn_ref, nwn_ref, wif_ref,
                  xo_ref, q_ref, k_ref, v_ref, zao_ref, xlo_ref, zlo_ref, wo_ref, wi_ref):
    _cast_weight_once(wof_ref, wo_ref)
    _cast_weight_once(wif_ref, wi_ref)
    out = _outproj_math(ya_ref, za_ref, yl_ref, zl_ref, x_ref, mod_ref, anw_ref, lnw_ref, wo_ref)
    xo_ref[0] = out
    _inproj_math(out, modn_ref, nwn_ref, wi_ref, (q_ref, k_ref, v_ref, zao_ref, xlo_ref, zlo_ref))


def _proj_specs(d, layer, tm=ROW_TILE):
    return dict(
        half=pl.BlockSpec((1, tm, D_ATTN), lambda i, j: (i, j, 0)),
        full=pl.BlockSpec((1, tm, d), lambda i, j: (i, j, 0)),
        mod=pl.BlockSpec((1, 3, d), lambda i, j: (i, 0, 0)),
        nvec=pl.BlockSpec((1, D_ATTN), lambda i, j: (0, 0)),
        dvec=pl.BlockSpec((1, d), lambda i, j: (0, 0)),
        w_out=pl.BlockSpec((1, d, d), lambda i, j: (layer, 0, 0), pipeline_mode=pl.Buffered(1)),
        w_in=pl.BlockSpec((1, d, D_IN), lambda i, j: (layer + 1, 0, 0), pipeline_mode=pl.Buffered(1)),
    )


def _outproj_call(ya, za, yl, zl, x, mod3, anw, lnw, w_out, layer, fnw):
    b, s, d = x.shape
    sp = _proj_specs(d, layer, OUT_ROW_TILE)
    return pl.pallas_call(
        _outproj_kernel,
        grid=(b, s // OUT_ROW_TILE),
        in_specs=[sp["half"]] * 4 + [sp["full"], sp["mod"], sp["nvec"], sp["nvec"], sp["w_out"],
                                     sp["dvec"]],
        out_specs=sp["full"],
        out_shape=jax.ShapeDtypeStruct((b, s, d), F32),
        scratch_shapes=[pltpu.VMEM((d, d), BF16)],
        compiler_params=pltpu.CompilerParams(
            dimension_semantics=("arbitrary", "arbitrary"), vmem_limit_bytes=VMEM_LIMIT_BYTES),
        name="outproj",
    )(ya, za, yl, zl, x, mod3, anw.reshape(1, D_ATTN), lnw.reshape(1, D_LRU), w_out,
      fnw.reshape(1, d))


def _outin_call(ya, za, yl, zl, x, mod3, anw, lnw, w_out, layer, mod3_next, norm_w_next, w_in):
    b, s, d = x.shape
    sp = _proj_specs(d, layer)
    proj_specs, proj_shapes = _inproj_outputs(b, s)
    return pl.pallas_call(
        _outin_kernel,
        grid=(b, s // ROW_TILE),
        in_specs=[sp["half"]] * 4 + [sp["full"], sp["mod"], sp["nvec"], sp["nvec"], sp["w_out"],
                                     sp["mod"], sp["dvec"], sp["w_in"]],
        out_specs=[sp["full"]] + proj_specs,
        out_shape=[jax.ShapeDtypeStruct((b, s, d), F32)] + proj_shapes,
        scratch_shapes=[pltpu.VMEM((d, d), BF16), pltpu.VMEM((d, D_IN), BF16)],
        compiler_params=pltpu.CompilerParams(
            dimension_semantics=("arbitrary", "arbitrary"), vmem_limit_bytes=PROJ_VMEM_LIMIT_BYTES),
        name="outproj_inproj",
    )(ya, za, yl, zl, x, mod3, anw.reshape(1, D_ATTN), lnw.reshape(1, D_LRU), w_out,
      mod3_next, norm_w_next.reshape(1, d), w_in)


def _gate_weights(a_w, x_w):
    per_slab = LANES // LRU_BLOCK
    slabs = []
    for j in range(D_LRU // LANES):
        blocks = range(j * per_slab, (j + 1) * per_slab)
        diag = lambda w: jax.scipy.linalg.block_diag(*[w[g] for g in blocks])
        slabs.append(jnp.concatenate([diag(a_w), diag(x_w)], axis=1))
    return jnp.stack(slabs).astype(BF16)


def kernel(x, c, norm_w, ada_w, ada_b, w_in, conv_w, conv_b, rg_a_w, rg_a_b, rg_x_w, rg_x_b,
           rg_lambda, attn_out_norm_w, lru_out_norm_w, w_out, final_norm_w):
    n_layers = ada_w.shape[0]
    b = x.shape[0]
    mod = _mod_call(c, ada_w, ada_b)
    slopes = jnp.asarray(_alibi_slopes())
    kext = _key_side_table(x.shape[1])
    mod3 = [mod[l].reshape(b, 3, D_MODEL) for l in range(n_layers)]
    q, k, v, za, xl, zl = _inproj_call(x, mod3[0], norm_w[0], w_in, 0)
    for l in range(n_layers):
        ya, yl = _mixer_call(slopes, kext, q, k, v, xl, conv_w[l], conv_b[l],
                             _gate_weights(rg_a_w[l], rg_x_w[l]), rg_a_b[l], rg_x_b[l], rg_lambda[l])
        if l + 1 < n_layers:
            x, q, k, v, za, xl, zl = _outin_call(
                ya, za, yl, zl, x, mod3[l], attn_out_norm_w[l], lru_out_norm_w[l], w_out, l,
                mod3[l + 1], norm_w[l + 1], w_in)
        else:
            x = _outproj_call(ya, za, yl, zl, x, mod3[l], attn_out_norm_w[l], lru_out_norm_w[l],
                              w_out, l, final_norm_w)
    return x
```
